```python
import math
import jax, jax.numpy as jnp
from jax import lax
import numpy as np

D_MODEL = 1024
BATCH = 8
SEQ = 2048
DEPTH = 2
DEC_BATCH = 128
DEC_SEQ = 8
PAST_LEN = 16384
PAGE_SIZE = 128

D_INNER = 2 * D_MODEL
SSD_HEAD_DIM = 64
SSD_HEADS = D_INNER // SSD_HEAD_DIM
SSD_GROUPS = 8
SSD_HPG = SSD_HEADS // SSD_GROUPS
D_STATE = 128
CONV_W = 4
CONV_DIM = D_INNER + 2 * SSD_GROUPS * D_STATE
SSD_IN_DIM = D_INNER + CONV_DIM + SSD_HEADS
CHUNK = 128
POOL_WINDOWS = (2, 4, 8, 16)
N_POOL_GROUPS = len(POOL_WINDOWS)
D_POOL = D_MODEL
POOL_GROUP_DIM = D_POOL // N_POOL_GROUPS
MAX_WIN = max(POOL_WINDOWS)
D_FF = 2816
N_SSD_LAYERS = (DEPTH + 1) // 2
N_POOL_LAYERS = DEPTH // 2
EPS = 1e-6

kernel_name = 'ssd_pool_macaron_decode'


def rmsnorm(x, g):
    xf = x.astype(jnp.float32)
    y = xf * lax.rsqrt(jnp.mean(xf * xf, axis=-1, keepdims=True) + EPS)
    return (y * g.astype(jnp.float32)).astype(x.dtype)


def swiglu(h, w_gate, w_up, w_down):
    return (jax.nn.silu(h @ w_gate) * (h @ w_up)) @ w_down


def causal_dwconv(u, buf, w, b):
    l = u.shape[1]
    ext = jnp.concatenate([buf.astype(u.dtype), u], axis=1)
    y = b
    for k in range(CONV_W):
        y = y + ext[:, k:k + l] * w[k]
    return y, ext[:, l:]


def ssd_scan(x, dt, A, Bm, Cm, h0):
    f32 = jnp.float32
    bsz, l = x.shape[:2]
    q = min(CHUNK, l)
    nc = -(-l // q)
    pad = nc * q - l
    if pad:
        padw = lambda a: jnp.pad(a, [(0, 0), (0, pad)] + [(0, 0)] * (a.ndim - 2))
        x, dt, Bm, Cm = padw(x), padw(dt), padw(Bm), padw(Cm)
    xc = x.reshape(bsz, nc, q, SSD_GROUPS, SSD_HPG, SSD_HEAD_DIM).astype(f32)
    dtc = dt.reshape(bsz, nc, q, SSD_GROUPS, SSD_HPG).astype(f32)
    Bc = Bm.reshape(bsz, nc, q, SSD_GROUPS, D_STATE).astype(f32)
    Cc = Cm.reshape(bsz, nc, q, SSD_GROUPS, D_STATE).astype(f32)
    a_cum = jnp.cumsum(dtc * A.reshape(SSD_GROUPS, SSD_HPG), axis=2)
    seg = a_cum[:, :, :, None] - a_cum[:, :, None]
    tril = jnp.tril(jnp.ones((q, q), dtype=bool))[:, :, None, None]
    decay = jnp.exp(jnp.where(tril, seg, -jnp.inf))
    xdt = xc * dtc[..., None]
    cb = jnp.einsum('bclgn,bcsgn->bclsg', Cc, Bc)
    y_diag = jnp.einsum('bclsgr,bcsgrp->bclgrp', cb[..., None] * decay, xdt)
    decay_end = jnp.exp(a_cum[:, :, -1:] - a_cum)
    chunk_states = jnp.einsum('bcsgn,bcsgr,bcsgrp->bcgrpn', Bc, decay_end, xdt)
    chunk_decay = jnp.exp(a_cum[:, :, -1])

    def step(h, inp):
        st, dec = inp
        return h * dec[..., None, None] + st, h

    h_init = h0.reshape(bsz, SSD_GROUPS, SSD_HPG, SSD_HEAD_DIM, D_STATE).astype(f32)
    h_final, h_starts = lax.scan(step, h_init,
                                 (jnp.moveaxis(chunk_states, 1, 0), jnp.moveaxis(chunk_decay, 1, 0)))
    h_starts = jnp.moveaxis(h_starts, 0, 1)
    y_off = jnp.einsum('bclgn,bclgr,bcgrpn->bclgrp', Cc, jnp.exp(a_cum), h_starts)
    y = (y_diag + y_off).reshape(bsz, nc * q, SSD_HEADS, SSD_HEAD_DIM)[:, :l]
    return y, h_final.reshape(bsz, SSD_HEADS, SSD_HEAD_DIM, D_STATE)


def ssd_mixer(h, ssm0, conv0, w_in, conv_w, conv_b, dt_bias, a_log, d_skip, norm_g, w_out):
    f32 = jnp.float32
    bsz, l, _ = h.shape
    proj = h @ w_in
    z, xbc, dt_raw = jnp.split(proj, [D_INNER, D_INNER + CONV_DIM], axis=-1)
    xbc, conv_new = causal_dwconv(xbc, conv0, conv_w, conv_b)
    xbc = jax.nn.silu(xbc)
    xs, Bm, Cm = jnp.split(xbc, [D_INNER, D_INNER + SSD_GROUPS * D_STATE], axis=-1)
    xs = xs.reshape(bsz, l, SSD_HEADS, SSD_HEAD_DIM)
    Bm = Bm.reshape(bsz, l, SSD_GROUPS, D_STATE)
    Cm = Cm.reshape(bsz, l, SSD_GROUPS, D_STATE)
    dt = jax.nn.softplus((dt_raw + dt_bias).astype(f32))
    A = -jnp.exp(a_log.astype(f32))
    y, ssm_new = ssd_scan(xs, dt, A, Bm, Cm, ssm0)
    y = y + d_skip.astype(f32)[:, None] * xs.astype(f32)
    y = y.reshape(bsz, l, D_INNER) * jax.nn.silu(z.astype(f32))
    yg = y.reshape(bsz, l, SSD_GROUPS, D_INNER // SSD_GROUPS)
    yg = yg * lax.rsqrt(jnp.mean(yg * yg, axis=-1, keepdims=True) + EPS)
    y = (yg.reshape(bsz, l, D_INNER) * norm_g.astype(f32)).astype(h.dtype)
    return y @ w_out, ssm_new.astype(ssm0.dtype), conv_new.astype(conv0.dtype)


def pool_mixer(h, buf, pos0, w_in, w_group, scale, w_out):
    f32 = jnp.float32
    bsz, l, _ = h.shape
    u = h @ w_in
    ext = jnp.concatenate([buf.astype(u.dtype), u], axis=1)
    cs = jnp.cumsum(ext.astype(f32), axis=1)
    cs = jnp.concatenate([jnp.zeros((bsz, 1, D_POOL), f32), cs], axis=1)
    end = cs[:, MAX_WIN:MAX_WIN + l]
    pos = (pos0 + jnp.arange(l)).astype(f32)
    outs = []
    for k, w in enumerate(POOL_WINDOWS):
        sl = slice(k * POOL_GROUP_DIM, (k + 1) * POOL_GROUP_DIM)
        count = jnp.minimum(jnp.float32(w), pos + 1.0)[None, :, None]
        mean = (end[..., sl] - cs[:, MAX_WIN - w:MAX_WIN - w + l, sl]) / count
        outs.append(mean - u[..., sl].astype(f32))
    mixed = jnp.stack(outs, axis=2).astype(u.dtype)
    mixed = jnp.einsum('blgc,gcd->blgd', mixed, w_group).reshape(bsz, l, D_POOL)
    return (mixed * scale) @ w_out, ext[:, l:].astype(buf.dtype)


def trunk(x, ssm_st, conv_st, pool_st, pos0,
          ffn_norm, ffn_w_gate, ffn_w_up, ffn_w_down, mix_norm,
          ssd_w_in, ssd_conv_w, ssd_conv_b, ssd_dt_bias, ssd_a_log, ssd_d, ssd_norm, ssd_w_out,
          pool_w_in, pool_w_group, pool_scale, pool_w_out, final_norm):
    new_ssm, new_conv, new_pool = [], [], []
    for i in range(DEPTH):
        x = x + 0.5 * swiglu(rmsnorm(x, ffn_norm[i, 0]), ffn_w_gate[i, 0], ffn_w_up[i, 0], ffn_w_down[i, 0])
        h = rmsnorm(x, mix_norm[i])
        j = i // 2
        if i % 2 == 0:
            m, s_new, c_new = ssd_mixer(h, ssm_st[j], conv_st[j], ssd_w_in[j], ssd_conv_w[j], ssd_conv_b[j],
                                        ssd_dt_bias[j], ssd_a_log[j], ssd_d[j], ssd_norm[j], ssd_w_out[j])
            new_ssm.append(s_new)
            new_conv.append(c_new)
        else:
            m, p_new = pool_mixer(h, pool_st[j], pos0, pool_w_in[j], pool_w_group[j], pool_scale[j], pool_w_out[j])
            new_pool.append(p_new)
        x = x + m
        x = x + 0.5 * swiglu(rmsnorm(x, ffn_norm[i, 1]), ffn_w_gate[i, 1], ffn_w_up[i, 1], ffn_w_down[i, 1])
    return rmsnorm(x, final_norm), jnp.stack(new_ssm), jnp.stack(new_conv), jnp.stack(new_pool)


def setup_inputs(seed: int = 0) -> dict:
    key = jax.random.key(seed)
    ks = jax.random.split(key, 24)
    f32 = jnp.float32

    def nrm(k, shape, scale):
        return jax.random.normal(k, shape, f32) * scale

    dt0 = jnp.exp(jax.random.uniform(ks[13], (N_SSD_LAYERS, SSD_HEADS), f32, math.log(1e-3), math.log(0.1)))
    return {
        'x_prompt': nrm(ks[0], (BATCH, SEQ, D_MODEL), 1.0),
        'x_sample': nrm(ks[1], (DEC_BATCH, DEC_SEQ, D_MODEL), 1.0),
        'state_ssm': nrm(ks[2], (N_SSD_LAYERS, DEC_BATCH, SSD_HEADS, SSD_HEAD_DIM, D_STATE), 0.1),
        'state_conv': nrm(ks[3], (N_SSD_LAYERS, DEC_BATCH, CONV_W - 1, CONV_DIM), 1.0),
        'state_pool': nrm(ks[4], (N_POOL_LAYERS, DEC_BATCH, MAX_WIN - 1, D_POOL), 1.0),
        'ffn_norm': 1.0 + nrm(ks[5], (DEPTH, 2, D_MODEL), 0.02),
        'ffn_w_gate': nrm(ks[6], (DEPTH, 2, D_MODEL, D_FF), D_MODEL ** -0.5),
        'ffn_w_up': nrm(ks[7], (DEPTH, 2, D_MODEL, D_FF), D_MODEL ** -0.5),
        'ffn_w_down': nrm(ks[8], (DEPTH, 2, D_FF, D_MODEL), D_FF ** -0.5),
        'mix_norm': 1.0 + nrm(ks[9], (DEPTH, D_MODEL), 0.02),
        'ssd_w_in': nrm(ks[10], (N_SSD_LAYERS, D_MODEL, SSD_IN_DIM), D_MODEL ** -0.5),
        'ssd_conv_w': nrm(ks[11], (N_SSD_LAYERS, CONV_W, CONV_DIM), CONV_W ** -0.5),
        'ssd_conv_b': nrm(ks[12], (N_SSD_LAYERS, CONV_DIM), 0.02),
        'ssd_dt_bias': dt0 + jnp.log(-jnp.expm1(-dt0)),
        'ssd_a_log': jnp.log(jax.random.uniform(ks[14], (N_SSD_LAYERS, SSD_HEADS), f32, 1.0, 16.0)),
        'ssd_d': 1.0 + nrm(ks[15], (N_SSD_LAYERS, SSD_HEADS), 0.1),
        'ssd_norm': 1.0 + nrm(ks[16], (N_SSD_LAYERS, D_INNER), 0.02),
        'ssd_w_out': nrm(ks[17], (N_SSD_LAYERS, D_INNER, D_MODEL), D_INNER ** -0.5),
        'pool_w_in': nrm(ks[18], (N_POOL_LAYERS, D_MODEL, D_POOL), D_MODEL ** -0.5),
        'pool_w_group': nrm(ks[19], (N_POOL_LAYERS, N_POOL_GROUPS, POOL_GROUP_DIM, POOL_GROUP_DIM), POOL_GROUP_DIM ** -0.5),
        'pool_scale': 1.0 + nrm(ks[20], (N_POOL_LAYERS, D_POOL), 0.1),
        'pool_w_out': nrm(ks[21], (N_POOL_LAYERS, D_POOL, D_MODEL), D_POOL ** -0.5),
        'final_norm': 1.0 + nrm(ks[22], (D_MODEL,), 0.02),
    }


def reference(x_prompt, x_sample, state_ssm, state_conv, state_pool,
              ffn_norm, ffn_w_gate, ffn_w_up, ffn_w_down, mix_norm,
              ssd_w_in, ssd_conv_w, ssd_conv_b, ssd_dt_bias, ssd_a_log, ssd_d, ssd_norm, ssd_w_out,
              pool_w_in, pool_w_group, pool_scale, pool_w_out, final_norm):
    bp = x_prompt.shape[0]
    ssm0 = jnp.zeros((N_SSD_LAYERS, bp, SSD_HEADS, SSD_HEAD_DIM, D_STATE), state_ssm.dtype)
    conv0 = jnp.zeros((N_SSD_LAYERS, bp, CONV_W - 1, CONV_DIM), state_conv.dtype)
    pool0 = jnp.zeros((N_POOL_LAYERS, bp, MAX_WIN - 1, D_POOL), state_pool.dtype)
    y_prompt, ssm_p, conv_p, pool_p = trunk(
        x_prompt, ssm0, conv0, pool0, 0,
        ffn_norm, ffn_w_gate, ffn_w_up, ffn_w_down, mix_norm,
        ssd_w_in, ssd_conv_w, ssd_conv_b, ssd_dt_bias, ssd_a_log, ssd_d, ssd_norm, ssd_w_out,
        pool_w_in, pool_w_group, pool_scale, pool_w_out, final_norm)
    y_sample, ssm_s, conv_s, pool_s = trunk(
        x_sample, state_ssm, state_conv, state_pool, PAST_LEN,
        ffn_norm, ffn_w_gate, ffn_w_up, ffn_w_down, mix_norm,
        ssd_w_in, ssd_conv_w, ssd_conv_b, ssd_dt_bias, ssd_a_log, ssd_d, ssd_norm, ssd_w_out,
        pool_w_in, pool_w_group, pool_scale, pool_w_out, final_norm)
    return (y_prompt, y_sample, ssm_p, conv_p, pool_p, ssm_s, conv_s, pool_s)
```

```python
import functools

import numpy as np
import jax
import jax.numpy as jnp
from jax import lax
from jax.experimental import pallas as pl
from jax.experimental.pallas import tpu as pltpu

F32 = jnp.float32
BF16 = jnp.bfloat16

EPS = 1e-6
D_MODEL = 1024
D_FF = 2816
D_INNER = 2048
HEAD_DIM = 64
N_HEADS = 32
N_GROUPS = 8
HEADS_PER_GROUP = 4
GROUP_DIM = HEADS_PER_GROUP * HEAD_DIM
D_STATE = 128
CONV_W = 4
CONV_DIM = D_INNER + 2 * N_GROUPS * D_STATE
CHUNK = 128
POOL_WINDOWS = (2, 4, 8, 16)
POOL_GROUP_DIM = 256
MAX_WIN = 16
PAST_LEN = 16384
LANES = 128
SUBLANES = 8
DT_PAD = LANES
VMEM_LIMIT = 56 * 1024 * 1024

NT_DIMS = (((1,), (1,)), ((), ()))


def _resident(shape):
    nd = len(shape)
    return pl.BlockSpec(shape, lambda *_: (0,) * nd, pipeline_mode=pl.Buffered(1))


def _params(n_axes):
    return pltpu.CompilerParams(dimension_semantics=("arbitrary",) * n_axes,
                                vmem_limit_bytes=VMEM_LIMIT)


def _rmsnorm(x, g):
    ms = jnp.mean(x * x, axis=-1, keepdims=True)
    return x * lax.rsqrt(ms + EPS) * g


def _silu(x):
    return x * jax.nn.sigmoid(x)


def _softplus(x):
    return jnp.maximum(x, 0.0) + jnp.log(1.0 + jnp.exp(-jnp.abs(x)))


def _dot(a, b):
    return jnp.dot(a, b, preferred_element_type=F32)


def _dot_nt(a, b):
    return lax.dot_general(a, b, NT_DIMS, preferred_element_type=F32)


def _dot_exact01(t, x):
    hi = x.astype(BF16)
    r1 = x - hi.astype(F32)
    mid = r1.astype(BF16)
    lo = (r1 - mid.astype(F32)).astype(BF16)
    return _dot(t, hi) + _dot(t, mid) + _dot(t, lo)


FFN_TM = 512
FFN_TF = 256


def _ffn_kernel(*refs, final):
    if final:
        x_ref, g_ref, wg_ref, wu_ref, wd_ref, fg_ref, o_ref, h_ref, a_ref = refs
    else:
        x_ref, g_ref, wg_ref, wu_ref, wd_ref, o_ref, h_ref, a_ref = refs
    x = x_ref[...]
    h_ref[...] = _rmsnorm(x, g_ref[...]).astype(BF16)
    for f in range(0, D_FF, FFN_TF):
        h = h_ref[...]
        gate = _dot(h, wg_ref[:, f:f + FFN_TF])
        up = _dot(h, wu_ref[:, f:f + FFN_TF])
        a_ref[:, f:f + FFN_TF] = (_silu(gate) * up).astype(BF16)
    y = x + 0.5 * _dot(a_ref[...], wd_ref[...])
    if final:
        y = _rmsnorm(y, fg_ref[...])
    o_ref[...] = y


def _ffn(x2d, g, wg, wu, wd, final_g=None):
    m = x2d.shape[0]
    tm = min(FFN_TM, m)
    final = final_g is not None
    row_spec = pl.BlockSpec((tm, D_MODEL), lambda i: (i, 0))
    in_specs = [row_spec, _resident((1, D_MODEL)), _resident((D_MODEL, D_FF)),
                _resident((D_MODEL, D_FF)), _resident((D_FF, D_MODEL))]
    args = [x2d, g.reshape(1, D_MODEL), wg, wu, wd]
    if final:
        in_specs.append(_resident((1, D_MODEL)))
        args.append(final_g.reshape(1, D_MODEL))
    return pl.pallas_call(
        functools.partial(_ffn_kernel, final=final),
        grid=(m // tm,),
        in_specs=in_specs,
        out_specs=row_spec,
        out_shape=jax.ShapeDtypeStruct((m, D_MODEL), F32),
        scratch_shapes=[pltpu.VMEM((tm, D_MODEL), BF16), pltpu.VMEM((tm, D_FF), BF16)],
        compiler_params=_params(1),
        name="ffn_final" if final else "ffn",
    )(*args)


CONV_HALO = SUBLANES
SSD_IN_LANE_CHUNK = 512


def _ssd_in_kernel(x_ref, g_ref, wz_ref, wxbc_ref, wdt_ref, cw_ref, cb_ref, dtb_ref, conv0_ref,
                   z_ref, xbc_ref, dt_ref, convnew_ref, ext_ref, *, nb, tl):
    rows = nb * tl
    lo = CONV_HALO - (CONV_W - 1)

    @pl.when(pl.program_id(1) == 0)
    def _():
        ext_ref[:, 0:CONV_HALO, :] = conv0_ref[...]

    h = _rmsnorm(x_ref[...].reshape(rows, D_MODEL), g_ref[...]).astype(BF16)
    z_ref[...] = _dot(h, wz_ref[...]).reshape(nb, tl, D_INNER)
    dt_w = dt_ref.shape[-1]
    dt_ref[...] = _softplus(_dot(h, wdt_ref[...]) + dtb_ref[...]).reshape(nb, tl, dt_w)
    for c in range(0, CONV_DIM, SSD_IN_LANE_CHUNK):
        sl = slice(c, c + SSD_IN_LANE_CHUNK)
        ext_ref[:, CONV_HALO:CONV_HALO + tl, sl] = _dot(h, wxbc_ref[:, sl]).reshape(
            nb, tl, SSD_IN_LANE_CHUNK)
        acc = cb_ref[:, sl].reshape(1, 1, SSD_IN_LANE_CHUNK)
        for k in range(CONV_W):
            acc = acc + ext_ref[:, lo + k:lo + k + tl, sl] * cw_ref[k:k + 1, sl].reshape(
                1, 1, SSD_IN_LANE_CHUNK)
        xbc_ref[:, :, sl] = _silu(acc)
    convnew_ref[...] = ext_ref[:, tl + lo:tl + CONV_HALO, :]
    ext_ref[:, 0:CONV_HALO, :] = ext_ref[:, tl:tl + CONV_HALO, :]


def _ssd_in(x3d, g, wz, wxbc, wdt, cw, cb, dtb, conv0_pad, *, nb, tl):
    b, l, _ = x3d.shape
    dt_w = wdt.shape[1]
    grid = (b // nb, l // tl)
    blk = lambda w: pl.BlockSpec((nb, tl, w), lambda i, j: (i, j, 0))
    return pl.pallas_call(
        functools.partial(_ssd_in_kernel, nb=nb, tl=tl),
        grid=grid,
        in_specs=[blk(D_MODEL), _resident((1, D_MODEL)), _resident((D_MODEL, D_INNER)),
                  _resident((D_MODEL, CONV_DIM)), _resident((D_MODEL, dt_w)),
                  _resident((CONV_W, CONV_DIM)), _resident((1, CONV_DIM)), _resident((1, dt_w)),
                  pl.BlockSpec((nb, CONV_HALO, CONV_DIM), lambda i, j: (i, 0, 0))],
        out_specs=[blk(D_INNER), blk(CONV_DIM), blk(dt_w),
                   pl.BlockSpec((nb, CONV_W - 1, CONV_DIM), lambda i, j: (i, 0, 0))],
        out_shape=[jax.ShapeDtypeStruct((b, l, D_INNER), F32),
                   jax.ShapeDtypeStruct((b, l, CONV_DIM), F32),
                   jax.ShapeDtypeStruct((b, l, dt_w), F32),
                   jax.ShapeDtypeStruct((b, CONV_W - 1, CONV_DIM), F32)],
        scratch_shapes=[pltpu.VMEM((nb, CONV_HALO + tl, CONV_DIM), F32)],
        compiler_params=_params(2),
        name="ssd_in",
    )(x3d, g.reshape(1, D_MODEL), wz, wxbc, wdt, cw, cb.reshape(1, CONV_DIM),
      dtb.reshape(1, dt_w), conv0_pad)


B_OFF = D_INNER
C_OFF = D_INNER + N_GROUPS * D_STATE


def _gate_norm(y, z, ng):
    yg = y * _silu(z)
    ms = jnp.mean(yg * yg, axis=-1, keepdims=True)
    return yg * lax.rsqrt(ms + EPS) * ng


def _head_rows(mat, g):
    return jnp.concatenate(
        [jnp.broadcast_to(mat[g * HEADS_PER_GROUP + r:g * HEADS_PER_GROUP + r + 1, :],
                          (HEAD_DIM, mat.shape[1])) for r in range(HEADS_PER_GROUP)], axis=0)


def _scan_prompt_kernel(xbc_ref, z_ref, dt_ref, apad_ref, de_ref, ng_ref, y_ref, hout_ref, h_ref):
    q = CHUNK
    c = pl.program_id(1)

    @pl.when(c == 0)
    def _():
        h_ref[...] = jnp.zeros_like(h_ref)

    dt = dt_ref[0]
    a = dt * apad_ref[...]
    row = lax.broadcasted_iota(jnp.int32, (q, q), 0)
    col = lax.broadcasted_iota(jnp.int32, (q, q), 1)
    tril = row >= col
    acum = _dot_exact01(tril.astype(BF16), a)
    acum_t = acum.T
    dt_t = dt.T
    alast_t = acum_t[:, q - 1:q]
    cd_b = jnp.broadcast_to(jnp.exp(alast_t), (DT_PAD, D_STATE))
    w_t = jnp.exp(alast_t - acum_t) * dt_t
    lane_g = lax.broadcasted_iota(jnp.int32, (q, GROUP_DIM), 1)
    lane_half = lax.broadcasted_iota(jnp.int32, (q, LANES), 1) < HEAD_DIM

    for g in range(N_GROUPS):
        gs = slice(g * GROUP_DIM, (g + 1) * GROUP_DIM)
        bg = xbc_ref[0, :, B_OFF + g * D_STATE:B_OFF + (g + 1) * D_STATE].astype(BF16)
        cg = xbc_ref[0, :, C_OFF + g * D_STATE:C_OFF + (g + 1) * D_STATE].astype(BF16)
        xg = xbc_ref[0, :, gs]
        xgb = xg.astype(BF16)
        cb = _dot_nt(cg, bg)
        ms, cols = [], []
        for r in range(HEADS_PER_GROUP):
            hd = g * HEADS_PER_GROUP + r
            colf = jnp.broadcast_to(acum[:, hd:hd + 1], (q, q))
            rowf = jnp.broadcast_to(acum_t[hd:hd + 1, :], (q, q))
            decay = jnp.exp(jnp.where(tril, colf - rowf, -jnp.inf))
            ms.append((cb * decay * jnp.broadcast_to(dt_t[hd:hd + 1, :], (q, q))).astype(BF16))
            cols.append(colf)
        mcat = jnp.concatenate(ms, axis=1)
        zero = jnp.zeros_like(xgb)
        bd = jnp.concatenate(
            [jnp.where((lane_g >= r * HEAD_DIM) & (lane_g < (r + 1) * HEAD_DIM), xgb, zero)
             for r in range(HEADS_PER_GROUP)], axis=0)
        y_diag = _dot(mcat, bd)
        hg = h_ref[gs, :]
        y_off = _dot_nt(cg, hg.astype(BF16))
        acum_e = jnp.concatenate([jnp.where(lane_half, cols[0], cols[1]),
                                  jnp.where(lane_half, cols[2], cols[3])], axis=1)
        y = y_diag + y_off * jnp.exp(acum_e) + de_ref[:, gs] * xg
        xg_t = xg.T
        wt = jnp.concatenate(
            [xg_t[r * HEAD_DIM:(r + 1) * HEAD_DIM, :] * w_t[g * HEADS_PER_GROUP + r:
                                                            g * HEADS_PER_GROUP + r + 1, :]
             for r in range(HEADS_PER_GROUP)], axis=0).astype(BF16)
        h_ref[gs, :] = hg * _head_rows(cd_b, g) + _dot(wt, bg)
        y_ref[0, :, gs] = _gate_norm(y, z_ref[0, :, gs], ng_ref[:, gs]).astype(y_ref.dtype)

    @pl.when(c == pl.num_programs(1) - 1)
    def _():
        hout_ref[0] = h_ref[...]


def _scan_prompt(xbc, z, dt, a_pad, d_e, ng):
    b, l, _ = xbc.shape
    blk = lambda w: pl.BlockSpec((1, CHUNK, w), lambda i, j: (i, j, 0))
    return pl.pallas_call(
        _scan_prompt_kernel,
        grid=(b, l // CHUNK),
        in_specs=[blk(CONV_DIM), blk(D_INNER), blk(DT_PAD), _resident((1, DT_PAD)),
                  _resident((1, D_INNER)), _resident((1, D_INNER))],
        out_specs=[blk(D_INNER), pl.BlockSpec((1, D_INNER, D_STATE), lambda i, j: (i, 0, 0))],
        out_shape=[jax.ShapeDtypeStruct((b, l, D_INNER), BF16),
                   jax.ShapeDtypeStruct((b, D_INNER, D_STATE), F32)],
        scratch_shapes=[pltpu.VMEM((D_INNER, D_STATE), F32)],
        compiler_params=_params(2),
        name="ssd_scan_prompt",
    )(xbc, z, dt, a_pad, d_e, ng)


SAMPLE_LEN = 8
SCAN_NB = 16
SCAN_ROWS = SCAN_NB * SAMPLE_LEN


def _scan_sample_kernel(xbc_ref, z_ref, dt_ref, h0_ref, apad_ref, ae_ref, de_ref, ng_ref,
                        gsum_ref, gexp_ref, y_ref, hout_ref,
                        wt_ref, ea_ref, yd_ref, acp_ref, p_ref):
    rows = SCAN_ROWS
    j = pl.program_id(1)

    @pl.when(j == 0)
    def _():
        shape3 = lambda w: (SCAN_NB, SAMPLE_LEN, w)
        tok = lax.broadcasted_iota(jnp.int32, (rows, D_INNER), 0) & (SAMPLE_LEN - 1)
        tok_p = lax.broadcasted_iota(jnp.int32, (rows, DT_PAD), 0) & (SAMPLE_LEN - 1)

        def cumsum_tokens(v, t):
            for sh in (1, 2, 4):
                v = v + jnp.where(t >= sh, pltpu.roll(v, sh, 0), 0.0)
            return v

        def bcast_token(v, s):
            w = v.shape[-1]
            v3 = v.reshape(shape3(w))
            return jnp.broadcast_to(v3[:, s:s + 1, :], shape3(w)).reshape(rows, w)

        xs = xbc_ref[:, :, 0:D_INNER].reshape(rows, D_INNER)
        bm = xbc_ref[:, :, B_OFF:C_OFF].reshape(rows, N_GROUPS * D_STATE)
        cm = xbc_ref[:, :, C_OFF:CONV_DIM].reshape(rows, N_GROUPS * D_STATE)
        dt_p = dt_ref[:, :, 0:DT_PAD].reshape(rows, DT_PAD)
        dt_e = dt_ref[:, :, DT_PAD:DT_PAD + D_INNER].reshape(rows, D_INNER)
        acp_ref[...] = cumsum_tokens(dt_p * apad_ref[...], tok_p)
        acum_e = cumsum_tokens(dt_e * ae_ref[...], tok)
        xdt = xs * dt_e
        w = xdt * jnp.exp(bcast_token(acum_e, SAMPLE_LEN - 1) - acum_e)
        wt_ref[...] = w.T.astype(BF16)
        ea_ref[...] = jnp.exp(acum_e)
        for s in range(SAMPLE_LEN):
            p_ref[s * rows:(s + 1) * rows, :] = (cm * bcast_token(bm, s)).astype(BF16)
        cb_sum = _dot(p_ref[...], gsum_ref[...])
        yd = de_ref[...] * xs
        for s in range(SAMPLE_LEN):
            cb_e = _dot(cb_sum[s * rows:(s + 1) * rows, :].astype(BF16), gexp_ref[...])
            diff = acum_e - bcast_token(acum_e, s)
            decay = jnp.exp(jnp.where(tok >= s, diff, -jnp.inf))
            yd = yd + cb_e * decay * bcast_token(xdt, s)
        yd_ref[...] = yd

    r0 = pl.multiple_of(j * SAMPLE_LEN, SAMPLE_LEN)
    alast = acp_ref[pl.ds(r0 + SAMPLE_LEN - 1, 1), :]
    eye = (lax.broadcasted_iota(jnp.int32, (N_HEADS, DT_PAD), 0)
           == lax.broadcasted_iota(jnp.int32, (N_HEADS, DT_PAD), 1))
    alast_col = jnp.sum(jnp.where(eye, jnp.broadcast_to(alast, (N_HEADS, DT_PAD)), 0.0),
                        axis=1, keepdims=True)
    cd_b = jnp.broadcast_to(jnp.exp(alast_col), (N_HEADS, D_STATE))
    rowid = lax.broadcasted_iota(jnp.int32, (rows, D_STATE), 0)
    mine = (rowid >= r0) & (rowid < r0 + SAMPLE_LEN)
    y_offs = []
    for g in range(N_GROUPS):
        gs = slice(g * GROUP_DIM, (g + 1) * GROUP_DIM)
        hg = h0_ref[0, gs, :]
        cg = xbc_ref[j, :, C_OFF + g * D_STATE:C_OFF + (g + 1) * D_STATE].astype(BF16)
        y_offs.append(_dot_nt(cg, hg.astype(BF16)))
        b_all = xbc_ref[:, :, B_OFF + g * D_STATE:B_OFF + (g + 1) * D_STATE].reshape(rows, D_STATE)
        b_mine = jnp.where(mine, b_all, 0.0).astype(BF16)
        hout_ref[0, gs, :] = hg * _head_rows(cd_b, g) + _dot(wt_ref[gs, :], b_mine)
    y = yd_ref[pl.ds(r0, SAMPLE_LEN), :] + jnp.concatenate(y_offs, axis=1) * ea_ref[
        pl.ds(r0, SAMPLE_LEN), :]
    z = z_ref[j]
    y_ref[j] = jnp.concatenate(
        [_gate_norm(y[:, g * GROUP_DIM:(g + 1) * GROUP_DIM], z[:, g * GROUP_DIM:(g + 1) * GROUP_DIM],
                    ng_ref[:, g * GROUP_DIM:(g + 1) * GROUP_DIM]) for g in range(N_GROUPS)], axis=1)


def _scan_sample(xbc, z, dt, h0, a_pad, a_e, d_e, ng, gsum, gexp):
    b = xbc.shape[0]
    dt_w = dt.shape[-1]
    blk = lambda w: pl.BlockSpec((SCAN_NB, SAMPLE_LEN, w), lambda i, j: (i, 0, 0))
    st = pl.BlockSpec((1, D_INNER, D_STATE), lambda i, j: (i * SCAN_NB + j, 0, 0))
    return pl.pallas_call(
        _scan_sample_kernel,
        grid=(b // SCAN_NB, SCAN_NB),
        in_specs=[blk(CONV_DIM), blk(D_INNER), blk(dt_w), st, _resident((1, DT_PAD)),
                  _resident((1, D_INNER)), _resident((1, D_INNER)), _resident((1, D_INNER)),
                  _resident(gsum.shape), _resident(gexp.shape)],
        out_specs=[blk(D_INNER), st],
        out_shape=[jax.ShapeDtypeStruct((b, SAMPLE_LEN, D_INNER), F32),
                   jax.ShapeDtypeStruct((b, D_INNER, D_STATE), F32)],
        scratch_shapes=[pltpu.VMEM((D_INNER, SCAN_ROWS), BF16),
                        pltpu.VMEM((SCAN_ROWS, D_INNER), F32),
                        pltpu.VMEM((SCAN_ROWS, D_INNER), F32),
                        pltpu.VMEM((SCAN_ROWS, DT_PAD), F32),
                        pltpu.VMEM((SAMPLE_LEN * SCAN_ROWS, N_GROUPS * D_STATE), BF16)],
        compiler_params=_params(2),
        name="ssd_scan_sample",
    )(xbc, z, dt, h0, a_pad, a_e, d_e, ng, gsum, gexp)


PROJ_TM = 512


def _proj_res_kernel(x_ref, y_ref, w_ref, o_ref):
    o_ref[...] = x_ref[...] + _dot(y_ref[...].astype(BF16), w_ref[...])


def _proj_res(x2d, y2d, w):
    m, k = y2d.shape
    tm = min(PROJ_TM, m)
    return pl.pallas_call(
        _proj_res_kernel,
        grid=(m // tm,),
        in_specs=[pl.BlockSpec((tm, D_MODEL), lambda i: (i, 0)),
                  pl.BlockSpec((tm, k), lambda i: (i, 0)), _resident((k, D_MODEL))],
        out_specs=pl.BlockSpec((tm, D_MODEL), lambda i: (i, 0)),
        out_shape=jax.ShapeDtypeStruct((m, D_MODEL), F32),
        compiler_params=_params(1),
        name="proj_res",
    )(x2d, y2d, w)


POOL_HALO = 2 * SUBLANES


def _pool_kernel(x_ref, g_ref, win_ref, wgrp_ref, scale_ref, wout_ref, buf0_ref,
                 o_ref, bufnew_ref, ext_ref, *, nb, tl, pos0):
    rows = nb * tl
    jt = pl.program_id(1)

    @pl.when(jt == 0)
    def _():
        ext_ref[:, 0:POOL_HALO, :] = buf0_ref[...]

    x = x_ref[...].reshape(rows, D_MODEL)
    h = _rmsnorm(x, g_ref[...]).astype(BF16)
    ext_ref[:, POOL_HALO:POOL_HALO + tl, :] = _dot(h, win_ref[...]).reshape(nb, tl, D_MODEL)
    pos = (pos0 + jt * tl + lax.broadcasted_iota(jnp.int32, (nb, tl, POOL_GROUP_DIM), 1)).astype(F32)
    mixed = []
    for k, w in enumerate(POOL_WINDOWS):
        sl = slice(k * POOL_GROUP_DIM, (k + 1) * POOL_GROUP_DIM)
        u = ext_ref[:, POOL_HALO:POOL_HALO + tl, sl]
        tot = u
        for i in range(1, w):
            tot = tot + ext_ref[:, POOL_HALO - i:POOL_HALO - i + tl, sl]
        mean = tot / jnp.minimum(jnp.float32(w), pos + 1.0)
        m = (mean - u).reshape(rows, POOL_GROUP_DIM).astype(BF16)
        mixed.append(_dot(m, wgrp_ref[k]))
    mixed = (jnp.concatenate(mixed, axis=1) * scale_ref[...]).astype(BF16)
    o_ref[...] = (x + _dot(mixed, wout_ref[...])).reshape(nb, tl, D_MODEL)
    bufnew_ref[...] = ext_ref[:, tl + 1:tl + POOL_HALO, :]
    ext_ref[:, 0:POOL_HALO, :] = ext_ref[:, tl:tl + POOL_HALO, :]


def _pool(x3d, g, w_in, w_grp, scale, w_out, buf0_pad, *, nb, tl, pos0):
    b, l, _ = x3d.shape
    blk = pl.BlockSpec((nb, tl, D_MODEL), lambda i, j: (i, j, 0))
    return pl.pallas_call(
        functools.partial(_pool_kernel, nb=nb, tl=tl, pos0=pos0),
        grid=(b // nb, l // tl),
        in_specs=[blk, _resident((1, D_MODEL)), _resident((D_MODEL, D_MODEL)),
                  _resident(w_grp.shape), _resident((1, D_MODEL)), _resident((D_MODEL, D_MODEL)),
                  pl.BlockSpec((nb, POOL_HALO, D_MODEL), lambda i, j: (i, 0, 0))],
        out_specs=[blk, pl.BlockSpec((nb, MAX_WIN - 1, D_MODEL), lambda i, j: (i, 0, 0))],
        out_shape=[jax.ShapeDtypeStruct((b, l, D_MODEL), F32),
                   jax.ShapeDtypeStruct((b, MAX_WIN - 1, D_MODEL), F32)],
        scratch_shapes=[pltpu.VMEM((nb, POOL_HALO + tl, D_MODEL), F32)],
        compiler_params=_params(2),
        name="pool_mixer",
    )(x3d, g.reshape(1, D_MODEL), w_in, w_grp, scale.reshape(1, D_MODEL), w_out, buf0_pad)


def _expand_heads(v):
    return jnp.repeat(v.astype(F32), HEAD_DIM).reshape(1, D_INNER)


def _pad_heads(v):
    return jnp.pad(v.astype(F32), (0, DT_PAD - N_HEADS)).reshape(1, DT_PAD)


def _group_sum_matrix():
    m = np.zeros((N_GROUPS * D_STATE, LANES), np.float32)
    for g in range(N_GROUPS):
        m[g * D_STATE:(g + 1) * D_STATE, g] = 1.0
    return jnp.asarray(m, BF16)


def _group_expand_matrix():
    m = np.zeros((LANES, D_INNER), np.float32)
    for g in range(N_GROUPS):
        m[g, g * GROUP_DIM:(g + 1) * GROUP_DIM] = 1.0
    return jnp.asarray(m, BF16)


def _trunk(x, ssm0, conv0, pool0, pos0, p, *, sample):
    b, l, _ = x.shape
    m = b * l
    bf = lambda a: a.astype(BF16)
    ffn = lambda x2d, i, k, fg=None: _ffn(x2d, p["ffn_norm"][i, k], bf(p["ffn_w_gate"][i, k]),
                                          bf(p["ffn_w_up"][i, k]), bf(p["ffn_w_down"][i, k]), fg)
    x2d = ffn(x.reshape(m, D_MODEL), 0, 0)

    w_in = p["ssd_w_in"][0]
    wz = bf(w_in[:, :D_INNER])
    wxbc = bf(w_in[:, D_INNER:D_INNER + CONV_DIM])
    w_dt = w_in[:, D_INNER + CONV_DIM:]
    dt_bias = p["ssd_dt_bias"][0]
    wdt = jnp.pad(w_dt, ((0, 0), (0, DT_PAD - N_HEADS)))
    dtb = jnp.pad(dt_bias, (0, DT_PAD - N_HEADS))
    if sample:
        wdt = jnp.concatenate([wdt, jnp.repeat(w_dt, HEAD_DIM, axis=1)], axis=1)
        dtb = jnp.concatenate([dtb, jnp.repeat(dt_bias, HEAD_DIM)])
    conv0_pad = jnp.pad(conv0, ((0, 0), (CONV_HALO - (CONV_W - 1), 0), (0, 0)))
    nb, tl = (32, SAMPLE_LEN) if sample else (1, 256)
    z, xbc, dt, conv_new = _ssd_in(x2d.reshape(b, l, D_MODEL), p["mix_norm"][0], wz, wxbc, bf(wdt),
                                   p["ssd_conv_w"][0], p["ssd_conv_b"][0], dtb, conv0_pad,
                                   nb=nb, tl=tl)
    a_neg = -jnp.exp(p["ssd_a_log"][0].astype(F32))
    a_pad, d_e = _pad_heads(a_neg), _expand_heads(p["ssd_d"][0])
    ng = p["ssd_norm"][0].reshape(1, D_INNER)
    if sample:
        y, ssm_new = _scan_sample(xbc, z, dt, ssm0.reshape(b, D_INNER, D_STATE), a_pad,
                                  _expand_heads(a_neg), d_e, ng,
                                  _group_sum_matrix(), _group_expand_matrix())
    else:
        y, ssm_new = _scan_prompt(xbc, z, dt, a_pad, d_e, ng)
    ssm_new = ssm_new.reshape(b, N_HEADS, HEAD_DIM, D_STATE)
    x2d = _proj_res(x2d, y.reshape(m, D_INNER), bf(p["ssd_w_out"][0]))
    x2d = ffn(x2d, 0, 1)

    x2d = ffn(x2d, 1, 0)
    pool0_pad = jnp.pad(pool0, ((0, 0), (POOL_HALO - (MAX_WIN - 1), 0), (0, 0)))
    nb, tl = (64, SAMPLE_LEN) if sample else (1, 512)
    x3d, pool_new = _pool(x2d.reshape(b, l, D_MODEL), p["mix_norm"][1], bf(p["pool_w_in"][0]),
                          bf(p["pool_w_group"][0]), p["pool_scale"][0], bf(p["pool_w_out"][0]),
                          pool0_pad, nb=nb, tl=tl, pos0=pos0)
    y2d = ffn(x3d.reshape(m, D_MODEL), 1, 1, p["final_norm"])
    return y2d.reshape(b, l, D_MODEL), ssm_new[None], conv_new[None], pool_new[None]


def kernel(x_prompt, x_sample, state_ssm, state_conv, state_pool, ffn_norm, ffn_w_gate, ffn_w_up,
           ffn_w_down, mix_norm, ssd_w_in, ssd_conv_w, ssd_conv_b, ssd_dt_bias, ssd_a_log, ssd_d,
           ssd_norm, ssd_w_out, pool_w_in, pool_w_group, pool_scale, pool_w_out, final_norm):
    p = dict(ffn_norm=ffn_norm, ffn_w_gate=ffn_w_gate, ffn_w_up=ffn_w_up, ffn_w_down=ffn_w_down,
             mix_norm=mix_norm, ssd_w_in=ssd_w_in, ssd_conv_w=ssd_conv_w, ssd_conv_b=ssd_conv_b,
             ssd_dt_bias=ssd_dt_bias, ssd_a_log=ssd_a_log, ssd_d=ssd_d, ssd_norm=ssd_norm,
             ssd_w_out=ssd_w_out, pool_w_in=pool_w_in, pool_w_group=pool_w_group,
             pool_scale=pool_scale, pool_w_out=pool_w_out, final_norm=final_norm)
    bp = x_prompt.shape[0]
    conv_zero = jnp.zeros((bp, CONV_W - 1, CONV_DIM), F32)
    pool_zero = jnp.zeros((bp, MAX_WIN - 1, D_MODEL), F32)
    y_p, ssm_p, conv_p, pool_p = _trunk(x_prompt, None, conv_zero, pool_zero, 0, p, sample=False)
    y_s, ssm_s, conv_s, pool_s = _trunk(x_sample, state_ssm[0], state_conv[0], state_pool[0],
                                        PAST_LEN, p, sample=True)
    return (y_p, y_s, ssm_p, conv_p, pool_p, ssm_s, conv_s, pool_s)
```

```python
import functools

import numpy as np
import jax
import jax.numpy as jnp
from jax import lax
from jax.experimental import pallas as pl
from jax.experimental.pallas import tpu as pltpu

F32 = jnp.float32
BF16 = jnp.bfloat16

EPS = 1e-6
D_MODEL = 1024
D_FF = 2816
D_INNER = 2048
HEAD_DIM = 64
N_HEADS = 32
N_GROUPS = 8
HEADS_PER_GROUP = 4
GROUP_DIM = HEADS_PER_GROUP * HEAD_DIM
D_STATE = 128
CONV_W = 4
CONV_DIM = D_INNER + 2 * N_GROUPS * D_STATE
CHUNK = 128
POOL_WINDOWS = (2, 4, 8, 16)
POOL_GROUP_DIM = 256
MAX_WIN = 16
PAST_LEN = 16384
LANES = 128
SUBLANES = 8
DT_PAD = LANES
VMEM_LIMIT = 56 * 1024 * 1024

NT_DIMS = (((1,), (1,)), ((), ()))
LOG2E = 1.4426950408889634


def _resident(shape):
    nd = len(shape)
    return pl.BlockSpec(shape, lambda *_: (0,) * nd, pipeline_mode=pl.Buffered(1))


def _params(n_axes):
    return pltpu.CompilerParams(dimension_semantics=("arbitrary",) * n_axes,
                                vmem_limit_bytes=VMEM_LIMIT)


def _rmsnorm(x, g):
    ms = jnp.mean(x * x, axis=-1, keepdims=True)
    return x * lax.rsqrt(ms + EPS) * g


def _silu(x):
    return x * jax.nn.sigmoid(x)


def _softplus(x):
    return jnp.maximum(x, 0.0) + jnp.log(1.0 + jnp.exp(-jnp.abs(x)))


def _dot(a, b):
    return jnp.dot(a, b, preferred_element_type=F32)


def _dot_nt(a, b):
    return lax.dot_general(a, b, NT_DIMS, preferred_element_type=F32)


def _dot_exact01(t, x):
    hi = x.astype(BF16)
    r1 = x - hi.astype(F32)
    mid = r1.astype(BF16)
    lo = (r1 - mid.astype(F32)).astype(BF16)
    return _dot(t, hi) + _dot(t, mid) + _dot(t, lo)


FFN_TM = 512
FFN_TF = 256


def _ffn_kernel(*refs, final, pre):
    refs = list(refs)
    x_ref = refs.pop(0)
    y_ref, wpre_ref = (refs.pop(0), refs.pop(0)) if pre else (None, None)
    g_ref, wg_ref, wu_ref, wd_ref = refs[:4]
    fg_ref = refs[4] if final else None
    o_ref, h_ref, a_ref = refs[-3:]
    x = x_ref[...]
    if pre:
        x = x + _dot(y_ref[...].astype(BF16), wpre_ref[...])
    h_ref[...] = _rmsnorm(x, g_ref[...]).astype(BF16)
    for f in range(0, D_FF, FFN_TF):
        h = h_ref[...]
        gate = _dot(h, wg_ref[:, f:f + FFN_TF])
        up = _dot(h, wu_ref[:, f:f + FFN_TF])
        a_ref[:, f:f + FFN_TF] = (_silu(gate) * up).astype(BF16)
    y = x + 0.5 * _dot(a_ref[...], wd_ref[...])
    if final:
        y = _rmsnorm(y, fg_ref[...])
    o_ref[...] = y


def _ffn(x2d, g, wg, wu, wd, final_g=None, pre=None):
    m = x2d.shape[0]
    tm = min(FFN_TM, m)
    final = final_g is not None
    row_spec = pl.BlockSpec((tm, D_MODEL), lambda i: (i, 0))
    in_specs, args = [row_spec], [x2d]
    if pre is not None:
        y2d, wpre = pre
        in_specs += [pl.BlockSpec((tm, y2d.shape[1]), lambda i: (i, 0)), _resident(wpre.shape)]
        args += [y2d, wpre]
    in_specs += [_resident((1, D_MODEL)), _resident((D_MODEL, D_FF)),
                 _resident((D_MODEL, D_FF)), _resident((D_FF, D_MODEL))]
    args += [g.reshape(1, D_MODEL), wg, wu, wd]
    if final:
        in_specs.append(_resident((1, D_MODEL)))
        args.append(final_g.reshape(1, D_MODEL))
    return pl.pallas_call(
        functools.partial(_ffn_kernel, final=final, pre=pre is not None),
        grid=(m // tm,),
        in_specs=in_specs,
        out_specs=row_spec,
        out_shape=jax.ShapeDtypeStruct((m, D_MODEL), F32),
        scratch_shapes=[pltpu.VMEM((tm, D_MODEL), BF16), pltpu.VMEM((tm, D_FF), BF16)],
        compiler_params=_params(1),
        name="ffn" + ("_pre" if pre is not None else "") + ("_final" if final else ""),
    )(*args)


CONV_HALO = SUBLANES
SSD_IN_LANE_CHUNK = 512


def _shift_rows(v, k):
    r = pltpu.roll(v, k, 2)
    prev = jnp.concatenate([r[:, :1], r[:, :-1]], axis=1)
    sub = lax.broadcasted_iota(jnp.int32, v.shape, 2)
    return jnp.where(sub < k, prev, r)


def _chunk_decays(dt, a_pad):
    q = dt.shape[0]
    tril = (lax.broadcasted_iota(jnp.int32, (q, q), 0) >= lax.broadcasted_iota(jnp.int32, (q, q), 1))
    acum = _dot_exact01(tril.astype(BF16), dt * a_pad) * LOG2E
    acum_t = acum.T
    dt_t = dt.T
    src_t = acum_t - jnp.log2(dt_t)
    w_t = jnp.exp2(acum_t[:, q - 1:q] - acum_t) * dt_t
    return acum, acum_t, src_t, w_t


def _ssd_in_kernel(*refs, nb, tl, chunked):
    refs = list(refs)
    x_ref, g_ref, wz_ref, wxbc_ref, wdt_ref, cw_ref, cb_ref, dtb_ref, conv0_ref = refs[:9]
    del refs[:9]
    apad_ref = refs.pop(0) if chunked else None
    z_ref, xbc_ref = refs[:2]
    del refs[:2]
    if chunked:
        acum_ref, tr_ref = refs[:2]
        del refs[:2]
    else:
        dt_ref = refs.pop(0)
    convnew_ref, halo_ref, h_ref = refs[:3]
    raw_refs = refs[3:]
    rows = nb * tl
    wc = SSD_IN_LANE_CHUNK
    nblk = tl // SUBLANES

    @pl.when(pl.program_id(1) == 0)
    def _():
        halo_ref[...] = conv0_ref[...]

    h_ref[...] = _rmsnorm(x_ref[...].reshape(rows, D_MODEL), g_ref[...]).astype(BF16)
    dt = _softplus(_dot(h_ref[...], wdt_ref[...]) + dtb_ref[...])
    if chunked:
        for j in range(tl // CHUNK):
            cs = slice(j * CHUNK, (j + 1) * CHUNK)
            acum, acum_t, src_t, w_t = _chunk_decays(dt[cs, :], apad_ref[...])
            acum_ref[0, cs, :] = acum
            tr_ref[0, j, 0] = acum_t
            tr_ref[0, j, 1] = src_t
            tr_ref[0, j, 2] = w_t
    else:
        dt_ref[...] = dt.reshape(nb, tl, dt.shape[-1])
    for i, c in enumerate(range(0, CONV_DIM, wc)):
        sl = slice(c, c + wc)
        zs = slice(c // 2, c // 2 + wc // 2)
        raw_ref = raw_refs[i % len(raw_refs)]
        raw_ref[:, 0:CONV_HALO, :] = halo_ref[:, :, sl]
        raw_ref[:, CONV_HALO:, :] = _dot(h_ref[...], wxbc_ref[:, sl]).reshape(nb, tl, wc)
        z_ref[:, :, zs] = _dot(h_ref[...], wz_ref[:, zs]).reshape(nb, tl, wc // 2)
        halo_ref[:, :, sl] = raw_ref[:, tl:, :]
        ext = raw_ref[...].reshape(nb, nblk + 1, SUBLANES, wc)
        tap = lambda k: cw_ref[k:k + 1, sl].reshape(1, 1, 1, wc)
        ext1 = _shift_rows(ext, 1)
        p = ext * tap(3) + ext1 * tap(2)
        q = ext * tap(1) + ext1 * tap(0)
        acc = (cb_ref[:, sl].reshape(1, 1, 1, wc) + p + _shift_rows(q, 2))[:, 1:]
        xbc_ref[:, :, sl] = _silu(acc).reshape(nb, tl, wc)
    convnew_ref[...] = halo_ref[:, CONV_HALO - (CONV_W - 1):, :]


def _ssd_in(x3d, g, wz, wxbc, wdt, cw, cb, dtb, conv0_pad, a_pad=None, *, nb, tl):
    b, l, _ = x3d.shape
    dt_w = wdt.shape[1]
    chunked = a_pad is not None
    assert not chunked or (nb == 1 and tl % CHUNK == 0 and dt_w == DT_PAD)
    grid = (b // nb, l // tl)
    blk = lambda w: pl.BlockSpec((nb, tl, w), lambda i, j: (i, j, 0))
    in_specs = [blk(D_MODEL), _resident((1, D_MODEL)), _resident((D_MODEL, D_INNER)),
                _resident((D_MODEL, CONV_DIM)), _resident((D_MODEL, dt_w)),
                _resident((CONV_W, CONV_DIM)), _resident((1, CONV_DIM)), _resident((1, dt_w)),
                pl.BlockSpec((nb, CONV_HALO, CONV_DIM), lambda i, j: (i, 0, 0))]
    args = [x3d, g.reshape(1, D_MODEL), wz, wxbc, wdt, cw, cb.reshape(1, CONV_DIM),
            dtb.reshape(1, dt_w), conv0_pad]
    out_specs = [blk(D_INNER), blk(CONV_DIM)]
    out_shape = [jax.ShapeDtypeStruct((b, l, D_INNER), F32),
                 jax.ShapeDtypeStruct((b, l, CONV_DIM), F32)]
    if chunked:
        in_specs.append(_resident((1, DT_PAD)))
        args.append(a_pad)
        ncs = tl // CHUNK
        out_specs += [blk(DT_PAD),
                      pl.BlockSpec((1, ncs, 3, DT_PAD, CHUNK), lambda i, j: (i, j, 0, 0, 0))]
        out_shape += [jax.ShapeDtypeStruct((b, l, DT_PAD), F32),
                      jax.ShapeDtypeStruct((b, l // CHUNK, 3, DT_PAD, CHUNK), F32)]
    else:
        out_specs.append(blk(dt_w))
        out_shape.append(jax.ShapeDtypeStruct((b, l, dt_w), F32))
    out_specs.append(pl.BlockSpec((nb, CONV_W - 1, CONV_DIM), lambda i, j: (i, 0, 0)))
    out_shape.append(jax.ShapeDtypeStruct((b, CONV_W - 1, CONV_DIM), F32))
    return pl.pallas_call(
        functools.partial(_ssd_in_kernel, nb=nb, tl=tl, chunked=chunked),
        grid=grid,
        in_specs=in_specs,
        out_specs=out_specs,
        out_shape=out_shape,
        scratch_shapes=[pltpu.VMEM((nb, CONV_HALO, CONV_DIM), F32),
                        pltpu.VMEM((nb * tl, D_MODEL), BF16)]
        + [pltpu.VMEM((nb, CONV_HALO + tl, SSD_IN_LANE_CHUNK), F32)] * 2,
        compiler_params=_params(2),
        name="ssd_in_chunked" if chunked else "ssd_in",
    )(*args)


B_OFF = D_INNER
C_OFF = D_INNER + N_GROUPS * D_STATE


def _gate_norm(y, z, ng):
    yg = y * _silu(z)
    ms = jnp.mean(yg * yg, axis=-1, keepdims=True)
    return yg * lax.rsqrt(ms + EPS) * ng


def _head_rows(mat, g):
    return jnp.concatenate(
        [jnp.broadcast_to(mat[g * HEADS_PER_GROUP + r:g * HEADS_PER_GROUP + r + 1, :],
                          (HEAD_DIM, mat.shape[1])) for r in range(HEADS_PER_GROUP)], axis=0)


def _scan_prompt_kernel(xbc_ref, z_ref, acum_ref, tr_ref, de_ref, ng_ref, y_ref, hout_ref, *h_refs,
                        chunks):
    q = CHUNK
    step = pl.program_id(1)

    @pl.when(step == 0)
    def _():
        for h_ref in h_refs:
            h_ref[...] = jnp.zeros_like(h_ref)

    tril = (lax.broadcasted_iota(jnp.int32, (q, q), 0) >= lax.broadcasted_iota(jnp.int32, (q, q), 1))
    lane_g = lax.broadcasted_iota(jnp.int32, (q, GROUP_DIM), 1)
    lane_half = lax.broadcasted_iota(jnp.int32, (q, LANES), 1) < HEAD_DIM

    for ci in range(chunks):
        ts = slice(ci * q, (ci + 1) * q)
        acum = acum_ref[0, ts, :]
        acum_t, src_t, w_t = tr_ref[0, ci, 0], tr_ref[0, ci, 1], tr_ref[0, ci, 2]
        cd_b = jnp.broadcast_to(jnp.exp2(acum_t[:, q - 1:q]), (DT_PAD, D_STATE))
        for g in range(N_GROUPS):
            gs = slice(g * GROUP_DIM, (g + 1) * GROUP_DIM)
            bg = xbc_ref[0, ts, B_OFF + g * D_STATE:B_OFF + (g + 1) * D_STATE].astype(BF16)
            cg = xbc_ref[0, ts, C_OFF + g * D_STATE:C_OFF + (g + 1) * D_STATE].astype(BF16)
            xg = xbc_ref[0, ts, gs]
            xgb = xg.astype(BF16)
            cb = _dot_nt(cg, bg)
            ms, cols = [], []
            for r in range(HEADS_PER_GROUP):
                hd = g * HEADS_PER_GROUP + r
                colf = jnp.broadcast_to(acum[:, hd:hd + 1], (q, q))
                rowf = jnp.broadcast_to(src_t[hd:hd + 1, :], (q, q))
                ms.append((cb * jnp.exp2(jnp.where(tril, colf - rowf, -jnp.inf))).astype(BF16))
                cols.append(colf)
            mcat = jnp.concatenate(ms, axis=1)
            zero = jnp.zeros_like(xgb)
            bd = jnp.concatenate(
                [jnp.where((lane_g >= r * HEAD_DIM) & (lane_g < (r + 1) * HEAD_DIM), xgb, zero)
                 for r in range(HEADS_PER_GROUP)], axis=0)
            y_diag = _dot(mcat, bd)
            hg = h_refs[g][...]
            y_off = _dot_nt(cg, hg.astype(BF16))
            acum_e = jnp.concatenate([jnp.where(lane_half, cols[0], cols[1]),
                                      jnp.where(lane_half, cols[2], cols[3])], axis=1)
            y = y_diag + y_off * jnp.exp2(acum_e) + de_ref[:, gs] * xg
            xg_t = xg.T
            wt = jnp.concatenate(
                [xg_t[r * HEAD_DIM:(r + 1) * HEAD_DIM, :] * w_t[g * HEADS_PER_GROUP + r:
                                                                g * HEADS_PER_GROUP + r + 1, :]
                 for r in range(HEADS_PER_GROUP)], axis=0).astype(BF16)
            h_refs[g][...] = hg * _head_rows(cd_b, g) + _dot(wt, bg)
            y_ref[0, ts, gs] = _gate_norm(y, z_ref[0, ts, gs], ng_ref[:, gs]).astype(y_ref.dtype)

    @pl.when(step == pl.num_programs(1) - 1)
    def _():
        for g, h_ref in enumerate(h_refs):
            hout_ref[0, g * GROUP_DIM:(g + 1) * GROUP_DIM, :] = h_ref[...]


SCAN_CHUNKS_PER_STEP = 1


def _scan_prompt(xbc, z, acum, tr, d_e, ng):
    b, l, _ = xbc.shape
    chunks = SCAN_CHUNKS_PER_STEP
    tl = chunks * CHUNK
    blk = lambda w: pl.BlockSpec((1, tl, w), lambda i, j: (i, j, 0))
    return pl.pallas_call(
        functools.partial(_scan_prompt_kernel, chunks=chunks),
        grid=(b, l // tl),
        in_specs=[blk(CONV_DIM), blk(D_INNER), blk(DT_PAD),
                  pl.BlockSpec((1, chunks, 3, DT_PAD, CHUNK), lambda i, j: (i, j, 0, 0, 0)),
                  _resident((1, D_INNER)), _resident((1, D_INNER))],
        out_specs=[blk(D_INNER), pl.BlockSpec((1, D_INNER, D_STATE), lambda i, j: (i, 0, 0))],
        out_shape=[jax.ShapeDtypeStruct((b, l, D_INNER), BF16),
                   jax.ShapeDtypeStruct((b, D_INNER, D_STATE), F32)],
        scratch_shapes=[pltpu.VMEM((GROUP_DIM, D_STATE), F32)] * N_GROUPS,
        compiler_params=_params(2),
        name="ssd_scan_prompt",
    )(xbc, z, acum, tr, d_e, ng)


SAMPLE_LEN = 8
SCAN_NB = 16
SCAN_ROWS = SCAN_NB * SAMPLE_LEN


def _scan_sample_kernel(xbc_ref, z_ref, dt_ref, h0_ref, apad_ref, ae_ref, de_ref, ng_ref,
                        gsum_ref, gexp_ref, y_ref, hout_ref,
                        wt_ref, ea_ref, yd_ref, acp_ref, p_ref):
    rows = SCAN_ROWS
    j = pl.program_id(1)

    @pl.when(j == 0)
    def _():
        shape3 = lambda w: (SCAN_NB, SAMPLE_LEN, w)
        tok = lax.broadcasted_iota(jnp.int32, (rows, D_INNER), 0) & (SAMPLE_LEN - 1)
        tok_p = lax.broadcasted_iota(jnp.int32, (rows, DT_PAD), 0) & (SAMPLE_LEN - 1)

        def cumsum_tokens(v, t):
            for sh in (1, 2, 4):
                v = v + jnp.where(t >= sh, pltpu.roll(v, sh, 0), 0.0)
            return v

        def bcast_token(v, s):
            w = v.shape[-1]
            v3 = v.reshape(shape3(w))
            return jnp.broadcast_to(v3[:, s:s + 1, :], shape3(w)).reshape(rows, w)

        xs = xbc_ref[:, :, 0:D_INNER].reshape(rows, D_INNER)
        bm = xbc_ref[:, :, B_OFF:C_OFF].reshape(rows, N_GROUPS * D_STATE)
        cm = xbc_ref[:, :, C_OFF:CONV_DIM].reshape(rows, N_GROUPS * D_STATE)
        dt_p = dt_ref[:, :, 0:DT_PAD].reshape(rows, DT_PAD)
        dt_e = dt_ref[:, :, DT_PAD:DT_PAD + D_INNER].reshape(rows, D_INNER)
        acp_ref[...] = cumsum_tokens(dt_p * apad_ref[...], tok_p)
        acum_e = cumsum_tokens(dt_e * ae_ref[...], tok)
        xdt = xs * dt_e
        w = xdt * jnp.exp(bcast_token(acum_e, SAMPLE_LEN - 1) - acum_e)
        wt_ref[...] = w.T.astype(BF16)
        ea_ref[...] = jnp.exp(acum_e)
        for s in range(SAMPLE_LEN):
            p_ref[s * rows:(s + 1) * rows, :] = (cm * bcast_token(bm, s)).astype(BF16)
        cb_sum = _dot(p_ref[...], gsum_ref[...])
        yd = de_ref[...] * xs
        for s in range(SAMPLE_LEN):
            cb_e = _dot(cb_sum[s * rows:(s + 1) * rows, :].astype(BF16), gexp_ref[...])
            diff = acum_e - bcast_token(acum_e, s)
            decay = jnp.exp(jnp.where(tok >= s, diff, -jnp.inf))
            yd = yd + cb_e * decay * bcast_token(xdt, s)
        yd_ref[...] = yd

    r0 = pl.multiple_of(j * SAMPLE_LEN, SAMPLE_LEN)
    alast = acp_ref[pl.ds(r0 + SAMPLE_LEN - 1, 1), :]
    eye = (lax.broadcasted_iota(jnp.int32, (N_HEADS, DT_PAD), 0)
           == lax.broadcasted_iota(jnp.int32, (N_HEADS, DT_PAD), 1))
    alast_col = jnp.sum(jnp.where(eye, jnp.broadcast_to(alast, (N_HEADS, DT_PAD)), 0.0),
                        axis=1, keepdims=True)
    cd_b = jnp.broadcast_to(jnp.exp(alast_col), (N_HEADS, D_STATE))
    rowid = lax.broadcasted_iota(jnp.int32, (rows, D_STATE), 0)
    mine = (rowid >= r0) & (rowid < r0 + SAMPLE_LEN)
    y_offs = []
    for g in range(N_GROUPS):
        gs = slice(g * GROUP_DIM, (g + 1) * GROUP_DIM)
        hg = h0_ref[0, gs, :]
        cg = xbc_ref[j, :, C_OFF + g * D_STATE:C_OFF + (g + 1) * D_STATE].astype(BF16)
        y_offs.append(_dot_nt(cg, hg.astype(BF16)))
        b_all = xbc_ref[:, :, B_OFF + g * D_STATE:B_OFF + (g + 1) * D_STATE].reshape(rows, D_STATE)
        b_mine = jnp.where(mine, b_all, 0.0).astype(BF16)
        hout_ref[0, gs, :] = hg * _head_rows(cd_b, g) + _dot(wt_ref[gs, :], b_mine)
    y = yd_ref[pl.ds(r0, SAMPLE_LEN), :] + jnp.concatenate(y_offs, axis=1) * ea_ref[
        pl.ds(r0, SAMPLE_LEN), :]
    z = z_ref[j]
    y_ref[j] = jnp.concatenate(
        [_gate_norm(y[:, g * GROUP_DIM:(g + 1) * GROUP_DIM], z[:, g * GROUP_DIM:(g + 1) * GROUP_DIM],
                    ng_ref[:, g * GROUP_DIM:(g + 1) * GROUP_DIM]) for g in range(N_GROUPS)], axis=1)


def _scan_sample(xbc, z, dt, h0, a_pad, a_e, d_e, ng, gsum, gexp):
    b = xbc.shape[0]
    dt_w = dt.shape[-1]
    blk = lambda w: pl.BlockSpec((SCAN_NB, SAMPLE_LEN, w), lambda i, j: (i, 0, 0))
    st = pl.BlockSpec((1, D_INNER, D_STATE), lambda i, j: (i * SCAN_NB + j, 0, 0))
    return pl.pallas_call(
        _scan_sample_kernel,
        grid=(b // SCAN_NB, SCAN_NB),
        in_specs=[blk(CONV_DIM), blk(D_INNER), blk(dt_w), st, _resident((1, DT_PAD)),
                  _resident((1, D_INNER)), _resident((1, D_INNER)), _resident((1, D_INNER)),
                  _resident(gsum.shape), _resident(gexp.shape)],
        out_specs=[blk(D_INNER), st],
        out_shape=[jax.ShapeDtypeStruct((b, SAMPLE_LEN, D_INNER), F32),
                   jax.ShapeDtypeStruct((b, D_INNER, D_STATE), F32)],
        scratch_shapes=[pltpu.VMEM((D_INNER, SCAN_ROWS), BF16),
                        pltpu.VMEM((SCAN_ROWS, D_INNER), F32),
                        pltpu.VMEM((SCAN_ROWS, D_INNER), F32),
                        pltpu.VMEM((SCAN_ROWS, DT_PAD), F32),
                        pltpu.VMEM((SAMPLE_LEN * SCAN_ROWS, N_GROUPS * D_STATE), BF16)],
        compiler_params=_params(2),
        name="ssd_scan_sample",
    )(xbc, z, dt, h0, a_pad, a_e, d_e, ng, gsum, gexp)


POOL_HALO = 2 * SUBLANES


def _pool_kernel(x_ref, g_ref, win_ref, wgrp_ref, scale_ref, wout_ref, buf0_ref,
                 o_ref, bufnew_ref, ext_ref, *, nb, tl, pos0):
    rows = nb * tl
    jt = pl.program_id(1)

    @pl.when(jt == 0)
    def _():
        ext_ref[:, 0:POOL_HALO, :] = buf0_ref[...]

    x = x_ref[...].reshape(rows, D_MODEL)
    h = _rmsnorm(x, g_ref[...]).astype(BF16)
    ext_ref[:, POOL_HALO:POOL_HALO + tl, :] = _dot(h, win_ref[...]).reshape(nb, tl, D_MODEL)
    pos = (pos0 + jt * tl + lax.broadcasted_iota(jnp.int32, (nb, tl, POOL_GROUP_DIM), 1)).astype(F32)
    mixed = []
    for k, w in enumerate(POOL_WINDOWS):
        sl = slice(k * POOL_GROUP_DIM, (k + 1) * POOL_GROUP_DIM)
        u = ext_ref[:, POOL_HALO:POOL_HALO + tl, sl]
        tot = u
        for i in range(1, w):
            tot = tot + ext_ref[:, POOL_HALO - i:POOL_HALO - i + tl, sl]
        mean = tot / jnp.minimum(jnp.float32(w), pos + 1.0)
        m = (mean - u).reshape(rows, POOL_GROUP_DIM).astype(BF16)
        mixed.append(_dot(m, wgrp_ref[k]))
    mixed = (jnp.concatenate(mixed, axis=1) * scale_ref[...]).astype(BF16)
    o_ref[...] = (x + _dot(mixed, wout_ref[...])).reshape(nb, tl, D_MODEL)
    bufnew_ref[...] = ext_ref[:, tl + 1:tl + POOL_HALO, :]
    ext_ref[:, 0:POOL_HALO, :] = ext_ref[:, tl:tl + POOL_HALO, :]


def _pool(x3d, g, w_in, w_grp, scale, w_out, buf0_pad, *, nb, tl, pos0):
    b, l, _ = x3d.shape
    blk = pl.BlockSpec((nb, tl, D_MODEL), lambda i, j: (i, j, 0))
    return pl.pallas_call(
        functools.partial(_pool_kernel, nb=nb, tl=tl, pos0=pos0),
        grid=(b // nb, l // tl),
        in_specs=[blk, _resident((1, D_MODEL)), _resident((D_MODEL, D_MODEL)),
                  _resident(w_grp.shape), _resident((1, D_MODEL)), _resident((D_MODEL, D_MODEL)),
                  pl.BlockSpec((nb, POOL_HALO, D_MODEL), lambda i, j: (i, 0, 0))],
        out_specs=[blk, pl.BlockSpec((nb, MAX_WIN - 1, D_MODEL), lambda i, j: (i, 0, 0))],
        out_shape=[jax.ShapeDtypeStruct((b, l, D_MODEL), F32),
                   jax.ShapeDtypeStruct((b, MAX_WIN - 1, D_MODEL), F32)],
        scratch_shapes=[pltpu.VMEM((nb, POOL_HALO + tl, D_MODEL), F32)],
        compiler_params=_params(2),
        name="pool_mixer",
    )(x3d, g.reshape(1, D_MODEL), w_in, w_grp, scale.reshape(1, D_MODEL), w_out, buf0_pad)


def _expand_heads(v):
    return jnp.repeat(v.astype(F32), HEAD_DIM).reshape(1, D_INNER)


def _pad_heads(v):
    return jnp.pad(v.astype(F32), (0, DT_PAD - N_HEADS)).reshape(1, DT_PAD)


def _group_sum_matrix():
    m = np.zeros((N_GROUPS * D_STATE, LANES), np.float32)
    for g in range(N_GROUPS):
        m[g * D_STATE:(g + 1) * D_STATE, g] = 1.0
    return jnp.asarray(m, BF16)


def _group_expand_matrix():
    m = np.zeros((LANES, D_INNER), np.float32)
    for g in range(N_GROUPS):
        m[g, g * GROUP_DIM:(g + 1) * GROUP_DIM] = 1.0
    return jnp.asarray(m, BF16)


def _trunk(x, ssm0, conv0, pool0, pos0, p, *, sample):
    b, l, _ = x.shape
    m = b * l
    bf = lambda a: a.astype(BF16)
    ffn = lambda x2d, i, k, fg=None, pre=None: _ffn(
        x2d, p["ffn_norm"][i, k], bf(p["ffn_w_gate"][i, k]), bf(p["ffn_w_up"][i, k]),
        bf(p["ffn_w_down"][i, k]), fg, pre)
    x2d = ffn(x.reshape(m, D_MODEL), 0, 0)

    w_in = p["ssd_w_in"][0]
    wz = bf(w_in[:, :D_INNER])
    wxbc = bf(w_in[:, D_INNER:D_INNER + CONV_DIM])
    w_dt = w_in[:, D_INNER + CONV_DIM:]
    dt_bias = p["ssd_dt_bias"][0]
    wdt = jnp.pad(w_dt, ((0, 0), (0, DT_PAD - N_HEADS)))
    dtb = jnp.pad(dt_bias, (0, DT_PAD - N_HEADS))
    if sample:
        wdt = jnp.concatenate([wdt, jnp.repeat(w_dt, HEAD_DIM, axis=1)], axis=1)
        dtb = jnp.concatenate([dtb, jnp.repeat(dt_bias, HEAD_DIM)])
    conv0_pad = jnp.pad(conv0, ((0, 0), (CONV_HALO - (CONV_W - 1), 0), (0, 0)))
    a_neg = -jnp.exp(p["ssd_a_log"][0].astype(F32))
    a_pad, d_e = _pad_heads(a_neg), _expand_heads(p["ssd_d"][0])
    ng = p["ssd_norm"][0].reshape(1, D_INNER)
    ssd_in = functools.partial(_ssd_in, x2d.reshape(b, l, D_MODEL), p["mix_norm"][0], wz, wxbc,
                               bf(wdt), p["ssd_conv_w"][0], p["ssd_conv_b"][0], dtb, conv0_pad)
    if sample:
        z, xbc, dt, conv_new = ssd_in(nb=32, tl=SAMPLE_LEN)
        y, ssm_new = _scan_sample(xbc, z, dt, ssm0.reshape(b, D_INNER, D_STATE), a_pad,
                                  _expand_heads(a_neg), d_e, ng,
                                  _group_sum_matrix(), _group_expand_matrix())
    else:
        z, xbc, acum, tr, conv_new = ssd_in(a_pad, nb=1, tl=256)
        y, ssm_new = _scan_prompt(xbc, z, acum, tr, d_e, ng)
    ssm_new = ssm_new.reshape(b, N_HEADS, HEAD_DIM, D_STATE)
    x2d = ffn(x2d, 0, 1, pre=(y.reshape(m, D_INNER), bf(p["ssd_w_out"][0])))

    x2d = ffn(x2d, 1, 0)
    pool0_pad = jnp.pad(pool0, ((0, 0), (POOL_HALO - (MAX_WIN - 1), 0), (0, 0)))
    nb, tl = (64, SAMPLE_LEN) if sample else (1, 512)
    x3d, pool_new = _pool(x2d.reshape(b, l, D_MODEL), p["mix_norm"][1], bf(p["pool_w_in"][0]),
                          bf(p["pool_w_group"][0]), p["pool_scale"][0], bf(p["pool_w_out"][0]),
                          pool0_pad, nb=nb, tl=tl, pos0=pos0)
    y2d = ffn(x3d.reshape(m, D_MODEL), 1, 1, p["final_norm"])
    return y2d.reshape(b, l, D_MODEL), ssm_new[None], conv_new[None], pool_new[None]


def kernel(x_prompt, x_sample, state_ssm, state_conv, state_pool, ffn_norm, ffn_w_gate, ffn_w_up,
           ffn_w_down, mix_norm, ssd_w_in, ssd_conv_w, ssd_conv_b, ssd_dt_bias, ssd_a_log, ssd_d,
           ssd_norm, ssd_w_out, pool_w_in, pool_w_group, pool_scale, pool_w_out, final_norm):
    p = dict(ffn_norm=ffn_norm, ffn_w_gate=ffn_w_gate, ffn_w_up=ffn_w_up, ffn_w_down=ffn_w_down,
             mix_norm=mix_norm, ssd_w_in=ssd_w_in, ssd_conv_w=ssd_conv_w, ssd_conv_b=ssd_conv_b,
             ssd_dt_bias=ssd_dt_bias, ssd_a_log=ssd_a_log, ssd_d=ssd_d, ssd_norm=ssd_norm,
             ssd_w_out=ssd_w_out, pool_w_in=pool_w_in, pool_w_group=pool_w_group,
             pool_scale=pool_scale, pool_w_out=pool_w_out, final_norm=final_norm)
    bp = x_prompt.shape[0]
    conv_zero = jnp.zeros((bp, CONV_W - 1, CONV_DIM), F32)
    pool_zero = jnp.zeros((bp, MAX_WIN - 1, D_MODEL), F32)
    y_p, ssm_p, conv_p, pool_p = _trunk(x_prompt, None, conv_zero, pool_zero, 0, p, sample=False)
    y_s, ssm_s, conv_s, pool_s = _trunk(x_sample, state_ssm[0], state_conv[0], state_pool[0],
                                        PAST_LEN, p, sample=True)
    return (y_p, y_s, ssm_p, conv_p, pool_p, ssm_s, conv_s, pool_s)
```

```python
import functools

import numpy as np
import jax
import jax.numpy as jnp
from jax import lax
from jax.experimental import pallas as pl
from jax.experimental.pallas import tpu as pltpu

F32 = jnp.float32
BF16 = jnp.bfloat16

EPS = 1e-6
D_MODEL = 1024
D_FF = 2816
D_INNER = 2048
HEAD_DIM = 64
N_HEADS = 32
N_GROUPS = 8
HEADS_PER_GROUP = 4
GROUP_DIM = HEADS_PER_GROUP * HEAD_DIM
D_STATE = 128
CONV_W = 4
CONV_DIM = D_INNER + 2 * N_GROUPS * D_STATE
CHUNK = 128
POOL_WINDOWS = (2, 4, 8, 16)
POOL_GROUP_DIM = 256
MAX_WIN = 16
PAST_LEN = 16384
LANES = 128
SUBLANES = 8
DT_PAD = LANES
VMEM_LIMIT = 56 * 1024 * 1024

NT_DIMS = (((1,), (1,)), ((), ()))
LOG2E = 1.4426950408889634


def _resident(shape):
    nd = len(shape)
    return pl.BlockSpec(shape, lambda *_: (0,) * nd, pipeline_mode=pl.Buffered(1))


def _params(n_axes):
    return pltpu.CompilerParams(dimension_semantics=("arbitrary",) * n_axes,
                                vmem_limit_bytes=VMEM_LIMIT)


def _rmsnorm(x, g):
    ms = jnp.mean(x * x, axis=-1, keepdims=True)
    return x * lax.rsqrt(ms + EPS) * g


def _silu(x):
    return x * jax.nn.sigmoid(x)


def _softplus(x):
    return jnp.maximum(x, 0.0) + jnp.log(1.0 + jnp.exp(-jnp.abs(x)))


def _dot(a, b):
    return jnp.dot(a, b, preferred_element_type=F32)


def _dot_nt(a, b):
    return lax.dot_general(a, b, NT_DIMS, preferred_element_type=F32)


def _dot_exact01(t, x):
    hi = x.astype(BF16)
    r1 = x - hi.astype(F32)
    mid = r1.astype(BF16)
    lo = (r1 - mid.astype(F32)).astype(BF16)
    return _dot(t, hi) + _dot(t, mid) + _dot(t, lo)


FFN_TM = 512
FFN_TF = 256


def _ffn_kernel(*refs, final, pre):
    refs = list(refs)
    x_ref = refs.pop(0)
    y_ref, wpre_ref = (refs.pop(0), refs.pop(0)) if pre else (None, None)
    g_ref, wg_ref, wu_ref, wd_ref = refs[:4]
    fg_ref = refs[4] if final else None
    o_ref, h_ref, a_ref = refs[-3:]
    x = x_ref[...]
    if pre:
        x = x + _dot(y_ref[...].astype(BF16), wpre_ref[...])
    h_ref[...] = _rmsnorm(x, g_ref[...]).astype(BF16)
    for f in range(0, D_FF, FFN_TF):
        h = h_ref[...]
        gate = _dot(h, wg_ref[:, f:f + FFN_TF])
        up = _dot(h, wu_ref[:, f:f + FFN_TF])
        a_ref[:, f:f + FFN_TF] = (_silu(gate) * up).astype(BF16)
    y = x + 0.5 * _dot(a_ref[...], wd_ref[...])
    if final:
        y = _rmsnorm(y, fg_ref[...])
    o_ref[...] = y


def _ffn(x2d, g, wg, wu, wd, final_g=None, pre=None):
    m = x2d.shape[0]
    tm = min(FFN_TM, m)
    final = final_g is not None
    row_spec = pl.BlockSpec((tm, D_MODEL), lambda i: (i, 0))
    in_specs, args = [row_spec], [x2d]
    if pre is not None:
        y2d, wpre = pre
        in_specs += [pl.BlockSpec((tm, y2d.shape[1]), lambda i: (i, 0)), _resident(wpre.shape)]
        args += [y2d, wpre]
    in_specs += [_resident((1, D_MODEL)), _resident((D_MODEL, D_FF)),
                 _resident((D_MODEL, D_FF)), _resident((D_FF, D_MODEL))]
    args += [g.reshape(1, D_MODEL), wg, wu, wd]
    if final:
        in_specs.append(_resident((1, D_MODEL)))
        args.append(final_g.reshape(1, D_MODEL))
    return pl.pallas_call(
        functools.partial(_ffn_kernel, final=final, pre=pre is not None),
        grid=(m // tm,),
        in_specs=in_specs,
        out_specs=row_spec,
        out_shape=jax.ShapeDtypeStruct((m, D_MODEL), F32),
        scratch_shapes=[pltpu.VMEM((tm, D_MODEL), BF16), pltpu.VMEM((tm, D_FF), BF16)],
        compiler_params=_params(1),
        name="ffn" + ("_pre" if pre is not None else "") + ("_final" if final else ""),
    )(*args)


CONV_HALO = SUBLANES
SSD_IN_LANE_CHUNK = 512


def _shift_rows(v, k):
    r = pltpu.roll(v, k, 2)
    prev = jnp.concatenate([r[:, :1], r[:, :-1]], axis=1)
    sub = lax.broadcasted_iota(jnp.int32, v.shape, 2)
    return jnp.where(sub < k, prev, r)


def _chunk_decays(dt, a_pad):
    q = dt.shape[0]
    tril = (lax.broadcasted_iota(jnp.int32, (q, q), 0) >= lax.broadcasted_iota(jnp.int32, (q, q), 1))
    acum = _dot_exact01(tril.astype(BF16), dt * a_pad) * LOG2E
    acum_t = acum.T
    dt_t = dt.T
    src_t = acum_t - jnp.log2(dt_t)
    w_t = jnp.exp2(acum_t[:, q - 1:q] - acum_t) * dt_t
    return acum, acum_t, src_t, w_t


def _conv_silu(ext, cw_ref, cb_ref, sl):
    w = ext.shape[-1]
    tap = lambda k: cw_ref[k:k + 1, sl].reshape(1, 1, 1, w)
    ext1 = _shift_rows(ext, 1)
    p = ext * tap(3) + ext1 * tap(2)
    q = ext * tap(1) + ext1 * tap(0)
    return _silu((cb_ref[:, sl].reshape(1, 1, 1, w) + p + _shift_rows(q, 2))[:, 1:])


def _ssd_in_kernel(x_ref, g_ref, win_ref, wdt_ref, cw_ref, cb_ref, dtb_ref, conv0_ref,
                   z_ref, xbc_ref, dt_ref, convnew_ref, halo_ref, h_ref, *, nb, tl):
    rows = nb * tl
    wc = SSD_IN_LANE_CHUNK
    nblk = tl // SUBLANES

    @pl.when(pl.program_id(1) == 0)
    def _():
        halo_ref[...] = jnp.zeros_like(halo_ref)
        halo_ref[:, CONV_HALO - (CONV_W - 1):, :] = conv0_ref[...]

    h_ref[...] = _rmsnorm(x_ref[...].reshape(rows, D_MODEL), g_ref[...]).astype(BF16)
    dt = _softplus(_dot(h_ref[...], wdt_ref[...]) + dtb_ref[...])
    dt_ref[...] = dt.reshape(nb, tl, dt.shape[-1])
    for c in range(0, CONV_DIM, wc):
        sl = slice(c, c + wc)
        zs = slice(c // 2, c // 2 + wc // 2)
        z_ref[:, :, zs] = _dot(h_ref[...], win_ref[:, zs]).reshape(nb, tl, wc // 2)
        cur = _dot(h_ref[...], win_ref[:, D_INNER + c:D_INNER + c + wc]).reshape(nb, tl, wc)
        ext = jnp.concatenate([halo_ref[:, :, sl], cur], axis=1).reshape(nb, nblk + 1, SUBLANES, wc)
        xbc_ref[:, :, sl] = _conv_silu(ext, cw_ref, cb_ref, sl).reshape(nb, tl, wc)
        halo_ref[:, :, sl] = cur[:, tl - CONV_HALO:, :]
    convnew_ref[...] = halo_ref[:, CONV_HALO - (CONV_W - 1):, :]


def _ssd_in(x3d, g, w_in, wdt, cw, cb, dtb, conv0, *, nb, tl):
    b, l, _ = x3d.shape
    dt_w = wdt.shape[1]
    blk = lambda w: pl.BlockSpec((nb, tl, w), lambda i, j: (i, j, 0))
    return pl.pallas_call(
        functools.partial(_ssd_in_kernel, nb=nb, tl=tl),
        grid=(b // nb, l // tl),
        in_specs=[blk(D_MODEL), _resident((1, D_MODEL)), _resident(w_in.shape),
                  _resident((D_MODEL, dt_w)),
                  _resident((CONV_W, CONV_DIM)), _resident((1, CONV_DIM)), _resident((1, dt_w)),
                  pl.BlockSpec((nb, CONV_W - 1, CONV_DIM), lambda i, j: (i, 0, 0))],
        out_specs=[blk(D_INNER), blk(CONV_DIM), blk(dt_w),
                   pl.BlockSpec((nb, CONV_W - 1, CONV_DIM), lambda i, j: (i, 0, 0))],
        out_shape=[jax.ShapeDtypeStruct((b, l, D_INNER), F32),
                   jax.ShapeDtypeStruct((b, l, CONV_DIM), F32),
                   jax.ShapeDtypeStruct((b, l, dt_w), F32),
                   jax.ShapeDtypeStruct((b, CONV_W - 1, CONV_DIM), F32)],
        scratch_shapes=[pltpu.VMEM((nb, CONV_HALO, CONV_DIM), F32),
                        pltpu.VMEM((nb * tl, D_MODEL), BF16)],
        compiler_params=_params(2),
        name="ssd_in",
    )(x3d, g.reshape(1, D_MODEL), w_in, wdt, cw, cb.reshape(1, CONV_DIM),
      dtb.reshape(1, dt_w), conv0)


SSD_PIPE_TL = 256


def _ssd_in_prompt_kernel(x_ref, g_ref, win_ref, wdt_ref, cw_ref, cb_ref, dtb_ref, conv0_ref,
                          apad_ref, z_ref, acum_ref, tr_ref, xbc_ref, convnew_ref, halo_ref, h_ref):
    tl = SSD_PIPE_TL
    wc = SSD_IN_LANE_CHUNK
    nblk = tl // SUBLANES

    @pl.when(pl.program_id(1) == 0)
    def _():
        halo_ref[...] = jnp.zeros_like(halo_ref)
        halo_ref[CONV_HALO - (CONV_W - 1):, :] = conv0_ref[0]

    h_ref[...] = _rmsnorm(x_ref[0], g_ref[...]).astype(BF16)
    dt = _softplus(_dot(h_ref[...], wdt_ref[...]) + dtb_ref[...])
    for j in range(tl // CHUNK):
        cs = slice(j * CHUNK, (j + 1) * CHUNK)
        acum, acum_t, src_t, w_t = _chunk_decays(dt[cs, :], apad_ref[...])
        acum_ref[0, cs, :] = acum
        tr_ref[0, j, 0] = acum_t
        tr_ref[0, j, 1] = src_t
        tr_ref[0, j, 2] = w_t
    for c in range(0, CONV_DIM, wc):
        sl = slice(c, c + wc)
        zs = slice(c // 2, c // 2 + wc // 2)
        z_ref[0, :, zs] = _dot(h_ref[...], win_ref[:, zs]).astype(z_ref.dtype)
        cur = _dot(h_ref[...], win_ref[:, D_INNER + c:D_INNER + c + wc])
        ext = jnp.concatenate([halo_ref[:, sl], cur], axis=0).reshape(1, nblk + 1, SUBLANES, wc)
        xbc_ref[0, :, sl] = _conv_silu(ext, cw_ref, cb_ref, sl).reshape(tl, wc)
        halo_ref[:, sl] = cur[tl - CONV_HALO:, :]
    convnew_ref[0] = halo_ref[CONV_HALO - (CONV_W - 1):, :]


def _ssd_in_prompt(x3d, g, w_in, wdt, cw, cb, dtb, conv0, a_pad):
    b, l, _ = x3d.shape
    tl = SSD_PIPE_TL
    ncs = tl // CHUNK
    blk = lambda w: pl.BlockSpec((1, tl, w), lambda i, j: (i, j, 0))
    return pl.pallas_call(
        _ssd_in_prompt_kernel,
        grid=(b, l // tl),
        in_specs=[blk(D_MODEL), _resident((1, D_MODEL)), _resident(w_in.shape),
                  _resident((D_MODEL, DT_PAD)),
                  _resident((CONV_W, CONV_DIM)), _resident((1, CONV_DIM)), _resident((1, DT_PAD)),
                  pl.BlockSpec((1, CONV_W - 1, CONV_DIM), lambda i, j: (i, 0, 0)),
                  _resident((1, DT_PAD))],
        out_specs=[blk(D_INNER), blk(DT_PAD),
                   pl.BlockSpec((1, ncs, 3, DT_PAD, CHUNK), lambda i, j: (i, j, 0, 0, 0)),
                   blk(CONV_DIM),
                   pl.BlockSpec((1, CONV_W - 1, CONV_DIM), lambda i, j: (i, 0, 0))],
        out_shape=[jax.ShapeDtypeStruct((b, l, D_INNER), BF16),
                   jax.ShapeDtypeStruct((b, l, DT_PAD), F32),
                   jax.ShapeDtypeStruct((b, l // CHUNK, 3, DT_PAD, CHUNK), F32),
                   jax.ShapeDtypeStruct((b, l, CONV_DIM), F32),
                   jax.ShapeDtypeStruct((b, CONV_W - 1, CONV_DIM), F32)],
        scratch_shapes=[pltpu.VMEM((CONV_HALO, CONV_DIM), F32), pltpu.VMEM((tl, D_MODEL), BF16)],
        compiler_params=_params(2),
        name="ssd_in_prompt",
    )(x3d, g.reshape(1, D_MODEL), w_in, wdt, cw, cb.reshape(1, CONV_DIM),
      dtb.reshape(1, DT_PAD), conv0, a_pad)


B_OFF = D_INNER
C_OFF = D_INNER + N_GROUPS * D_STATE


def _gate_norm(y, z, ng):
    yg = y * _silu(z)
    ms = jnp.mean(yg * yg, axis=-1, keepdims=True)
    return yg * lax.rsqrt(ms + EPS) * ng


def _head_rows(mat, g):
    return jnp.concatenate(
        [jnp.broadcast_to(mat[g * HEADS_PER_GROUP + r:g * HEADS_PER_GROUP + r + 1, :],
                          (HEAD_DIM, mat.shape[1])) for r in range(HEADS_PER_GROUP)], axis=0)


def _scan_prompt_kernel(xbc_ref, z_ref, acum_ref, tr_ref, de_ref, ng_ref, y_ref, hout_ref, *h_refs,
                        chunks):
    q = CHUNK
    step = pl.program_id(1)

    @pl.when(step == 0)
    def _():
        for h_ref in h_refs:
            h_ref[...] = jnp.zeros_like(h_ref)

    tril = (lax.broadcasted_iota(jnp.int32, (q, q), 0) >= lax.broadcasted_iota(jnp.int32, (q, q), 1))
    lane_g = lax.broadcasted_iota(jnp.int32, (q, GROUP_DIM), 1)
    lane_half = lax.broadcasted_iota(jnp.int32, (q, LANES), 1) < HEAD_DIM

    for ci in range(chunks):
        ts = slice(ci * q, (ci + 1) * q)
        acum = acum_ref[0, ts, :]
        acum_t, src_t, w_t = tr_ref[0, ci, 0], tr_ref[0, ci, 1], tr_ref[0, ci, 2]
        cd_b = jnp.broadcast_to(jnp.exp2(acum_t[:, q - 1:q]), (DT_PAD, D_STATE))
        for g in range(N_GROUPS):
            gs = slice(g * GROUP_DIM, (g + 1) * GROUP_DIM)
            bg = xbc_ref[0, ts, B_OFF + g * D_STATE:B_OFF + (g + 1) * D_STATE].astype(BF16)
            cg = xbc_ref[0, ts, C_OFF + g * D_STATE:C_OFF + (g + 1) * D_STATE].astype(BF16)
            xg = xbc_ref[0, ts, gs]
            xgb = xg.astype(BF16)
            cb = _dot_nt(cg, bg)
            ms, cols = [], []
            for r in range(HEADS_PER_GROUP):
                hd = g * HEADS_PER_GROUP + r
                colf = jnp.broadcast_to(acum[:, hd:hd + 1], (q, q))
                rowf = jnp.broadcast_to(src_t[hd:hd + 1, :], (q, q))
                ms.append((cb * jnp.exp2(jnp.where(tril, colf - rowf, -jnp.inf))).astype(BF16))
                cols.append(colf)
            mcat = jnp.concatenate(ms, axis=1)
            zero = jnp.zeros_like(xgb)
            bd = jnp.concatenate(
                [jnp.where((lane_g >= r * HEAD_DIM) & (lane_g < (r + 1) * HEAD_DIM), xgb, zero)
                 for r in range(HEADS_PER_GROUP)], axis=0)
            y_diag = _dot(mcat, bd)
            hg = h_refs[g][...]
            y_off = _dot_nt(cg, hg.astype(BF16))
            acum_e = jnp.concatenate([jnp.where(lane_half, cols[0], cols[1]),
                                      jnp.where(lane_half, cols[2], cols[3])], axis=1)
            y = y_diag + y_off * jnp.exp2(acum_e) + de_ref[:, gs] * xg
            xg_t = xg.T
            wt = jnp.concatenate(
                [xg_t[r * HEAD_DIM:(r + 1) * HEAD_DIM, :] * w_t[g * HEADS_PER_GROUP + r:
                                                                g * HEADS_PER_GROUP + r + 1, :]
                 for r in range(HEADS_PER_GROUP)], axis=0).astype(BF16)
            h_refs[g][...] = hg * _head_rows(cd_b, g) + _dot(wt, bg)
            zg = z_ref[0, ts, gs].astype(F32)
            y_ref[0, ts, gs] = _gate_norm(y, zg, ng_ref[:, gs]).astype(y_ref.dtype)

    @pl.when(step == pl.num_programs(1) - 1)
    def _():
        for g, h_ref in enumerate(h_refs):
            hout_ref[0, g * GROUP_DIM:(g + 1) * GROUP_DIM, :] = h_ref[...]


SCAN_CHUNKS_PER_STEP = 1


def _scan_prompt(xbc, z, acum, tr, d_e, ng):
    b, l, _ = xbc.shape
    chunks = SCAN_CHUNKS_PER_STEP
    tl = chunks * CHUNK
    blk = lambda w: pl.BlockSpec((1, tl, w), lambda i, j: (i, j, 0))
    return pl.pallas_call(
        functools.partial(_scan_prompt_kernel, chunks=chunks),
        grid=(b, l // tl),
        in_specs=[blk(CONV_DIM), blk(D_INNER), blk(DT_PAD),
                  pl.BlockSpec((1, chunks, 3, DT_PAD, CHUNK), lambda i, j: (i, j, 0, 0, 0)),
                  _resident((1, D_INNER)), _resident((1, D_INNER))],
        out_specs=[blk(D_INNER), pl.BlockSpec((1, D_INNER, D_STATE), lambda i, j: (i, 0, 0))],
        out_shape=[jax.ShapeDtypeStruct((b, l, D_INNER), BF16),
                   jax.ShapeDtypeStruct((b, D_INNER, D_STATE), F32)],
        scratch_shapes=[pltpu.VMEM((GROUP_DIM, D_STATE), F32)] * N_GROUPS,
        compiler_params=_params(2),
        name="ssd_scan_prompt",
    )(xbc, z, acum, tr, d_e, ng)


SAMPLE_LEN = 8
SCAN_NB = 16
SCAN_ROWS = SCAN_NB * SAMPLE_LEN


def _scan_sample_kernel(xbc_ref, z_ref, dt_ref, h0_ref, apad_ref, ae_ref, de_ref, ng_ref,
                        gsum_ref, gexp_ref, y_ref, hout_ref,
                        wt_ref, ea_ref, yd_ref, acp_ref, p_ref):
    rows = SCAN_ROWS
    j = pl.program_id(1)

    @pl.when(j == 0)
    def _():
        shape3 = lambda w: (SCAN_NB, SAMPLE_LEN, w)
        tok = lax.broadcasted_iota(jnp.int32, (rows, D_INNER), 0) & (SAMPLE_LEN - 1)
        tok_p = lax.broadcasted_iota(jnp.int32, (rows, DT_PAD), 0) & (SAMPLE_LEN - 1)

        def cumsum_tokens(v, t):
            for sh in (1, 2, 4):
                v = v + jnp.where(t >= sh, pltpu.roll(v, sh, 0), 0.0)
            return v

        def bcast_token(v, s):
            w = v.shape[-1]
            v3 = v.reshape(shape3(w))
            return jnp.broadcast_to(v3[:, s:s + 1, :], shape3(w)).reshape(rows, w)

        xs = xbc_ref[:, :, 0:D_INNER].reshape(rows, D_INNER)
        bm = xbc_ref[:, :, B_OFF:C_OFF].reshape(rows, N_GROUPS * D_STATE)
        cm = xbc_ref[:, :, C_OFF:CONV_DIM].reshape(rows, N_GROUPS * D_STATE)
        dt_p = dt_ref[:, :, 0:DT_PAD].reshape(rows, DT_PAD)
        dt_e = dt_ref[:, :, DT_PAD:DT_PAD + D_INNER].reshape(rows, D_INNER)
        acp_ref[...] = cumsum_tokens(dt_p * apad_ref[...], tok_p)
        acum_e = cumsum_tokens(dt_e * ae_ref[...], tok)
        xdt = xs * dt_e
        w = xdt * jnp.exp(bcast_token(acum_e, SAMPLE_LEN - 1) - acum_e)
        wt_ref[...] = w.T.astype(BF16)
        ea_ref[...] = jnp.exp(acum_e)
        for s in range(SAMPLE_LEN):
            p_ref[s * rows:(s + 1) * rows, :] = (cm * bcast_token(bm, s)).astype(BF16)
        cb_sum = _dot(p_ref[...], gsum_ref[...])
        yd = de_ref[...] * xs
        for s in range(SAMPLE_LEN):
            cb_e = _dot(cb_sum[s * rows:(s + 1) * rows, :].astype(BF16), gexp_ref[...])
            diff = acum_e - bcast_token(acum_e, s)
            decay = jnp.exp(jnp.where(tok >= s, diff, -jnp.inf))
            yd = yd + cb_e * decay * bcast_token(xdt, s)
        yd_ref[...] = yd

    r0 = pl.multiple_of(j * SAMPLE_LEN, SAMPLE_LEN)
    alast = acp_ref[pl.ds(r0 + SAMPLE_LEN - 1, 1), :]
    eye = (lax.broadcasted_iota(jnp.int32, (N_HEADS, DT_PAD), 0)
           == lax.broadcasted_iota(jnp.int32, (N_HEADS, DT_PAD), 1))
    alast_col = jnp.sum(jnp.where(eye, jnp.broadcast_to(alast, (N_HEADS, DT_PAD)), 0.0),
                        axis=1, keepdims=True)
    cd_b = jnp.broadcast_to(jnp.exp(alast_col), (N_HEADS, D_STATE))
    rowid = lax.broadcasted_iota(jnp.int32, (rows, D_STATE), 0)
    mine = (rowid >= r0) & (rowid < r0 + SAMPLE_LEN)
    y_offs = []
    for g in range(N_GROUPS):
        gs = slice(g * GROUP_DIM, (g + 1) * GROUP_DIM)
        hg = h0_ref[0, gs, :]
        cg = xbc_ref[j, :, C_OFF + g * D_STATE:C_OFF + (g + 1) * D_STATE].astype(BF16)
        y_offs.append(_dot_nt(cg, hg.astype(BF16)))
        b_all = xbc_ref[:, :, B_OFF + g * D_STATE:B_OFF + (g + 1) * D_STATE].reshape(rows, D_STATE)
        b_mine = jnp.where(mine, b_all, 0.0).astype(BF16)
        hout_ref[0, gs, :] = hg * _head_rows(cd_b, g) + _dot(wt_ref[gs, :], b_mine)
    y = yd_ref[pl.ds(r0, SAMPLE_LEN), :] + jnp.concatenate(y_offs, axis=1) * ea_ref[
        pl.ds(r0, SAMPLE_LEN), :]
    z = z_ref[j]
    y_ref[j] = jnp.concatenate(
        [_gate_norm(y[:, g * GROUP_DIM:(g + 1) * GROUP_DIM], z[:, g * GROUP_DIM:(g + 1) * GROUP_DIM],
                    ng_ref[:, g * GROUP_DIM:(g + 1) * GROUP_DIM]) for g in range(N_GROUPS)], axis=1)


def _scan_sample(xbc, z, dt, h0, a_pad, a_e, d_e, ng, gsum, gexp):
    b = xbc.shape[0]
    dt_w = dt.shape[-1]
    blk = lambda w: pl.BlockSpec((SCAN_NB, SAMPLE_LEN, w), lambda i, j: (i, 0, 0))
    st = pl.BlockSpec((1, D_INNER, D_STATE), lambda i, j: (i * SCAN_NB + j, 0, 0))
    return pl.pallas_call(
        _scan_sample_kernel,
        grid=(b // SCAN_NB, SCAN_NB),
        in_specs=[blk(CONV_DIM), blk(D_INNER), blk(dt_w), st, _resident((1, DT_PAD)),
                  _resident((1, D_INNER)), _resident((1, D_INNER)), _resident((1, D_INNER)),
                  _resident(gsum.shape), _resident(gexp.shape)],
        out_specs=[blk(D_INNER), st],
        out_shape=[jax.ShapeDtypeStruct((b, SAMPLE_LEN, D_INNER), F32),
                   jax.ShapeDtypeStruct((b, D_INNER, D_STATE), F32)],
        scratch_shapes=[pltpu.VMEM((D_INNER, SCAN_ROWS), BF16),
                        pltpu.VMEM((SCAN_ROWS, D_INNER), F32),
                        pltpu.VMEM((SCAN_ROWS, D_INNER), F32),
                        pltpu.VMEM((SCAN_ROWS, DT_PAD), F32),
                        pltpu.VMEM((SAMPLE_LEN * SCAN_ROWS, N_GROUPS * D_STATE), BF16)],
        compiler_params=_params(2),
        name="ssd_scan_sample",
    )(xbc, z, dt, h0, a_pad, a_e, d_e, ng, gsum, gexp)


POOL_HALO = 2 * SUBLANES


def _pool_kernel(x_ref, g_ref, win_ref, wgrp_ref, scale_ref, wout_ref, buf0_ref,
                 o_ref, bufnew_ref, ext_ref, *, nb, tl, pos0):
    rows = nb * tl
    jt = pl.program_id(1)

    @pl.when(jt == 0)
    def _():
        ext_ref[:, 0:POOL_HALO, :] = jnp.zeros((nb, POOL_HALO, D_MODEL), F32)
        ext_ref[:, POOL_HALO - (MAX_WIN - 1):POOL_HALO, :] = buf0_ref[...]

    x = x_ref[...].reshape(rows, D_MODEL)
    h = _rmsnorm(x, g_ref[...]).astype(BF16)
    ext_ref[:, POOL_HALO:POOL_HALO + tl, :] = _dot(h, win_ref[...]).reshape(nb, tl, D_MODEL)
    pos = (pos0 + jt * tl + lax.broadcasted_iota(jnp.int32, (nb, tl, POOL_GROUP_DIM), 1)).astype(F32)
    mixed = []
    for k, w in enumerate(POOL_WINDOWS):
        sl = slice(k * POOL_GROUP_DIM, (k + 1) * POOL_GROUP_DIM)
        u = ext_ref[:, POOL_HALO:POOL_HALO + tl, sl]
        tot = u
        for i in range(1, w):
            tot = tot + ext_ref[:, POOL_HALO - i:POOL_HALO - i + tl, sl]
        mean = tot / jnp.minimum(jnp.float32(w), pos + 1.0)
        m = (mean - u).reshape(rows, POOL_GROUP_DIM).astype(BF16)
        mixed.append(_dot(m, wgrp_ref[k]))
    mixed = (jnp.concatenate(mixed, axis=1) * scale_ref[...]).astype(BF16)
    o_ref[...] = (x + _dot(mixed, wout_ref[...])).reshape(nb, tl, D_MODEL)
    bufnew_ref[...] = ext_ref[:, tl + 1:tl + POOL_HALO, :]
    ext_ref[:, 0:POOL_HALO, :] = ext_ref[:, tl:tl + POOL_HALO, :]


def _pool(x3d, g, w_in, w_grp, scale, w_out, buf0, *, nb, tl, pos0):
    b, l, _ = x3d.shape
    blk = pl.BlockSpec((nb, tl, D_MODEL), lambda i, j: (i, j, 0))
    return pl.pallas_call(
        functools.partial(_pool_kernel, nb=nb, tl=tl, pos0=pos0),
        grid=(b // nb, l // tl),
        in_specs=[blk, _resident((1, D_MODEL)), _resident((D_MODEL, D_MODEL)),
                  _resident(w_grp.shape), _resident((1, D_MODEL)), _resident((D_MODEL, D_MODEL)),
                  pl.BlockSpec((nb, MAX_WIN - 1, D_MODEL), lambda i, j: (i, 0, 0))],
        out_specs=[blk, pl.BlockSpec((nb, MAX_WIN - 1, D_MODEL), lambda i, j: (i, 0, 0))],
        out_shape=[jax.ShapeDtypeStruct((b, l, D_MODEL), F32),
                   jax.ShapeDtypeStruct((b, MAX_WIN - 1, D_MODEL), F32)],
        scratch_shapes=[pltpu.VMEM((nb, POOL_HALO + tl, D_MODEL), F32)],
        compiler_params=_params(2),
        name="pool_mixer",
    )(x3d, g.reshape(1, D_MODEL), w_in, w_grp, scale.reshape(1, D_MODEL), w_out, buf0)


def _expand_heads(v):
    return jnp.repeat(v.astype(F32), HEAD_DIM).reshape(1, D_INNER)


def _pad_heads(v):
    return jnp.pad(v.astype(F32), (0, DT_PAD - N_HEADS)).reshape(1, DT_PAD)


def _group_sum_matrix():
    m = np.zeros((N_GROUPS * D_STATE, LANES), np.float32)
    for g in range(N_GROUPS):
        m[g * D_STATE:(g + 1) * D_STATE, g] = 1.0
    return jnp.asarray(m, BF16)


def _group_expand_matrix():
    m = np.zeros((LANES, D_INNER), np.float32)
    for g in range(N_GROUPS):
        m[g, g * GROUP_DIM:(g + 1) * GROUP_DIM] = 1.0
    return jnp.asarray(m, BF16)


def _trunk(x, ssm0, conv0, pool0, pos0, p, *, sample):
    b, l, _ = x.shape
    m = b * l
    bf = lambda a: a.astype(BF16)
    ffn = lambda x2d, i, k, fg=None, pre=None: _ffn(
        x2d, p["ffn_norm"][i, k], bf(p["ffn_w_gate"][i, k]), bf(p["ffn_w_up"][i, k]),
        bf(p["ffn_w_down"][i, k]), fg, pre)
    x2d = ffn(x.reshape(m, D_MODEL), 0, 0)

    w_in = p["ssd_w_in"][0]
    w_dt = w_in[:, D_INNER + CONV_DIM:]
    dt_bias = p["ssd_dt_bias"][0]
    wdt = jnp.pad(w_dt, ((0, 0), (0, DT_PAD - N_HEADS)))
    dtb = jnp.pad(dt_bias, (0, DT_PAD - N_HEADS))
    if sample:
        wdt = jnp.concatenate([wdt, jnp.repeat(w_dt, HEAD_DIM, axis=1)], axis=1)
        dtb = jnp.concatenate([dtb, jnp.repeat(dt_bias, HEAD_DIM)])
    a_neg = -jnp.exp(p["ssd_a_log"][0].astype(F32))
    a_pad, d_e = _pad_heads(a_neg), _expand_heads(p["ssd_d"][0])
    ng = p["ssd_norm"][0].reshape(1, D_INNER)
    ssd_in_args = (x2d.reshape(b, l, D_MODEL), p["mix_norm"][0], bf(w_in), bf(wdt),
                   p["ssd_conv_w"][0], p["ssd_conv_b"][0], dtb, conv0)
    if sample:
        z, xbc, dt, conv_new = _ssd_in(*ssd_in_args, nb=32, tl=SAMPLE_LEN)
        y, ssm_new = _scan_sample(xbc, z, dt, ssm0.reshape(b, D_INNER, D_STATE), a_pad,
                                  _expand_heads(a_neg), d_e, ng,
                                  _group_sum_matrix(), _group_expand_matrix())
    else:
        z, acum, tr, xbc, conv_new = _ssd_in_prompt(*ssd_in_args, a_pad)
        y, ssm_new = _scan_prompt(xbc, z, acum, tr, d_e, ng)
    ssm_new = ssm_new.reshape(b, N_HEADS, HEAD_DIM, D_STATE)
    x2d = ffn(x2d, 0, 1, pre=(y.reshape(m, D_INNER), bf(p["ssd_w_out"][0])))

    x2d = ffn(x2d, 1, 0)
    nb, tl = (64, SAMPLE_LEN) if sample else (1, 512)
    x3d, pool_new = _pool(x2d.reshape(b, l, D_MODEL), p["mix_norm"][1], bf(p["pool_w_in"][0]),
                          bf(p["pool_w_group"][0]), p["pool_scale"][0], bf(p["pool_w_out"][0]),
                          pool0, nb=nb, tl=tl, pos0=pos0)
    y2d = ffn(x3d.reshape(m, D_MODEL), 1, 1, p["final_norm"])
    return y2d.reshape(b, l, D_MODEL), ssm_new[None], conv_new[None], pool_new[None]


def kernel(x_prompt, x_sample, state_ssm, state_conv, state_pool, ffn_norm, ffn_w_gate, ffn_w_up,
           ffn_w_down, mix_norm, ssd_w_in, ssd_conv_w, ssd_conv_b, ssd_dt_bias, ssd_a_log, ssd_d,
           ssd_norm, ssd_w_out, pool_w_in, pool_w_group, pool_scale, pool_w_out, final_norm):
    p = dict(ffn_norm=ffn_norm, ffn_w_gate=ffn_w_gate, ffn_w_up=ffn_w_up, ffn_w_down=ffn_w_down,
             mix_norm=mix_norm, ssd_w_in=ssd_w_in, ssd_conv_w=ssd_conv_w, ssd_conv_b=ssd_conv_b,
             ssd_dt_bias=ssd_dt_bias, ssd_a_log=ssd_a_log, ssd_d=ssd_d, ssd_norm=ssd_norm,
             ssd_w_out=ssd_w_out, pool_w_in=pool_w_in, pool_w_group=pool_w_group,
             pool_scale=pool_scale, pool_w_out=pool_w_out, final_norm=final_norm)
    bp = x_prompt.shape[0]
    conv_zero = jnp.zeros((bp, CONV_W - 1, CONV_DIM), F32)
    pool_zero = jnp.zeros((bp, MAX_WIN - 1, D_MODEL), F32)
    y_p, ssm_p, conv_p, pool_p = _trunk(x_prompt, None, conv_zero, pool_zero, 0, p, sample=False)
    y_s, ssm_s, conv_s, pool_s = _trunk(x_sample, state_ssm[0], state_conv[0], state_pool[0],
                                        PAST_LEN, p, sample=True)
    return (y_p, y_s, ssm_p, conv_p, pool_p, ssm_s, conv_s, pool_s)
```

```python
import functools

import numpy as np
import jax
import jax.numpy as jnp
from jax import lax
from jax.experimental import pallas as pl
from jax.experimental.pallas import tpu as pltpu

F32 = jnp.float32
BF16 = jnp.bfloat16

EPS = 1e-6
D_MODEL = 1024
D_FF = 2816
D_INNER = 2048
HEAD_DIM = 64
N_HEADS = 32
N_GROUPS = 8
HEADS_PER_GROUP = 4
GROUP_DIM = HEADS_PER_GROUP * HEAD_DIM
D_STATE = 128
CONV_W = 4
CONV_DIM = D_INNER + 2 * N_GROUPS * D_STATE
CHUNK = 128
POOL_WINDOWS = (2, 4, 8, 16)
POOL_GROUP_DIM = 256
MAX_WIN = 16
PAST_LEN = 16384
LANES = 128
SUBLANES = 8
DT_PAD = LANES
VMEM_LIMIT = 56 * 1024 * 1024

NT_DIMS = (((1,), (1,)), ((), ()))
LOG2E = 1.4426950408889634


def _resident(shape, lead=()):
    nd = len(shape)
    return pl.BlockSpec((None,) * len(lead) + tuple(shape), lambda *_: tuple(lead) + (0,) * nd,
                        pipeline_mode=pl.Buffered(1))


def _params(n_axes):
    return pltpu.CompilerParams(dimension_semantics=("arbitrary",) * n_axes,
                                vmem_limit_bytes=VMEM_LIMIT)


def _rmsnorm(x, g):
    ms = jnp.mean(x * x, axis=-1, keepdims=True)
    return x * lax.rsqrt(ms + EPS) * g


def _silu(x):
    return x * jax.nn.sigmoid(x)


def _softplus(x):
    return jnp.maximum(x, 0.0) + jnp.log(1.0 + jnp.exp(-jnp.abs(x)))


def _dot(a, b):
    return jnp.dot(a, b, preferred_element_type=F32)


def _dot_nt(a, b):
    return lax.dot_general(a, b, NT_DIMS, preferred_element_type=F32)


def _dot_exact01(t, x):
    hi = x.astype(BF16)
    r1 = x - hi.astype(F32)
    mid = r1.astype(BF16)
    lo = (r1 - mid.astype(F32)).astype(BF16)
    return _dot(t, hi) + _dot(t, mid) + _dot(t, lo)


FFN_TM = 512
FFN_TF = 256


def _ffn_kernel(*refs, n_x, n_y, final, n_out, tiles_a):
    refs = list(refs)
    x_refs = [refs.pop(0) for _ in range(n_x)]
    y_refs = [refs.pop(0) for _ in range(n_y)]
    wpre_ref = refs.pop(0) if n_y else None
    g_ref, wg_ref, wu_ref, wd_ref = refs[:4]
    del refs[:4]
    fg_ref = refs.pop(0) if final else None
    o_refs = [refs.pop(0) for _ in range(n_out)]
    h_ref, a_ref = refs
    second = pl.program_id(0) >= tiles_a

    def pick(rs, dtype):
        if len(rs) == 1:
            return rs[0][...].astype(dtype)
        return jnp.where(second, rs[1][...].astype(dtype), rs[0][...].astype(dtype))

    x = pick(x_refs, F32)
    if n_y:
        x = x + _dot(pick(y_refs, BF16), wpre_ref[...])
    h_ref[...] = _rmsnorm(x, g_ref[...]).astype(BF16)
    for f in range(0, D_FF, FFN_TF):
        h = h_ref[...]
        gate = _dot(h, wg_ref[:, f:f + FFN_TF])
        up = _dot(h, wu_ref[:, f:f + FFN_TF])
        a_ref[:, f:f + FFN_TF] = (_silu(gate) * up).astype(BF16)
    y = x + 0.5 * _dot(a_ref[...], wd_ref[...])
    if final:
        y = _rmsnorm(y, fg_ref[...])
    if n_out == 1:
        o_refs[0][...] = y
    else:
        @pl.when(jnp.logical_not(second))
        def _():
            o_refs[0][...] = y

        @pl.when(second)
        def _():
            o_refs[1][...] = y


def _ffn(xs, rows, g, w_gate, w_up, w_down, idx, final_g=None, pre=None, split_out=False):
    tm = FFN_TM
    tiles_a, tiles_b = rows[0] // tm, rows[1] // tm
    assert rows[0] % tm == 0 and rows[1] % tm == 0
    final = final_g is not None

    def row_specs(arrs):
        if len(arrs) == 1:
            return [pl.BlockSpec((tm, arrs[0].shape[1]), lambda i: (i, 0))]
        return [pl.BlockSpec((tm, arrs[0].shape[1]), lambda i: (jnp.minimum(i, tiles_a - 1), 0)),
                pl.BlockSpec((tm, arrs[1].shape[1]), lambda i: (jnp.maximum(i - tiles_a, 0), 0))]

    in_specs, args = row_specs(xs), list(xs)
    ys = []
    if pre is not None:
        ys, wpre = pre
        in_specs += row_specs(ys) + [_resident(wpre.shape)]
        args += list(ys) + [wpre]
    in_specs += [_resident((1, D_MODEL)), _resident((D_MODEL, D_FF), idx),
                 _resident((D_MODEL, D_FF), idx), _resident((D_FF, D_MODEL), idx)]
    args += [g.reshape(1, D_MODEL), w_gate, w_up, w_down]
    if final:
        in_specs.append(_resident((1, D_MODEL)))
        args.append(final_g.reshape(1, D_MODEL))
    if split_out:
        outs = [jax.ShapeDtypeStruct((r, D_MODEL), F32) for r in rows]
    else:
        outs = [jax.ShapeDtypeStruct((rows[0] + rows[1], D_MODEL), F32)]
    res = pl.pallas_call(
        functools.partial(_ffn_kernel, n_x=len(xs), n_y=len(ys), final=final, n_out=len(outs),
                          tiles_a=tiles_a),
        grid=(tiles_a + tiles_b,),
        in_specs=in_specs,
        out_specs=row_specs(outs),
        out_shape=outs,
        scratch_shapes=[pltpu.VMEM((tm, D_MODEL), BF16), pltpu.VMEM((tm, D_FF), BF16)],
        compiler_params=_params(1),
        name="ffn" + ("_pre" if pre is not None else "") + ("_final" if final else ""),
    )(*args)
    return res if split_out else res[0]


CONV_HALO = SUBLANES
SSD_IN_LANE_CHUNK = 512


def _shift_rows(v, k):
    if k == SUBLANES:
        return jnp.concatenate([v[:, :1], v[:, :-1]], axis=1)
    r = pltpu.roll(v, k, 2)
    prev = jnp.concatenate([r[:, :1], r[:, :-1]], axis=1)
    sub = lax.broadcasted_iota(jnp.int32, v.shape, 2)
    return jnp.where(sub < k, prev, r)


def _chunk_decays(dt, a_pad):
    q = dt.shape[0]
    tril = (lax.broadcasted_iota(jnp.int32, (q, q), 0) >= lax.broadcasted_iota(jnp.int32, (q, q), 1))
    acum = _dot_exact01(tril.astype(BF16), dt * a_pad) * LOG2E
    acum_t = acum.T
    dt_t = dt.T
    src_t = acum_t - jnp.log2(dt_t)
    w_t = jnp.exp2(acum_t[:, q - 1:q] - acum_t) * dt_t
    return acum, acum_t, src_t, w_t


def _conv_silu(ext, cw_ref, cb_ref, sl):
    w = ext.shape[-1]
    tap = lambda k: cw_ref[k:k + 1, sl].reshape(1, 1, 1, w)
    ext1 = _shift_rows(ext, 1)
    p = ext * tap(3) + ext1 * tap(2)
    q = ext * tap(1) + ext1 * tap(0)
    return _silu((cb_ref[:, sl].reshape(1, 1, 1, w) + p + _shift_rows(q, 2))[:, 1:])


def _ssd_in_kernel(x_ref, g_ref, win_ref, wdt_ref, cw_ref, cb_ref, dtb_ref, conv0_ref,
                   z_ref, xbc_ref, dt_ref, convnew_ref, halo_ref, h_ref, *, nb, tl):
    rows = nb * tl
    wc = SSD_IN_LANE_CHUNK
    nblk = tl // SUBLANES

    @pl.when(pl.program_id(1) == 0)
    def _():
        halo_ref[...] = jnp.zeros_like(halo_ref)
        halo_ref[:, CONV_HALO - (CONV_W - 1):, :] = conv0_ref[...]

    h_ref[...] = _rmsnorm(x_ref[...], g_ref[...]).astype(BF16)
    dt = _softplus(_dot(h_ref[...], wdt_ref[...]) + dtb_ref[...])
    dt_ref[...] = dt.reshape(nb, tl, dt.shape[-1])
    for c in range(0, CONV_DIM, wc):
        sl = slice(c, c + wc)
        zs = slice(c // 2, c // 2 + wc // 2)
        z_ref[:, :, zs] = _dot(h_ref[...], win_ref[:, zs]).reshape(nb, tl, wc // 2)
        cur = _dot(h_ref[...], win_ref[:, D_INNER + c:D_INNER + c + wc]).reshape(nb, tl, wc)
        ext = jnp.concatenate([halo_ref[:, :, sl], cur], axis=1).reshape(nb, nblk + 1, SUBLANES, wc)
        xbc_ref[:, :, sl] = _conv_silu(ext, cw_ref, cb_ref, sl).reshape(nb, tl, wc)
        halo_ref[:, :, sl] = cur[:, tl - CONV_HALO:, :]
    convnew_ref[...] = halo_ref[:, CONV_HALO - (CONV_W - 1):, :]


def _ssd_in(x2d, row0, b, l, g, w_in, wdt, cw, cb, dtb, conv0, *, nb, tl):
    assert l == tl and row0 % (nb * tl) == 0
    tile0 = row0 // (nb * tl)
    dt_w = wdt.shape[1]
    blk = lambda w: pl.BlockSpec((nb, tl, w), lambda i, j: (i, j, 0))
    return pl.pallas_call(
        functools.partial(_ssd_in_kernel, nb=nb, tl=tl),
        grid=(b // nb, l // tl),
        in_specs=[pl.BlockSpec((nb * tl, D_MODEL), lambda i, j: (tile0 + i, 0)),
                  _resident((1, D_MODEL)), _resident(w_in.shape), _resident((D_MODEL, dt_w)),
                  _resident((CONV_W, CONV_DIM)), _resident((1, CONV_DIM)), _resident((1, dt_w)),
                  pl.BlockSpec((nb, CONV_W - 1, CONV_DIM), lambda i, j: (i, 0, 0))],
        out_specs=[blk(D_INNER), blk(CONV_DIM), blk(dt_w),
                   pl.BlockSpec((nb, CONV_W - 1, CONV_DIM), lambda i, j: (i, 0, 0))],
        out_shape=[jax.ShapeDtypeStruct((b, l, D_INNER), F32),
                   jax.ShapeDtypeStruct((b, l, CONV_DIM), F32),
                   jax.ShapeDtypeStruct((b, l, dt_w), F32),
                   jax.ShapeDtypeStruct((b, CONV_W - 1, CONV_DIM), F32)],
        scratch_shapes=[pltpu.VMEM((nb, CONV_HALO, CONV_DIM), F32),
                        pltpu.VMEM((nb * tl, D_MODEL), BF16)],
        compiler_params=_params(2),
        name="ssd_in",
    )(x2d, g.reshape(1, D_MODEL), w_in, wdt, cw, cb.reshape(1, CONV_DIM),
      dtb.reshape(1, dt_w), conv0)


SSD_PIPE_TL = 256


def _ssd_in_prompt_kernel(x_ref, g_ref, win_ref, wdt_ref, cw_ref, cb_ref, dtb_ref, conv0_ref,
                          apad_ref, z_ref, acum_ref, tr_ref, xbc_ref, convnew_ref, halo_ref, h_ref):
    tl = SSD_PIPE_TL
    wc = SSD_IN_LANE_CHUNK
    nblk = tl // SUBLANES

    @pl.when(pl.program_id(1) == 0)
    def _():
        halo_ref[...] = jnp.zeros_like(halo_ref)
        halo_ref[CONV_HALO - (CONV_W - 1):, :] = conv0_ref[0]

    h_ref[...] = _rmsnorm(x_ref[...], g_ref[...]).astype(BF16)
    dt = _softplus(_dot(h_ref[...], wdt_ref[...]) + dtb_ref[...])
    for j in range(tl // CHUNK):
        cs = slice(j * CHUNK, (j + 1) * CHUNK)
        acum, acum_t, src_t, w_t = _chunk_decays(dt[cs, :], apad_ref[...])
        acum_ref[0, cs, :] = acum
        tr_ref[0, j, 0] = acum_t
        tr_ref[0, j, 1] = src_t
        tr_ref[0, j, 2] = w_t
    for c in range(0, CONV_DIM, wc):
        sl = slice(c, c + wc)
        zs = slice(c // 2, c // 2 + wc // 2)
        z_ref[0, :, zs] = _dot(h_ref[...], win_ref[:, zs]).astype(z_ref.dtype)
        cur = _dot(h_ref[...], win_ref[:, D_INNER + c:D_INNER + c + wc])
        ext = jnp.concatenate([halo_ref[:, sl], cur], axis=0).reshape(1, nblk + 1, SUBLANES, wc)
        xbc_ref[0, :, sl] = _conv_silu(ext, cw_ref, cb_ref, sl).reshape(tl, wc)
        halo_ref[:, sl] = cur[tl - CONV_HALO:, :]
    convnew_ref[0] = halo_ref[CONV_HALO - (CONV_W - 1):, :]


def _ssd_in_prompt(x2d, b, l, g, w_in, wdt, cw, cb, dtb, conv0, a_pad):
    tl = SSD_PIPE_TL
    tps = l // tl
    ncs = tl // CHUNK
    blk = lambda w: pl.BlockSpec((1, tl, w), lambda i, j: (i, j, 0))
    return pl.pallas_call(
        _ssd_in_prompt_kernel,
        grid=(b, l // tl),
        in_specs=[pl.BlockSpec((tl, D_MODEL), lambda i, j: (i * tps + j, 0)),
                  _resident((1, D_MODEL)), _resident(w_in.shape), _resident((D_MODEL, DT_PAD)),
                  _resident((CONV_W, CONV_DIM)), _resident((1, CONV_DIM)), _resident((1, DT_PAD)),
                  pl.BlockSpec((1, CONV_W - 1, CONV_DIM), lambda i, j: (i, 0, 0)),
                  _resident((1, DT_PAD))],
        out_specs=[blk(D_INNER), blk(DT_PAD),
                   pl.BlockSpec((1, ncs, 3, DT_PAD, CHUNK), lambda i, j: (i, j, 0, 0, 0)),
                   blk(CONV_DIM),
                   pl.BlockSpec((1, CONV_W - 1, CONV_DIM), lambda i, j: (i, 0, 0))],
        out_shape=[jax.ShapeDtypeStruct((b, l, D_INNER), BF16),
                   jax.ShapeDtypeStruct((b, l, DT_PAD), F32),
                   jax.ShapeDtypeStruct((b, l // CHUNK, 3, DT_PAD, CHUNK), F32),
                   jax.ShapeDtypeStruct((b, l, CONV_DIM), F32),
                   jax.ShapeDtypeStruct((b, CONV_W - 1, CONV_DIM), F32)],
        scratch_shapes=[pltpu.VMEM((CONV_HALO, CONV_DIM), F32), pltpu.VMEM((tl, D_MODEL), BF16)],
        compiler_params=_params(2),
        name="ssd_in_prompt",
    )(x2d, g.reshape(1, D_MODEL), w_in, wdt, cw, cb.reshape(1, CONV_DIM),
      dtb.reshape(1, DT_PAD), conv0, a_pad)


B_OFF = D_INNER
C_OFF = D_INNER + N_GROUPS * D_STATE


def _gate_norm(y, z, ng):
    yg = y * _silu(z)
    ms = jnp.mean(yg * yg, axis=-1, keepdims=True)
    return yg * lax.rsqrt(ms + EPS) * ng


def _head_rows(mat, g):
    return jnp.concatenate(
        [jnp.broadcast_to(mat[g * HEADS_PER_GROUP + r:g * HEADS_PER_GROUP + r + 1, :],
                          (HEAD_DIM, mat.shape[1])) for r in range(HEADS_PER_GROUP)], axis=0)


def _scan_prompt_kernel(xbc_ref, z_ref, acum_ref, tr_ref, de_ref, ng_ref, y_ref, hout_ref, *h_refs,
                        chunks):
    q = CHUNK
    step = pl.program_id(1)

    @pl.when(step == 0)
    def _():
        for h_ref in h_refs:
            h_ref[...] = jnp.zeros_like(h_ref)

    tril = (lax.broadcasted_iota(jnp.int32, (q, q), 0) >= lax.broadcasted_iota(jnp.int32, (q, q), 1))
    lane_g = lax.broadcasted_iota(jnp.int32, (q, GROUP_DIM), 1)
    lane_half = lax.broadcasted_iota(jnp.int32, (q, LANES), 1) < HEAD_DIM

    for ci in range(chunks):
        ts = slice(ci * q, (ci + 1) * q)
        acum = acum_ref[0, ts, :]
        acum_t, src_t, w_t = tr_ref[0, ci, 0], tr_ref[0, ci, 1], tr_ref[0, ci, 2]
        cd_b = jnp.broadcast_to(jnp.exp2(acum_t[:, q - 1:q]), (DT_PAD, D_STATE))
        for g in range(N_GROUPS):
            gs = slice(g * GROUP_DIM, (g + 1) * GROUP_DIM)
            bg = xbc_ref[0, ts, B_OFF + g * D_STATE:B_OFF + (g + 1) * D_STATE].astype(BF16)
            cg = xbc_ref[0, ts, C_OFF + g * D_STATE:C_OFF + (g + 1) * D_STATE].astype(BF16)
            xg = xbc_ref[0, ts, gs]
            xgb = xg.astype(BF16)
            cb = _dot_nt(cg, bg)
            ms, cols = [], []
            for r in range(HEADS_PER_GROUP):
                hd = g * HEADS_PER_GROUP + r
                colf = jnp.broadcast_to(acum[:, hd:hd + 1], (q, q))
                rowf = jnp.broadcast_to(src_t[hd:hd + 1, :], (q, q))
                ms.append((cb * jnp.exp2(jnp.where(tril, colf - rowf, -jnp.inf))).astype(BF16))
                cols.append(colf)
            mcat = jnp.concatenate(ms, axis=1)
            zero = jnp.zeros_like(xgb)
            bd = jnp.concatenate(
                [jnp.where((lane_g >= r * HEAD_DIM) & (lane_g < (r + 1) * HEAD_DIM), xgb, zero)
                 for r in range(HEADS_PER_GROUP)], axis=0)
            y_diag = _dot(mcat, bd)
            hg = h_refs[g][...]
            y_off = _dot_nt(cg, hg.astype(BF16))
            acum_e = jnp.concatenate([jnp.where(lane_half, cols[0], cols[1]),
                                      jnp.where(lane_half, cols[2], cols[3])], axis=1)
            y = y_diag + y_off * jnp.exp2(acum_e) + de_ref[:, gs] * xg
            xg_t = xg.T
            wt = jnp.concatenate(
                [xg_t[r * HEAD_DIM:(r + 1) * HEAD_DIM, :] * w_t[g * HEADS_PER_GROUP + r:
                                                                g * HEADS_PER_GROUP + r + 1, :]
                 for r in range(HEADS_PER_GROUP)], axis=0).astype(BF16)
            h_refs[g][...] = hg * _head_rows(cd_b, g) + _dot(wt, bg)
            zg = z_ref[0, ts, gs].astype(F32)
            y_ref[0, ts, gs] = _gate_norm(y, zg, ng_ref[:, gs]).astype(y_ref.dtype)

    @pl.when(step == pl.num_programs(1) - 1)
    def _():
        for g, h_ref in enumerate(h_refs):
            hout_ref[0, g * GROUP_DIM:(g + 1) * GROUP_DIM, :] = h_ref[...]


SCAN_CHUNKS_PER_STEP = 1


def _scan_prompt(xbc, z, acum, tr, d_e, ng):
    b, l, _ = xbc.shape
    chunks = SCAN_CHUNKS_PER_STEP
    tl = chunks * CHUNK
    blk = lambda w: pl.BlockSpec((1, tl, w), lambda i, j: (i, j, 0))
    return pl.pallas_call(
        functools.partial(_scan_prompt_kernel, chunks=chunks),
        grid=(b, l // tl),
        in_specs=[blk(CONV_DIM), blk(D_INNER), blk(DT_PAD),
                  pl.BlockSpec((1, chunks, 3, DT_PAD, CHUNK), lambda i, j: (i, j, 0, 0, 0)),
                  _resident((1, D_INNER)), _resident((1, D_INNER))],
        out_specs=[blk(D_INNER), pl.BlockSpec((1, D_INNER, D_STATE), lambda i, j: (i, 0, 0))],
        out_shape=[jax.ShapeDtypeStruct((b, l, D_INNER), BF16),
                   jax.ShapeDtypeStruct((b, D_INNER, D_STATE), F32)],
        scratch_shapes=[pltpu.VMEM((GROUP_DIM, D_STATE), F32)] * N_GROUPS,
        compiler_params=_params(2),
        name="ssd_scan_prompt",
    )(xbc, z, acum, tr, d_e, ng)


SAMPLE_LEN = 8
SCAN_NB = 16
SCAN_ROWS = SCAN_NB * SAMPLE_LEN


def _scan_sample_kernel(xbc_ref, z_ref, dt_ref, h0_ref, apad_ref, ae_ref, de_ref, ng_ref,
                        gsum_ref, gexp_ref, y_ref, hout_ref,
                        wt_ref, ea_ref, yd_ref, acp_ref, p_ref):
    rows = SCAN_ROWS
    j = pl.program_id(1)

    @pl.when(j == 0)
    def _():
        shape3 = lambda w: (SCAN_NB, SAMPLE_LEN, w)
        tok = lax.broadcasted_iota(jnp.int32, (rows, D_INNER), 0) & (SAMPLE_LEN - 1)
        tok_p = lax.broadcasted_iota(jnp.int32, (rows, DT_PAD), 0) & (SAMPLE_LEN - 1)

        def cumsum_tokens(v, t):
            for sh in (1, 2, 4):
                v = v + jnp.where(t >= sh, pltpu.roll(v, sh, 0), 0.0)
            return v

        def bcast_token(v, s):
            w = v.shape[-1]
            v3 = v.reshape(shape3(w))
            return jnp.broadcast_to(v3[:, s:s + 1, :], shape3(w)).reshape(rows, w)

        xs = xbc_ref[:, :, 0:D_INNER].reshape(rows, D_INNER)
        bm = xbc_ref[:, :, B_OFF:C_OFF].reshape(rows, N_GROUPS * D_STATE)
        cm = xbc_ref[:, :, C_OFF:CONV_DIM].reshape(rows, N_GROUPS * D_STATE)
        dt_p = dt_ref[:, :, 0:DT_PAD].reshape(rows, DT_PAD)
        dt_e = dt_ref[:, :, DT_PAD:DT_PAD + D_INNER].reshape(rows, D_INNER)
        acp_ref[...] = cumsum_tokens(dt_p * apad_ref[...], tok_p)
        acum_e = cumsum_tokens(dt_e * ae_ref[...], tok)
        xdt = xs * dt_e
        w = xdt * jnp.exp(bcast_token(acum_e, SAMPLE_LEN - 1) - acum_e)
        wt_ref[...] = w.T.astype(BF16)
        ea_ref[...] = jnp.exp(acum_e)
        for s in range(SAMPLE_LEN):
            p_ref[s * rows:(s + 1) * rows, :] = (cm * bcast_token(bm, s)).astype(BF16)
        cb_sum = _dot(p_ref[...], gsum_ref[...])
        yd = de_ref[...] * xs
        for s in range(SAMPLE_LEN):
            cb_e = _dot(cb_sum[s * rows:(s + 1) * rows, :].astype(BF16), gexp_ref[...])
            diff = acum_e - bcast_token(acum_e, s)
            decay = jnp.exp(jnp.where(tok >= s, diff, -jnp.inf))
            yd = yd + cb_e * decay * bcast_token(xdt, s)
        yd_ref[...] = yd

    r0 = pl.multiple_of(j * SAMPLE_LEN, SAMPLE_LEN)
    alast = acp_ref[pl.ds(r0 + SAMPLE_LEN - 1, 1), :]
    eye = (lax.broadcasted_iota(jnp.int32, (N_HEADS, DT_PAD), 0)
           == lax.broadcasted_iota(jnp.int32, (N_HEADS, DT_PAD), 1))
    alast_col = jnp.sum(jnp.where(eye, jnp.broadcast_to(alast, (N_HEADS, DT_PAD)), 0.0),
                        axis=1, keepdims=True)
    cd_b = jnp.broadcast_to(jnp.exp(alast_col), (N_HEADS, D_STATE))
    rowid = lax.broadcasted_iota(jnp.int32, (rows, D_STATE), 0)
    mine = (rowid >= r0) & (rowid < r0 + SAMPLE_LEN)
    y_offs = []
    for g in range(N_GROUPS):
        gs = slice(g * GROUP_DIM, (g + 1) * GROUP_DIM)
        hg = h0_ref[0, gs, :]
        cg = xbc_ref[j, :, C_OFF + g * D_STATE:C_OFF + (g + 1) * D_STATE].astype(BF16)
        y_offs.append(_dot_nt(cg, hg.astype(BF16)))
        b_all = xbc_ref[:, :, B_OFF + g * D_STATE:B_OFF + (g + 1) * D_STATE].reshape(rows, D_STATE)
        b_mine = jnp.where(mine, b_all, 0.0).astype(BF16)
        hout_ref[0, gs, :] = hg * _head_rows(cd_b, g) + _dot(wt_ref[gs, :], b_mine)
    y = yd_ref[pl.ds(r0, SAMPLE_LEN), :] + jnp.concatenate(y_offs, axis=1) * ea_ref[
        pl.ds(r0, SAMPLE_LEN), :]
    z = z_ref[j]
    y_ref[j] = jnp.concatenate(
        [_gate_norm(y[:, g * GROUP_DIM:(g + 1) * GROUP_DIM], z[:, g * GROUP_DIM:(g + 1) * GROUP_DIM],
                    ng_ref[:, g * GROUP_DIM:(g + 1) * GROUP_DIM]) for g in range(N_GROUPS)], axis=1)


def _scan_sample(xbc, z, dt, h0, a_pad, a_e, d_e, ng, gsum, gexp):
    b = xbc.shape[0]
    dt_w = dt.shape[-1]
    blk = lambda w: pl.BlockSpec((SCAN_NB, SAMPLE_LEN, w), lambda i, j: (i, 0, 0))
    st = pl.BlockSpec((1, D_INNER, D_STATE), lambda i, j: (i * SCAN_NB + j, 0, 0))
    return pl.pallas_call(
        _scan_sample_kernel,
        grid=(b // SCAN_NB, SCAN_NB),
        in_specs=[blk(CONV_DIM), blk(D_INNER), blk(dt_w), st, _resident((1, DT_PAD)),
                  _resident((1, D_INNER)), _resident((1, D_INNER)), _resident((1, D_INNER)),
                  _resident(gsum.shape), _resident(gexp.shape)],
        out_specs=[blk(D_INNER), st],
        out_shape=[jax.ShapeDtypeStruct((b, SAMPLE_LEN, D_INNER), F32),
                   jax.ShapeDtypeStruct((b, D_INNER, D_STATE), F32)],
        scratch_shapes=[pltpu.VMEM((D_INNER, SCAN_ROWS), BF16),
                        pltpu.VMEM((SCAN_ROWS, D_INNER), F32),
                        pltpu.VMEM((SCAN_ROWS, D_INNER), F32),
                        pltpu.VMEM((SCAN_ROWS, DT_PAD), F32),
                        pltpu.VMEM((SAMPLE_LEN * SCAN_ROWS, N_GROUPS * D_STATE), BF16)],
        compiler_params=_params(2),
        name="ssd_scan_sample",
    )(xbc, z, dt, h0, a_pad, a_e, d_e, ng, gsum, gexp)


POOL_HALO = 2 * SUBLANES


def _pool_kernel(x_ref, g_ref, win_ref, wgrp_ref, scale_ref, wout_ref, buf0_ref,
                 o_ref, bufnew_ref, ext_ref, *, nb, tl, pos0):
    rows = nb * tl
    jt = pl.program_id(1)

    @pl.when(jt == 0)
    def _():
        ext_ref[:, 0:POOL_HALO, :] = jnp.zeros((nb, POOL_HALO, D_MODEL), F32)
        ext_ref[:, POOL_HALO - (MAX_WIN - 1):POOL_HALO, :] = buf0_ref[...]

    x = x_ref[...]
    h = _rmsnorm(x, g_ref[...]).astype(BF16)
    ext_ref[:, POOL_HALO:POOL_HALO + tl, :] = _dot(h, win_ref[...]).reshape(nb, tl, D_MODEL)
    nblk = tl // SUBLANES
    halo_blocks = POOL_HALO // SUBLANES
    shape4 = (nb, nblk, SUBLANES, POOL_GROUP_DIM)
    pos = (pos0 + jt * tl + SUBLANES * lax.broadcasted_iota(jnp.int32, shape4, 1)
           + lax.broadcasted_iota(jnp.int32, shape4, 2)).astype(F32)
    mixed = []
    for k, w in enumerate(POOL_WINDOWS):
        sl = slice(k * POOL_GROUP_DIM, (k + 1) * POOL_GROUP_DIM)
        ext = ext_ref[:, :, sl].reshape(nb, nblk + halo_blocks, SUBLANES, POOL_GROUP_DIM)
        tot, shift = ext, 1
        while shift < w:
            tot = tot + _shift_rows(tot, shift)
            shift *= 2
        u = ext[:, halo_blocks:]
        mean = tot[:, halo_blocks:] / jnp.minimum(jnp.float32(w), pos + 1.0)
        m = (mean - u).reshape(rows, POOL_GROUP_DIM).astype(BF16)
        mixed.append(_dot(m, wgrp_ref[k]))
    mixed = (jnp.concatenate(mixed, axis=1) * scale_ref[...]).astype(BF16)
    o_ref[...] = x + _dot(mixed, wout_ref[...])
    bufnew_ref[...] = ext_ref[:, tl + 1:tl + POOL_HALO, :]
    ext_ref[:, 0:POOL_HALO, :] = ext_ref[:, tl:tl + POOL_HALO, :]


def _pool(x2d, row0, b, l, g, w_in, w_grp, scale, w_out, buf0, *, nb, tl, pos0):
    assert row0 % (nb * tl) == 0 and (nb == 1 or l == tl)
    tile0, tps = row0 // (nb * tl), l // tl
    return pl.pallas_call(
        functools.partial(_pool_kernel, nb=nb, tl=tl, pos0=pos0),
        grid=(b // nb, tps),
        in_specs=[pl.BlockSpec((nb * tl, D_MODEL), lambda i, j: (tile0 + i * tps + j, 0)),
                  _resident((1, D_MODEL)), _resident((D_MODEL, D_MODEL)),
                  _resident(w_grp.shape), _resident((1, D_MODEL)), _resident((D_MODEL, D_MODEL)),
                  pl.BlockSpec((nb, MAX_WIN - 1, D_MODEL), lambda i, j: (i, 0, 0))],
        out_specs=[pl.BlockSpec((nb * tl, D_MODEL), lambda i, j: (i * tps + j, 0)),
                   pl.BlockSpec((nb, MAX_WIN - 1, D_MODEL), lambda i, j: (i, 0, 0))],
        out_shape=[jax.ShapeDtypeStruct((b * l, D_MODEL), F32),
                   jax.ShapeDtypeStruct((b, MAX_WIN - 1, D_MODEL), F32)],
        scratch_shapes=[pltpu.VMEM((nb, POOL_HALO + tl, D_MODEL), F32)],
        compiler_params=_params(2),
        name="pool_mixer",
    )(x2d, g.reshape(1, D_MODEL), w_in, w_grp, scale.reshape(1, D_MODEL), w_out, buf0)


def _expand_heads(v):
    return jnp.repeat(v.astype(F32), HEAD_DIM).reshape(1, D_INNER)


def _pad_heads(v):
    return jnp.pad(v.astype(F32), (0, DT_PAD - N_HEADS)).reshape(1, DT_PAD)


def _group_sum_matrix():
    m = np.zeros((N_GROUPS * D_STATE, LANES), np.float32)
    for g in range(N_GROUPS):
        m[g * D_STATE:(g + 1) * D_STATE, g] = 1.0
    return jnp.asarray(m, BF16)


def _group_expand_matrix():
    m = np.zeros((LANES, D_INNER), np.float32)
    for g in range(N_GROUPS):
        m[g, g * GROUP_DIM:(g + 1) * GROUP_DIM] = 1.0
    return jnp.asarray(m, BF16)


def _trunks(x_prompt, x_sample, ssm0_s, conv0_s, pool0_s, p):
    bp, lp, _ = x_prompt.shape
    bs, ls, _ = x_sample.shape
    rows = (bp * lp, bs * ls)
    bf = lambda a: a.astype(BF16)
    w_gate, w_up, w_down = bf(p["ffn_w_gate"]), bf(p["ffn_w_up"]), bf(p["ffn_w_down"])
    ffn = lambda xs, i, k, **kw: _ffn(xs, rows, p["ffn_norm"][i, k], w_gate, w_up, w_down, (i, k), **kw)
    x2d = ffn([x_prompt.reshape(rows[0], D_MODEL), x_sample.reshape(rows[1], D_MODEL)], 0, 0)

    w_in = p["ssd_w_in"][0]
    w_dt = w_in[:, D_INNER + CONV_DIM:]
    dt_bias = p["ssd_dt_bias"][0]
    wdt = jnp.pad(w_dt, ((0, 0), (0, DT_PAD - N_HEADS)))
    dtb = jnp.pad(dt_bias, (0, DT_PAD - N_HEADS))
    wdt_s = jnp.concatenate([wdt, jnp.repeat(w_dt, HEAD_DIM, axis=1)], axis=1)
    dtb_s = jnp.concatenate([dtb, jnp.repeat(dt_bias, HEAD_DIM)])
    a_neg = -jnp.exp(p["ssd_a_log"][0].astype(F32))
    a_pad, d_e = _pad_heads(a_neg), _expand_heads(p["ssd_d"][0])
    ng = p["ssd_norm"][0].reshape(1, D_INNER)
    common = (p["mix_norm"][0], bf(w_in))
    conv_w, conv_b = p["ssd_conv_w"][0], p["ssd_conv_b"][0]
    z, acum, tr, xbc, conv_p = _ssd_in_prompt(x2d, bp, lp, *common, bf(wdt), conv_w, conv_b, dtb,
                                              jnp.zeros((bp, CONV_W - 1, CONV_DIM), F32), a_pad)
    y_p, ssm_p = _scan_prompt(xbc, z, acum, tr, d_e, ng)
    z, xbc, dt, conv_s = _ssd_in(x2d, rows[0], bs, ls, *common, bf(wdt_s), conv_w, conv_b, dtb_s,
                                 conv0_s, nb=32, tl=SAMPLE_LEN)
    y_s, ssm_s = _scan_sample(xbc, z, dt, ssm0_s.reshape(bs, D_INNER, D_STATE), a_pad,
                              _expand_heads(a_neg), d_e, ng, _group_sum_matrix(),
                              _group_expand_matrix())
    x2d = ffn([x2d], 0, 1, pre=([y_p.reshape(rows[0], D_INNER), y_s.reshape(rows[1], D_INNER)],
                                bf(p["ssd_w_out"][0])))

    x2d = ffn([x2d], 1, 0)
    pool_w = (p["mix_norm"][1], bf(p["pool_w_in"][0]), bf(p["pool_w_group"][0]), p["pool_scale"][0],
              bf(p["pool_w_out"][0]))
    xm_p, pool_p = _pool(x2d, 0, bp, lp, *pool_w, jnp.zeros((bp, MAX_WIN - 1, D_MODEL), F32),
                         nb=1, tl=512, pos0=0)
    xm_s, pool_s = _pool(x2d, rows[0], bs, ls, *pool_w, pool0_s, nb=64, tl=SAMPLE_LEN, pos0=PAST_LEN)
    out_p, out_s = ffn([xm_p, xm_s], 1, 1, final_g=p["final_norm"], split_out=True)
    state = lambda a, b: a.reshape(b, N_HEADS, HEAD_DIM, D_STATE)[None]
    return (out_p.reshape(bp, lp, D_MODEL), out_s.reshape(bs, ls, D_MODEL),
            state(ssm_p, bp), conv_p[None], pool_p[None], state(ssm_s, bs), conv_s[None], pool_s[None])


def kernel(x_prompt, x_sample, state_ssm, state_conv, state_pool, ffn_norm, ffn_w_gate, ffn_w_up,
           ffn_w_down, mix_norm, ssd_w_in, ssd_conv_w, ssd_conv_b, ssd_dt_bias, ssd_a_log, ssd_d,
           ssd_norm, ssd_w_out, pool_w_in, pool_w_group, pool_scale, pool_w_out, final_norm):
    p = dict(ffn_norm=ffn_norm, ffn_w_gate=ffn_w_gate, ffn_w_up=ffn_w_up, ffn_w_down=ffn_w_down,
             mix_norm=mix_norm, ssd_w_in=ssd_w_in, ssd_conv_w=ssd_conv_w, ssd_conv_b=ssd_conv_b,
             ssd_dt_bias=ssd_dt_bias, ssd_a_log=ssd_a_log, ssd_d=ssd_d, ssd_norm=ssd_norm,
             ssd_w_out=ssd_w_out, pool_w_in=pool_w_in, pool_w_group=pool_w_group,
             pool_scale=pool_scale, pool_w_out=pool_w_out, final_norm=final_norm)
    return _trunks(x_prompt, x_sample, state_ssm[0], state_conv[0], state_pool[0], p)
```

```python
import functools

import numpy as np
import jax
import jax.numpy as jnp
from jax import lax
from jax.experimental import pallas as pl
from jax.experimental.pallas import tpu as pltpu

F32 = jnp.float32
BF16 = jnp.bfloat16

EPS = 1e-6
D_MODEL = 1024
D_FF = 2816
D_INNER = 2048
HEAD_DIM = 64
N_HEADS = 32
N_GROUPS = 8
HEADS_PER_GROUP = 4
GROUP_DIM = HEADS_PER_GROUP * HEAD_DIM
D_STATE = 128
CONV_W = 4
CONV_DIM = D_INNER + 2 * N_GROUPS * D_STATE
CHUNK = 128
POOL_WINDOWS = (2, 4, 8, 16)
POOL_GROUP_DIM = 256
MAX_WIN = 16
PAST_LEN = 16384
LANES = 128
SUBLANES = 8
DT_PAD = LANES
VMEM_LIMIT = 56 * 1024 * 1024

NT_DIMS = (((1,), (1,)), ((), ()))
LOG2E = 1.4426950408889634


def _resident(shape, lead=()):
    nd = len(shape)
    return pl.BlockSpec((None,) * len(lead) + tuple(shape), lambda *_: tuple(lead) + (0,) * nd,
                        pipeline_mode=pl.Buffered(1))


def _params(n_axes):
    return pltpu.CompilerParams(dimension_semantics=("arbitrary",) * n_axes,
                                vmem_limit_bytes=VMEM_LIMIT)


def _rmsnorm(x, g):
    ms = jnp.mean(x * x, axis=-1, keepdims=True)
    return x * lax.rsqrt(ms + EPS) * g


def _silu(x):
    return x * jax.nn.sigmoid(x)


def _softplus(x):
    return jnp.maximum(x, 0.0) + jnp.log(1.0 + jnp.exp(-jnp.abs(x)))


def _dot(a, b):
    return jnp.dot(a, b, preferred_element_type=F32)


def _dot_nt(a, b):
    return lax.dot_general(a, b, NT_DIMS, preferred_element_type=F32)


def _dot_exact01(t, x):
    hi = x.astype(BF16)
    r1 = x - hi.astype(F32)
    mid = r1.astype(BF16)
    lo = (r1 - mid.astype(F32)).astype(BF16)
    return _dot(t, hi) + _dot(t, mid) + _dot(t, lo)


FFN_TM = 512
FFN_TF = 256


def _ffn_kernel(*refs, n_x, n_y, final, n_out, tiles_a):
    refs = list(refs)
    x_refs = [refs.pop(0) for _ in range(n_x)]
    y_refs = [refs.pop(0) for _ in range(n_y)]
    wpre_ref = refs.pop(0) if n_y else None
    g_ref, wg_ref, wu_ref, wd_ref = refs[:4]
    del refs[:4]
    fg_ref = refs.pop(0) if final else None
    o_refs = [refs.pop(0) for _ in range(n_out)]
    h_ref, a_ref = refs
    second = pl.program_id(0) >= tiles_a

    def pick(rs, dtype):
        if len(rs) == 1:
            return rs[0][...].astype(dtype)
        return jnp.where(second, rs[1][...].astype(dtype), rs[0][...].astype(dtype))

    x = pick(x_refs, F32)
    if n_y:
        x = x + _dot(pick(y_refs, BF16), wpre_ref[...])
    h_ref[...] = _rmsnorm(x, g_ref[...]).astype(BF16)
    for f in range(0, D_FF, FFN_TF):
        h = h_ref[...]
        gate = _dot(h, wg_ref[:, f:f + FFN_TF])
        up = _dot(h, wu_ref[:, f:f + FFN_TF])
        a_ref[:, f:f + FFN_TF] = (_silu(gate) * up).astype(BF16)
    y = x + 0.5 * _dot(a_ref[...], wd_ref[...])
    if final:
        y = _rmsnorm(y, fg_ref[...])
    if n_out == 1:
        o_refs[0][...] = y
    else:
        @pl.when(jnp.logical_not(second))
        def _():
            o_refs[0][...] = y

        @pl.when(second)
        def _():
            o_refs[1][...] = y


def _ffn(xs, rows, g, w_gate, w_up, w_down, idx, final_g=None, pre=None, split_out=False):
    tm = FFN_TM
    tiles_a, tiles_b = rows[0] // tm, rows[1] // tm
    assert rows[0] % tm == 0 and rows[1] % tm == 0
    final = final_g is not None

    def row_specs(arrs):
        if len(arrs) == 1:
            return [pl.BlockSpec((tm, arrs[0].shape[1]), lambda i: (i, 0))]
        return [pl.BlockSpec((tm, arrs[0].shape[1]), lambda i: (jnp.minimum(i, tiles_a - 1), 0)),
                pl.BlockSpec((tm, arrs[1].shape[1]), lambda i: (jnp.maximum(i - tiles_a, 0), 0))]

    in_specs, args = row_specs(xs), list(xs)
    ys = []
    if pre is not None:
        ys, wpre = pre
        in_specs += row_specs(ys) + [_resident(wpre.shape)]
        args += list(ys) + [wpre]
    in_specs += [_resident((1, D_MODEL)), _resident((D_MODEL, D_FF), idx),
                 _resident((D_MODEL, D_FF), idx), _resident((D_FF, D_MODEL), idx)]
    args += [g.reshape(1, D_MODEL), w_gate, w_up, w_down]
    if final:
        in_specs.append(_resident((1, D_MODEL)))
        args.append(final_g.reshape(1, D_MODEL))
    if split_out:
        outs = [jax.ShapeDtypeStruct((r, D_MODEL), F32) for r in rows]
    else:
        outs = [jax.ShapeDtypeStruct((rows[0] + rows[1], D_MODEL), F32)]
    res = pl.pallas_call(
        functools.partial(_ffn_kernel, n_x=len(xs), n_y=len(ys), final=final, n_out=len(outs),
                          tiles_a=tiles_a),
        grid=(tiles_a + tiles_b,),
        in_specs=in_specs,
        out_specs=row_specs(outs),
        out_shape=outs,
        scratch_shapes=[pltpu.VMEM((tm, D_MODEL), BF16), pltpu.VMEM((tm, D_FF), BF16)],
        compiler_params=_params(1),
        name="ffn" + ("_pre" if pre is not None else "") + ("_final" if final else ""),
    )(*args)
    return res if split_out else res[0]


CONV_HALO = SUBLANES
SSD_IN_LANE_CHUNK = 512


def _shift_rows(v, k):
    if k == SUBLANES:
        return jnp.concatenate([v[:, :1], v[:, :-1]], axis=1)
    r = pltpu.roll(v, k, 2)
    prev = jnp.concatenate([r[:, :1], r[:, :-1]], axis=1)
    sub = lax.broadcasted_iota(jnp.int32, v.shape, 2)
    return jnp.where(sub < k, prev, r)


def _chunk_decays(dt, a_pad):
    q = dt.shape[0]
    tril = (lax.broadcasted_iota(jnp.int32, (q, q), 0) >= lax.broadcasted_iota(jnp.int32, (q, q), 1))
    acum = _dot_exact01(tril.astype(BF16), dt * a_pad) * LOG2E
    acum_t = acum.T
    dt_t = dt.T
    src_t = acum_t - jnp.log2(dt_t)
    w_t = jnp.exp2(acum_t[:, q - 1:q] - acum_t) * dt_t
    return acum, acum_t, src_t, w_t


def _conv_silu(ext, cw_ref, cb_ref, sl):
    w = ext.shape[-1]
    tap = lambda k: cw_ref[k:k + 1, sl].reshape(1, 1, 1, w)
    ext1 = _shift_rows(ext, 1)
    p = ext * tap(3) + ext1 * tap(2)
    q = ext * tap(1) + ext1 * tap(0)
    return _silu((cb_ref[:, sl].reshape(1, 1, 1, w) + p + _shift_rows(q, 2))[:, 1:])


def _ssd_in_kernel(x_ref, g_ref, win_ref, wdt_ref, cw_ref, cb_ref, dtb_ref, conv0_ref,
                   z_ref, xbc_ref, dt_ref, convnew_ref, halo_ref, h_ref, *, nb, tl):
    rows = nb * tl
    wc = SSD_IN_LANE_CHUNK
    nblk = tl // SUBLANES

    @pl.when(pl.program_id(1) == 0)
    def _():
        halo_ref[...] = jnp.zeros_like(halo_ref)
        halo_ref[:, CONV_HALO - (CONV_W - 1):, :] = conv0_ref[...]

    h_ref[...] = _rmsnorm(x_ref[...], g_ref[...]).astype(BF16)
    dt = _softplus(_dot(h_ref[...], wdt_ref[...]) + dtb_ref[...])
    dt_ref[...] = dt.reshape(nb, tl, dt.shape[-1])
    for c in range(0, CONV_DIM, wc):
        sl = slice(c, c + wc)
        zs = slice(c // 2, c // 2 + wc // 2)
        z_ref[:, :, zs] = _dot(h_ref[...], win_ref[:, zs]).reshape(nb, tl, wc // 2)
        cur = _dot(h_ref[...], win_ref[:, D_INNER + c:D_INNER + c + wc]).reshape(nb, tl, wc)
        ext = jnp.concatenate([halo_ref[:, :, sl], cur], axis=1).reshape(nb, nblk + 1, SUBLANES, wc)
        xbc_ref[:, :, sl] = _conv_silu(ext, cw_ref, cb_ref, sl).reshape(nb, tl, wc)
        halo_ref[:, :, sl] = cur[:, tl - CONV_HALO:, :]
    convnew_ref[...] = halo_ref[:, CONV_HALO - (CONV_W - 1):, :]


def _ssd_in(x2d, row0, b, l, g, w_in, wdt, cw, cb, dtb, conv0, *, nb, tl):
    assert l == tl and row0 % (nb * tl) == 0
    tile0 = row0 // (nb * tl)
    dt_w = wdt.shape[1]
    blk = lambda w: pl.BlockSpec((nb, tl, w), lambda i, j: (i, j, 0))
    return pl.pallas_call(
        functools.partial(_ssd_in_kernel, nb=nb, tl=tl),
        grid=(b // nb, l // tl),
        in_specs=[pl.BlockSpec((nb * tl, D_MODEL), lambda i, j: (tile0 + i, 0)),
                  _resident((1, D_MODEL)), _resident(w_in.shape), _resident((D_MODEL, dt_w)),
                  _resident((CONV_W, CONV_DIM)), _resident((1, CONV_DIM)), _resident((1, dt_w)),
                  pl.BlockSpec((nb, CONV_W - 1, CONV_DIM), lambda i, j: (i, 0, 0))],
        out_specs=[blk(D_INNER), blk(CONV_DIM), blk(dt_w),
                   pl.BlockSpec((nb, CONV_W - 1, CONV_DIM), lambda i, j: (i, 0, 0))],
        out_shape=[jax.ShapeDtypeStruct((b, l, D_INNER), F32),
                   jax.ShapeDtypeStruct((b, l, CONV_DIM), F32),
                   jax.ShapeDtypeStruct((b, l, dt_w), F32),
                   jax.ShapeDtypeStruct((b, CONV_W - 1, CONV_DIM), F32)],
        scratch_shapes=[pltpu.VMEM((nb, CONV_HALO, CONV_DIM), F32),
                        pltpu.VMEM((nb * tl, D_MODEL), BF16)],
        compiler_params=_params(2),
        name="ssd_in",
    )(x2d, g.reshape(1, D_MODEL), w_in, wdt, cw, cb.reshape(1, CONV_DIM),
      dtb.reshape(1, dt_w), conv0)


SSD_PIPE_TL = 512


def _ssd_in_prompt_kernel(x_ref, g_ref, win_ref, wdt_ref, cw_ref, cb_ref, dtb_ref, conv0_ref,
                          apad_ref, z_ref, acum_ref, tr_ref, xbc_ref, convnew_ref, halo_ref, h_ref):
    tl = SSD_PIPE_TL
    wc = SSD_IN_LANE_CHUNK
    nblk = tl // SUBLANES

    @pl.when(pl.program_id(1) == 0)
    def _():
        halo_ref[...] = jnp.zeros_like(halo_ref)
        halo_ref[CONV_HALO - (CONV_W - 1):, :] = conv0_ref[0]

    h_ref[...] = _rmsnorm(x_ref[...], g_ref[...]).astype(BF16)
    dt = _softplus(_dot(h_ref[...], wdt_ref[...]) + dtb_ref[...])
    for j in range(tl // CHUNK):
        cs = slice(j * CHUNK, (j + 1) * CHUNK)
        acum, acum_t, src_t, w_t = _chunk_decays(dt[cs, :], apad_ref[...])
        acum_ref[0, cs, :] = acum
        tr_ref[0, j, 0] = acum_t
        tr_ref[0, j, 1] = src_t
        tr_ref[0, j, 2] = w_t
    for c in range(0, CONV_DIM, wc):
        sl = slice(c, c + wc)
        zs = slice(c // 2, c // 2 + wc // 2)
        z_ref[0, :, zs] = _dot(h_ref[...], win_ref[:, zs]).astype(z_ref.dtype)
        cur = _dot(h_ref[...], win_ref[:, D_INNER + c:D_INNER + c + wc])
        ext = jnp.concatenate([halo_ref[:, sl], cur], axis=0).reshape(1, nblk + 1, SUBLANES, wc)
        xbc_ref[0, :, sl] = _conv_silu(ext, cw_ref, cb_ref, sl).reshape(tl, wc)
        halo_ref[:, sl] = cur[tl - CONV_HALO:, :]
    convnew_ref[0] = halo_ref[CONV_HALO - (CONV_W - 1):, :]


def _ssd_in_prompt(x2d, b, l, g, w_in, wdt, cw, cb, dtb, conv0, a_pad):
    tl = SSD_PIPE_TL
    tps = l // tl
    ncs = tl // CHUNK
    blk = lambda w: pl.BlockSpec((1, tl, w), lambda i, j: (i, j, 0))
    return pl.pallas_call(
        _ssd_in_prompt_kernel,
        grid=(b, l // tl),
        in_specs=[pl.BlockSpec((tl, D_MODEL), lambda i, j: (i * tps + j, 0)),
                  _resident((1, D_MODEL)), _resident(w_in.shape), _resident((D_MODEL, DT_PAD)),
                  _resident((CONV_W, CONV_DIM)), _resident((1, CONV_DIM)), _resident((1, DT_PAD)),
                  pl.BlockSpec((1, CONV_W - 1, CONV_DIM), lambda i, j: (i, 0, 0)),
                  _resident((1, DT_PAD))],
        out_specs=[blk(D_INNER), blk(DT_PAD),
                   pl.BlockSpec((1, ncs, 3, DT_PAD, CHUNK), lambda i, j: (i, j, 0, 0, 0)),
                   blk(CONV_DIM),
                   pl.BlockSpec((1, CONV_W - 1, CONV_DIM), lambda i, j: (i, 0, 0))],
        out_shape=[jax.ShapeDtypeStruct((b, l, D_INNER), BF16),
                   jax.ShapeDtypeStruct((b, l, DT_PAD), F32),
                   jax.ShapeDtypeStruct((b, l // CHUNK, 3, DT_PAD, CHUNK), F32),
                   jax.ShapeDtypeStruct((b, l, CONV_DIM), F32),
                   jax.ShapeDtypeStruct((b, CONV_W - 1, CONV_DIM), F32)],
        scratch_shapes=[pltpu.VMEM((CONV_HALO, CONV_DIM), F32), pltpu.VMEM((tl, D_MODEL), BF16)],
        compiler_params=_params(2),
        name="ssd_in_prompt",
    )(x2d, g.reshape(1, D_MODEL), w_in, wdt, cw, cb.reshape(1, CONV_DIM),
      dtb.reshape(1, DT_PAD), conv0, a_pad)


B_OFF = D_INNER
C_OFF = D_INNER + N_GROUPS * D_STATE


def _gate_norm(y, z, ng):
    yg = y * _silu(z)
    ms = jnp.mean(yg * yg, axis=-1, keepdims=True)
    return yg * lax.rsqrt(ms + EPS) * ng


def _head_rows(mat, g):
    return jnp.concatenate(
        [jnp.broadcast_to(mat[g * HEADS_PER_GROUP + r:g * HEADS_PER_GROUP + r + 1, :],
                          (HEAD_DIM, mat.shape[1])) for r in range(HEADS_PER_GROUP)], axis=0)


def _scan_prompt_kernel(xbc_ref, z_ref, acum_ref, tr_ref, de_ref, ng_ref, y_ref, hout_ref, *h_refs,
                        chunks):
    q = CHUNK
    step = pl.program_id(1)

    @pl.when(step == 0)
    def _():
        for h_ref in h_refs:
            h_ref[...] = jnp.zeros_like(h_ref)

    tril = (lax.broadcasted_iota(jnp.int32, (q, q), 0) >= lax.broadcasted_iota(jnp.int32, (q, q), 1))
    lane_g = lax.broadcasted_iota(jnp.int32, (q, GROUP_DIM), 1)
    lane_half = lax.broadcasted_iota(jnp.int32, (q, LANES), 1) < HEAD_DIM

    for ci in range(chunks):
        ts = slice(ci * q, (ci + 1) * q)
        acum = acum_ref[0, ts, :]
        acum_t, src_t, w_t = tr_ref[0, ci, 0], tr_ref[0, ci, 1], tr_ref[0, ci, 2]
        cd_b = jnp.broadcast_to(jnp.exp2(acum_t[:, q - 1:q]), (DT_PAD, D_STATE))
        for g in range(N_GROUPS):
            gs = slice(g * GROUP_DIM, (g + 1) * GROUP_DIM)
            bg = xbc_ref[0, ts, B_OFF + g * D_STATE:B_OFF + (g + 1) * D_STATE].astype(BF16)
            cg = xbc_ref[0, ts, C_OFF + g * D_STATE:C_OFF + (g + 1) * D_STATE].astype(BF16)
            xg = xbc_ref[0, ts, gs]
            xgb = xg.astype(BF16)
            cb = _dot_nt(cg, bg)
            ms, cols = [], []
            for r in range(HEADS_PER_GROUP):
                hd = g * HEADS_PER_GROUP + r
                colf = jnp.broadcast_to(acum[:, hd:hd + 1], (q, q))
                rowf = jnp.broadcast_to(src_t[hd:hd + 1, :], (q, q))
                ms.append((cb * jnp.exp2(jnp.where(tril, colf - rowf, -jnp.inf))).astype(BF16))
                cols.append(colf)
            mcat = jnp.concatenate(ms, axis=1)
            zero = jnp.zeros_like(xgb)
            bd = jnp.concatenate(
                [jnp.where((lane_g >= r * HEAD_DIM) & (lane_g < (r + 1) * HEAD_DIM), xgb, zero)
                 for r in range(HEADS_PER_GROUP)], axis=0)
            y_diag = _dot(mcat, bd)
            hg = h_refs[g][...]
            y_off = _dot_nt(cg, hg.astype(BF16))
            acum_e = jnp.concatenate([jnp.where(lane_half, cols[0], cols[1]),
                                      jnp.where(lane_half, cols[2], cols[3])], axis=1)
            y = y_diag + y_off * jnp.exp2(acum_e) + de_ref[:, gs] * xg
            xg_t = xg.T
            wt = jnp.concatenate(
                [xg_t[r * HEAD_DIM:(r + 1) * HEAD_DIM, :] * w_t[g * HEADS_PER_GROUP + r:
                                                                g * HEADS_PER_GROUP + r + 1, :]
                 for r in range(HEADS_PER_GROUP)], axis=0).astype(BF16)
            h_refs[g][...] = hg * _head_rows(cd_b, g) + _dot(wt, bg)
            zg = z_ref[0, ts, gs].astype(F32)
            y_ref[0, ts, gs] = _gate_norm(y, zg, ng_ref[:, gs]).astype(y_ref.dtype)

    @pl.when(step == pl.num_programs(1) - 1)
    def _():
        for g, h_ref in enumerate(h_refs):
            hout_ref[0, g * GROUP_DIM:(g + 1) * GROUP_DIM, :] = h_ref[...]


SCAN_CHUNKS_PER_STEP = 1


def _scan_prompt(xbc, z, acum, tr, d_e, ng):
    b, l, _ = xbc.shape
    chunks = SCAN_CHUNKS_PER_STEP
    tl = chunks * CHUNK
    blk = lambda w: pl.BlockSpec((1, tl, w), lambda i, j: (i, j, 0))
    return pl.pallas_call(
        functools.partial(_scan_prompt_kernel, chunks=chunks),
        grid=(b, l // tl),
        in_specs=[blk(CONV_DIM), blk(D_INNER), blk(DT_PAD),
                  pl.BlockSpec((1, chunks, 3, DT_PAD, CHUNK), lambda i, j: (i, j, 0, 0, 0)),
                  _resident((1, D_INNER)), _resident((1, D_INNER))],
        out_specs=[blk(D_INNER), pl.BlockSpec((1, D_INNER, D_STATE), lambda i, j: (i, 0, 0))],
        out_shape=[jax.ShapeDtypeStruct((b, l, D_INNER), BF16),
                   jax.ShapeDtypeStruct((b, D_INNER, D_STATE), F32)],
        scratch_shapes=[pltpu.VMEM((GROUP_DIM, D_STATE), F32)] * N_GROUPS,
        compiler_params=_params(2),
        name="ssd_scan_prompt",
    )(xbc, z, acum, tr, d_e, ng)


SAMPLE_LEN = 8
SCAN_NB = 16
SCAN_ROWS = SCAN_NB * SAMPLE_LEN
SCAN_STEP_NB = 4


def _scan_sample_kernel(xbc_ref, z_ref, dt_ref, h0_ref, apad_ref, ae_ref, de_ref, ng_ref,
                        gsum_ref, gexp_ref, y_ref, hout_ref,
                        wt_ref, ea_ref, yd_ref, acp_ref, p_ref):
    rows = SCAN_ROWS
    j = pl.program_id(1)

    @pl.when(j == 0)
    def _():
        shape3 = lambda w: (SCAN_NB, SAMPLE_LEN, w)
        tok = lax.broadcasted_iota(jnp.int32, (rows, D_INNER), 0) & (SAMPLE_LEN - 1)
        tok_p = lax.broadcasted_iota(jnp.int32, (rows, DT_PAD), 0) & (SAMPLE_LEN - 1)

        def cumsum_tokens(v, t):
            for sh in (1, 2, 4):
                v = v + jnp.where(t >= sh, pltpu.roll(v, sh, 0), 0.0)
            return v

        def bcast_token(v, s):
            w = v.shape[-1]
            v3 = v.reshape(shape3(w))
            return jnp.broadcast_to(v3[:, s:s + 1, :], shape3(w)).reshape(rows, w)

        xs = xbc_ref[:, :, 0:D_INNER].reshape(rows, D_INNER)
        bm = xbc_ref[:, :, B_OFF:C_OFF].reshape(rows, N_GROUPS * D_STATE)
        cm = xbc_ref[:, :, C_OFF:CONV_DIM].reshape(rows, N_GROUPS * D_STATE)
        dt_p = dt_ref[:, :, 0:DT_PAD].reshape(rows, DT_PAD)
        dt_e = dt_ref[:, :, DT_PAD:DT_PAD + D_INNER].reshape(rows, D_INNER)
        acp_ref[...] = cumsum_tokens(dt_p * apad_ref[...], tok_p)
        acum_e = cumsum_tokens(dt_e * ae_ref[...], tok)
        xdt = xs * dt_e
        w = xdt * jnp.exp(bcast_token(acum_e, SAMPLE_LEN - 1) - acum_e)
        wt_ref[...] = w.T.astype(BF16)
        ea_ref[...] = jnp.exp(acum_e)
        for s in range(SAMPLE_LEN):
            p_ref[s * rows:(s + 1) * rows, :] = (cm * bcast_token(bm, s)).astype(BF16)
        cb_sum = _dot(p_ref[...], gsum_ref[...])
        yd = de_ref[...] * xs
        for s in range(SAMPLE_LEN):
            cb_e = _dot(cb_sum[s * rows:(s + 1) * rows, :].astype(BF16), gexp_ref[...])
            diff = acum_e - bcast_token(acum_e, s)
            decay = jnp.exp(jnp.where(tok >= s, diff, -jnp.inf))
            yd = yd + cb_e * decay * bcast_token(xdt, s)
        yd_ref[...] = yd

    eye = (lax.broadcasted_iota(jnp.int32, (N_HEADS, DT_PAD), 0)
           == lax.broadcasted_iota(jnp.int32, (N_HEADS, DT_PAD), 1))
    rowid = lax.broadcasted_iota(jnp.int32, (rows, D_STATE), 0)
    for t in range(SCAN_STEP_NB):
        seq = j * SCAN_STEP_NB + t
        r0 = pl.multiple_of(seq * SAMPLE_LEN, SAMPLE_LEN)
        alast = acp_ref[pl.ds(r0 + SAMPLE_LEN - 1, 1), :]
        alast_col = jnp.sum(jnp.where(eye, jnp.broadcast_to(alast, (N_HEADS, DT_PAD)), 0.0),
                            axis=1, keepdims=True)
        cd_b = jnp.broadcast_to(jnp.exp(alast_col), (N_HEADS, D_STATE))
        mine = (rowid >= r0) & (rowid < r0 + SAMPLE_LEN)
        y_offs = []
        for g in range(N_GROUPS):
            gs = slice(g * GROUP_DIM, (g + 1) * GROUP_DIM)
            hg = h0_ref[t, gs, :]
            cg = xbc_ref[seq, :, C_OFF + g * D_STATE:C_OFF + (g + 1) * D_STATE].astype(BF16)
            y_offs.append(_dot_nt(cg, hg.astype(BF16)))
            b_all = xbc_ref[:, :, B_OFF + g * D_STATE:B_OFF + (g + 1) * D_STATE].reshape(
                rows, D_STATE)
            b_mine = jnp.where(mine, b_all, 0.0).astype(BF16)
            hout_ref[t, gs, :] = hg * _head_rows(cd_b, g) + _dot(wt_ref[gs, :], b_mine)
        y = yd_ref[pl.ds(r0, SAMPLE_LEN), :] + jnp.concatenate(y_offs, axis=1) * ea_ref[
            pl.ds(r0, SAMPLE_LEN), :]
        z = z_ref[seq]
        y_ref[seq] = jnp.concatenate(
            [_gate_norm(y[:, g * GROUP_DIM:(g + 1) * GROUP_DIM],
                        z[:, g * GROUP_DIM:(g + 1) * GROUP_DIM],
                        ng_ref[:, g * GROUP_DIM:(g + 1) * GROUP_DIM]) for g in range(N_GROUPS)],
            axis=1)


def _scan_sample(xbc, z, dt, h0, a_pad, a_e, d_e, ng, gsum, gexp):
    b = xbc.shape[0]
    dt_w = dt.shape[-1]
    blk = lambda w: pl.BlockSpec((SCAN_NB, SAMPLE_LEN, w), lambda i, j: (i, 0, 0))
    steps = SCAN_NB // SCAN_STEP_NB
    st = pl.BlockSpec((SCAN_STEP_NB, D_INNER, D_STATE), lambda i, j: (i * steps + j, 0, 0))
    return pl.pallas_call(
        _scan_sample_kernel,
        grid=(b // SCAN_NB, steps),
        in_specs=[blk(CONV_DIM), blk(D_INNER), blk(dt_w), st, _resident((1, DT_PAD)),
                  _resident((1, D_INNER)), _resident((1, D_INNER)), _resident((1, D_INNER)),
                  _resident(gsum.shape), _resident(gexp.shape)],
        out_specs=[blk(D_INNER), st],
        out_shape=[jax.ShapeDtypeStruct((b, SAMPLE_LEN, D_INNER), F32),
                   jax.ShapeDtypeStruct((b, D_INNER, D_STATE), F32)],
        scratch_shapes=[pltpu.VMEM((D_INNER, SCAN_ROWS), BF16),
                        pltpu.VMEM((SCAN_ROWS, D_INNER), F32),
                        pltpu.VMEM((SCAN_ROWS, D_INNER), F32),
                        pltpu.VMEM((SCAN_ROWS, DT_PAD), F32),
                        pltpu.VMEM((SAMPLE_LEN * SCAN_ROWS, N_GROUPS * D_STATE), BF16)],
        compiler_params=_params(2),
        name="ssd_scan_sample",
    )(xbc, z, dt, h0, a_pad, a_e, d_e, ng, gsum, gexp)


POOL_HALO = 2 * SUBLANES


def _pool_kernel(x_ref, g_ref, win_ref, wgrp_ref, scale_ref, wout_ref, buf0_ref,
                 o_ref, bufnew_ref, ext_ref, *, nb, tl, pos0):
    rows = nb * tl
    jt = pl.program_id(1)

    @pl.when(jt == 0)
    def _():
        ext_ref[:, 0:POOL_HALO, :] = jnp.zeros((nb, POOL_HALO, D_MODEL), F32)
        ext_ref[:, POOL_HALO - (MAX_WIN - 1):POOL_HALO, :] = buf0_ref[...]

    x = x_ref[...]
    h = _rmsnorm(x, g_ref[...]).astype(BF16)
    ext_ref[:, POOL_HALO:POOL_HALO + tl, :] = _dot(h, win_ref[...]).reshape(nb, tl, D_MODEL)
    nblk = tl // SUBLANES
    halo_blocks = POOL_HALO // SUBLANES
    shape4 = (nb, nblk, SUBLANES, POOL_GROUP_DIM)
    pos = (pos0 + jt * tl + SUBLANES * lax.broadcasted_iota(jnp.int32, shape4, 1)
           + lax.broadcasted_iota(jnp.int32, shape4, 2)).astype(F32)
    mixed = []
    for k, w in enumerate(POOL_WINDOWS):
        sl = slice(k * POOL_GROUP_DIM, (k + 1) * POOL_GROUP_DIM)
        ext = ext_ref[:, :, sl].reshape(nb, nblk + halo_blocks, SUBLANES, POOL_GROUP_DIM)
        tot, shift = ext, 1
        while shift < w:
            tot = tot + _shift_rows(tot, shift)
            shift *= 2
        u = ext[:, halo_blocks:]
        mean = tot[:, halo_blocks:] / jnp.minimum(jnp.float32(w), pos + 1.0)
        m = (mean - u).reshape(rows, POOL_GROUP_DIM).astype(BF16)
        mixed.append(_dot(m, wgrp_ref[k]))
    mixed = (jnp.concatenate(mixed, axis=1) * scale_ref[...]).astype(BF16)
    o_ref[...] = x + _dot(mixed, wout_ref[...])
    bufnew_ref[...] = ext_ref[:, tl + 1:tl + POOL_HALO, :]
    ext_ref[:, 0:POOL_HALO, :] = ext_ref[:, tl:tl + POOL_HALO, :]


def _pool(x2d, row0, b, l, g, w_in, w_grp, scale, w_out, buf0, *, nb, tl, pos0):
    assert row0 % (nb * tl) == 0 and (nb == 1 or l == tl)
    tile0, tps = row0 // (nb * tl), l // tl
    return pl.pallas_call(
        functools.partial(_pool_kernel, nb=nb, tl=tl, pos0=pos0),
        grid=(b // nb, tps),
        in_specs=[pl.BlockSpec((nb * tl, D_MODEL), lambda i, j: (tile0 + i * tps + j, 0)),
                  _resident((1, D_MODEL)), _resident((D_MODEL, D_MODEL)),
                  _resident(w_grp.shape), _resident((1, D_MODEL)), _resident((D_MODEL, D_MODEL)),
                  pl.BlockSpec((nb, MAX_WIN - 1, D_MODEL), lambda i, j: (i, 0, 0))],
        out_specs=[pl.BlockSpec((nb * tl, D_MODEL), lambda i, j: (i * tps + j, 0)),
                   pl.BlockSpec((nb, MAX_WIN - 1, D_MODEL), lambda i, j: (i, 0, 0))],
        out_shape=[jax.ShapeDtypeStruct((b * l, D_MODEL), F32),
                   jax.ShapeDtypeStruct((b, MAX_WIN - 1, D_MODEL), F32)],
        scratch_shapes=[pltpu.VMEM((nb, POOL_HALO + tl, D_MODEL), F32)],
        compiler_params=_params(2),
        name="pool_mixer",
    )(x2d, g.reshape(1, D_MODEL), w_in, w_grp, scale.reshape(1, D_MODEL), w_out, buf0)


def _expand_heads(v):
    return jnp.repeat(v.astype(F32), HEAD_DIM).reshape(1, D_INNER)


def _pad_heads(v):
    return jnp.pad(v.astype(F32), (0, DT_PAD - N_HEADS)).reshape(1, DT_PAD)


def _group_sum_matrix():
    m = np.zeros((N_GROUPS * D_STATE, LANES), np.float32)
    for g in range(N_GROUPS):
        m[g * D_STATE:(g + 1) * D_STATE, g] = 1.0
    return jnp.asarray(m, BF16)


def _group_expand_matrix():
    m = np.zeros((LANES, D_INNER), np.float32)
    for g in range(N_GROUPS):
        m[g, g * GROUP_DIM:(g + 1) * GROUP_DIM] = 1.0
    return jnp.asarray(m, BF16)


def _trunks(x_prompt, x_sample, ssm0_s, conv0_s, pool0_s, p):
    bp, lp, _ = x_prompt.shape
    bs, ls, _ = x_sample.shape
    rows = (bp * lp, bs * ls)
    bf = lambda a: a.astype(BF16)
    w_gate, w_up, w_down = bf(p["ffn_w_gate"]), bf(p["ffn_w_up"]), bf(p["ffn_w_down"])
    ffn = lambda xs, i, k, **kw: _ffn(xs, rows, p["ffn_norm"][i, k], w_gate, w_up, w_down, (i, k), **kw)
    x2d = ffn([x_prompt.reshape(rows[0], D_MODEL), x_sample.reshape(rows[1], D_MODEL)], 0, 0)

    w_in = p["ssd_w_in"][0]
    w_dt = w_in[:, D_INNER + CONV_DIM:]
    dt_bias = p["ssd_dt_bias"][0]
    wdt = jnp.pad(w_dt, ((0, 0), (0, DT_PAD - N_HEADS)))
    dtb = jnp.pad(dt_bias, (0, DT_PAD - N_HEADS))
    wdt_s = jnp.concatenate([wdt, jnp.repeat(w_dt, HEAD_DIM, axis=1)], axis=1)
    dtb_s = jnp.concatenate([dtb, jnp.repeat(dt_bias, HEAD_DIM)])
    a_neg = -jnp.exp(p["ssd_a_log"][0].astype(F32))
    a_pad, d_e = _pad_heads(a_neg), _expand_heads(p["ssd_d"][0])
    ng = p["ssd_norm"][0].reshape(1, D_INNER)
    common = (p["mix_norm"][0], bf(w_in))
    conv_w, conv_b = p["ssd_conv_w"][0], p["ssd_conv_b"][0]
    z, acum, tr, xbc, conv_p = _ssd_in_prompt(x2d, bp, lp, *common, bf(wdt), conv_w, conv_b, dtb,
                                              jnp.zeros((bp, CONV_W - 1, CONV_DIM), F32), a_pad)
    y_p, ssm_p = _scan_prompt(xbc, z, acum, tr, d_e, ng)
    z, xbc, dt, conv_s = _ssd_in(x2d, rows[0], bs, ls, *common, bf(wdt_s), conv_w, conv_b, dtb_s,
                                 conv0_s, nb=32, tl=SAMPLE_LEN)
    y_s, ssm_s = _scan_sample(xbc, z, dt, ssm0_s.reshape(bs, D_INNER, D_STATE), a_pad,
                              _expand_heads(a_neg), d_e, ng, _group_sum_matrix(),
                              _group_expand_matrix())
    x2d = ffn([x2d], 0, 1, pre=([y_p.reshape(rows[0], D_INNER), y_s.reshape(rows[1], D_INNER)],
                                bf(p["ssd_w_out"][0])))

    x2d = ffn([x2d], 1, 0)
    pool_w = (p["mix_norm"][1], bf(p["pool_w_in"][0]), bf(p["pool_w_group"][0]), p["pool_scale"][0],
              bf(p["pool_w_out"][0]))
    xm_p, pool_p = _pool(x2d, 0, bp, lp, *pool_w, jnp.zeros((bp, MAX_WIN - 1, D_MODEL), F32),
                         nb=1, tl=512, pos0=0)
    xm_s, pool_s = _pool(x2d, rows[0], bs, ls, *pool_w, pool0_s, nb=64, tl=SAMPLE_LEN, pos0=PAST_LEN)
    out_p, out_s = ffn([xm_p, xm_s], 1, 1, final_g=p["final_norm"], split_out=True)
    state = lambda a, b: a.reshape(b, N_HEADS, HEAD_DIM, D_STATE)[None]
    return (out_p.reshape(bp, lp, D_MODEL), out_s.reshape(bs, ls, D_MODEL),
            state(ssm_p, bp), conv_p[None], pool_p[None], state(ssm_s, bs), conv_s[None], pool_s[None])


def kernel(x_prompt, x_sample, state_ssm, state_conv, state_pool, ffn_norm, ffn_w_gate, ffn_w_up,
           ffn_w_down, mix_norm, ssd_w_in, ssd_conv_w, ssd_conv_b, ssd_dt_bias, ssd_a_log, ssd_d,
           ssd_norm, ssd_w_out, pool_w_in, pool_w_group, pool_scale, pool_w_out, final_norm):
    p = dict(ffn_norm=ffn_norm, ffn_w_gate=ffn_w_gate, ffn_w_up=ffn_w_up, ffn_w_down=ffn_w_down,
             mix_norm=mix_norm, ssd_w_in=ssd_w_in, ssd_conv_w=ssd_conv_w, ssd_conv_b=ssd_conv_b,
             ssd_dt_bias=ssd_dt_bias, ssd_a_log=ssd_a_log, ssd_d=ssd_d, ssd_norm=ssd_norm,
             ssd_w_out=ssd_w_out, pool_w_in=pool_w_in, pool_w_group=pool_w_group,
             pool_scale=pool_scale, pool_w_out=pool_w_out, final_norm=final_norm)
    return _trunks(x_prompt, x_sample, state_ssm[0], state_conv[0], state_pool[0], p)
```

```python
import functools

import numpy as np
import jax
import jax.numpy as jnp
from jax import lax
from jax.experimental import pallas as pl
from jax.experimental.pallas import tpu as pltpu

F32 = jnp.float32
BF16 = jnp.bfloat16

EPS = 1e-6
D_MODEL = 1024
D_FF = 2816
D_INNER = 2048
HEAD_DIM = 64
N_HEADS = 32
N_GROUPS = 8
HEADS_PER_GROUP = 4
GROUP_DIM = HEADS_PER_GROUP * HEAD_DIM
D_STATE = 128
CONV_W = 4
CONV_DIM = D_INNER + 2 * N_GROUPS * D_STATE
CHUNK = 128
POOL_WINDOWS = (2, 4, 8, 16)
POOL_GROUP_DIM = 256
MAX_WIN = 16
PAST_LEN = 16384
LANES = 128
SUBLANES = 8
DT_PAD = LANES
VMEM_LIMIT = 56 * 1024 * 1024

NT_DIMS = (((1,), (1,)), ((), ()))
LOG2E = 1.4426950408889634


def _resident(shape, lead=()):
    nd = len(shape)
    return pl.BlockSpec((None,) * len(lead) + tuple(shape), lambda *_: tuple(lead) + (0,) * nd,
                        pipeline_mode=pl.Buffered(1))


def _params(n_axes):
    return pltpu.CompilerParams(dimension_semantics=("arbitrary",) * n_axes,
                                vmem_limit_bytes=VMEM_LIMIT)


def _rmsnorm(x, g):
    ms = jnp.mean(x * x, axis=-1, keepdims=True)
    return x * lax.rsqrt(ms + EPS) * g


def _silu(x):
    return x * jax.nn.sigmoid(x)


def _softplus(x):
    return jnp.maximum(x, 0.0) + jnp.log(1.0 + jnp.exp(-jnp.abs(x)))


def _dot(a, b):
    return jnp.dot(a, b, preferred_element_type=F32)


def _dot_nt(a, b):
    return lax.dot_general(a, b, NT_DIMS, preferred_element_type=F32)


def _dot_exact01(t, x):
    hi = x.astype(BF16)
    r1 = x - hi.astype(F32)
    mid = r1.astype(BF16)
    lo = (r1 - mid.astype(F32)).astype(BF16)
    return _dot(t, hi) + _dot(t, mid) + _dot(t, lo)


FFN_TM = 512
FFN_TF = 256


def _ffn_kernel(*refs, n_x, n_y, final, n_out, tiles_a):
    refs = list(refs)
    x_refs = [refs.pop(0) for _ in range(n_x)]
    y_refs = [refs.pop(0) for _ in range(n_y)]
    wpre_ref = refs.pop(0) if n_y else None
    g_ref, wg_ref, wu_ref, wd_ref = refs[:4]
    del refs[:4]
    fg_ref = refs.pop(0) if final else None
    o_refs = [refs.pop(0) for _ in range(n_out)]
    h_ref, a_ref = refs
    second = pl.program_id(0) >= tiles_a

    def pick(rs, dtype):
        if len(rs) == 1:
            return rs[0][...].astype(dtype)
        return jnp.where(second, rs[1][...].astype(dtype), rs[0][...].astype(dtype))

    x = pick(x_refs, F32)
    if n_y:
        x = x + _dot(pick(y_refs, BF16), wpre_ref[...])
    h_ref[...] = _rmsnorm(x, g_ref[...]).astype(BF16)
    for f in range(0, D_FF, FFN_TF):
        h = h_ref[...]
        gate = _dot(h, wg_ref[:, f:f + FFN_TF])
        up = _dot(h, wu_ref[:, f:f + FFN_TF])
        a_ref[:, f:f + FFN_TF] = (_silu(gate) * up).astype(BF16)
    y = x + 0.5 * _dot(a_ref[...], wd_ref[...])
    if final:
        y = _rmsnorm(y, fg_ref[...])
    if n_out == 1:
        o_refs[0][...] = y
    else:
        @pl.when(jnp.logical_not(second))
        def _():
            o_refs[0][...] = y

        @pl.when(second)
        def _():
            o_refs[1][...] = y


def _ffn(xs, rows, g, w_gate, w_up, w_down, idx, final_g=None, pre=None, split_out=False):
    tm = FFN_TM
    tiles_a, tiles_b = rows[0] // tm, rows[1] // tm
    assert rows[0] % tm == 0 and rows[1] % tm == 0
    final = final_g is not None

    def row_specs(arrs):
        if len(arrs) == 1:
            return [pl.BlockSpec((tm, arrs[0].shape[1]), lambda i: (i, 0))]
        return [pl.BlockSpec((tm, arrs[0].shape[1]), lambda i: (jnp.minimum(i, tiles_a - 1), 0)),
                pl.BlockSpec((tm, arrs[1].shape[1]), lambda i: (jnp.maximum(i - tiles_a, 0), 0))]

    in_specs, args = row_specs(xs), list(xs)
    ys = []
    if pre is not None:
        ys, wpre = pre
        in_specs += row_specs(ys) + [_resident(wpre.shape)]
        args += list(ys) + [wpre]
    in_specs += [_resident((1, D_MODEL)), _resident((D_MODEL, D_FF), idx),
                 _resident((D_MODEL, D_FF), idx), _resident((D_FF, D_MODEL), idx)]
    args += [g.reshape(1, D_MODEL), w_gate, w_up, w_down]
    if final:
        in_specs.append(_resident((1, D_MODEL)))
        args.append(final_g.reshape(1, D_MODEL))
    if split_out:
        outs = [jax.ShapeDtypeStruct((r, D_MODEL), F32) for r in rows]
    else:
        outs = [jax.ShapeDtypeStruct((rows[0] + rows[1], D_MODEL), F32)]
    res = pl.pallas_call(
        functools.partial(_ffn_kernel, n_x=len(xs), n_y=len(ys), final=final, n_out=len(outs),
                          tiles_a=tiles_a),
        grid=(tiles_a + tiles_b,),
        in_specs=in_specs,
        out_specs=row_specs(outs),
        out_shape=outs,
        scratch_shapes=[pltpu.VMEM((tm, D_MODEL), BF16), pltpu.VMEM((tm, D_FF), BF16)],
        compiler_params=_params(1),
        name="ffn" + ("_pre" if pre is not None else "") + ("_final" if final else ""),
    )(*args)
    return res if split_out else res[0]


CONV_HALO = SUBLANES
SSD_IN_LANE_CHUNK = 512


def _shift_rows(v, k):
    if k == SUBLANES:
        return jnp.concatenate([v[:, :1], v[:, :-1]], axis=1)
    r = pltpu.roll(v, k, 2)
    prev = jnp.concatenate([r[:, :1], r[:, :-1]], axis=1)
    sub = lax.broadcasted_iota(jnp.int32, v.shape, 2)
    return jnp.where(sub < k, prev, r)


def _chunk_decays(dt, a_pad):
    q = dt.shape[0]
    tril = (lax.broadcasted_iota(jnp.int32, (q, q), 0) >= lax.broadcasted_iota(jnp.int32, (q, q), 1))
    acum = _dot_exact01(tril.astype(BF16), dt * a_pad) * LOG2E
    acum_t = acum.T
    dt_t = dt.T
    src_t = acum_t - jnp.log2(dt_t)
    w_t = jnp.exp2(acum_t[:, q - 1:q] - acum_t) * dt_t
    return acum, acum_t, src_t, w_t


def _conv_silu(ext, cw_ref, cb_ref, sl):
    w = ext.shape[-1]
    tap = lambda k: cw_ref[k:k + 1, sl].reshape(1, 1, 1, w)
    ext1 = _shift_rows(ext, 1)
    p = ext * tap(3) + ext1 * tap(2)
    q = ext * tap(1) + ext1 * tap(0)
    return _silu((cb_ref[:, sl].reshape(1, 1, 1, w) + p + _shift_rows(q, 2))[:, 1:])


def _ssd_in_kernel(x_ref, g_ref, win_ref, wdt_ref, cw_ref, cb_ref, dtb_ref, conv0_ref,
                   z_ref, xbc_ref, dt_ref, convnew_ref, halo_ref, h_ref, *, nb, tl):
    rows = nb * tl
    wc = SSD_IN_LANE_CHUNK
    nblk = tl // SUBLANES

    @pl.when(pl.program_id(1) == 0)
    def _():
        halo_ref[...] = jnp.zeros_like(halo_ref)
        halo_ref[:, CONV_HALO - (CONV_W - 1):, :] = conv0_ref[...]

    h_ref[...] = _rmsnorm(x_ref[...], g_ref[...]).astype(BF16)
    dt = _softplus(_dot(h_ref[...], wdt_ref[...]) + dtb_ref[...])
    dt_ref[...] = dt.reshape(nb, tl, dt.shape[-1])
    for c in range(0, CONV_DIM, wc):
        sl = slice(c, c + wc)
        zs = slice(c // 2, c // 2 + wc // 2)
        z_ref[:, :, zs] = _dot(h_ref[...], win_ref[:, zs]).reshape(nb, tl, wc // 2)
        cur = _dot(h_ref[...], win_ref[:, D_INNER + c:D_INNER + c + wc]).reshape(nb, tl, wc)
        ext = jnp.concatenate([halo_ref[:, :, sl], cur], axis=1).reshape(nb, nblk + 1, SUBLANES, wc)
        xbc_ref[:, :, sl] = _conv_silu(ext, cw_ref, cb_ref, sl).reshape(nb, tl, wc)
        halo_ref[:, :, sl] = cur[:, tl - CONV_HALO:, :]
    convnew_ref[...] = halo_ref[:, CONV_HALO - (CONV_W - 1):, :]


def _ssd_in(x2d, row0, b, l, g, w_in, wdt, cw, cb, dtb, conv0, *, nb, tl):
    assert l == tl and row0 % (nb * tl) == 0
    tile0 = row0 // (nb * tl)
    dt_w = wdt.shape[1]
    blk = lambda w: pl.BlockSpec((nb, tl, w), lambda i, j: (i, j, 0))
    return pl.pallas_call(
        functools.partial(_ssd_in_kernel, nb=nb, tl=tl),
        grid=(b // nb, l // tl),
        in_specs=[pl.BlockSpec((nb * tl, D_MODEL), lambda i, j: (tile0 + i, 0)),
                  _resident((1, D_MODEL)), _resident(w_in.shape), _resident((D_MODEL, dt_w)),
                  _resident((CONV_W, CONV_DIM)), _resident((1, CONV_DIM)), _resident((1, dt_w)),
                  pl.BlockSpec((nb, CONV_W - 1, CONV_DIM), lambda i, j: (i, 0, 0))],
        out_specs=[blk(D_INNER), blk(CONV_DIM), blk(dt_w),
                   pl.BlockSpec((nb, CONV_W - 1, CONV_DIM), lambda i, j: (i, 0, 0))],
        out_shape=[jax.ShapeDtypeStruct((b, l, D_INNER), F32),
                   jax.ShapeDtypeStruct((b, l, CONV_DIM), F32),
                   jax.ShapeDtypeStruct((b, l, dt_w), F32),
                   jax.ShapeDtypeStruct((b, CONV_W - 1, CONV_DIM), F32)],
        scratch_shapes=[pltpu.VMEM((nb, CONV_HALO, CONV_DIM), F32),
                        pltpu.VMEM((nb * tl, D_MODEL), BF16)],
        compiler_params=_params(2),
        name="ssd_in",
    )(x2d, g.reshape(1, D_MODEL), w_in, wdt, cw, cb.reshape(1, CONV_DIM),
      dtb.reshape(1, dt_w), conv0)


SSD_PIPE_TL = 512


def _ssd_in_prompt_kernel(x_ref, g_ref, win_ref, wdt_ref, cw_ref, cb_ref, dtb_ref, conv0_ref,
                          apad_ref, z_ref, acum_ref, tr_ref, xbc_ref, convnew_ref, halo_ref, h_ref):
    tl = SSD_PIPE_TL
    wc = SSD_IN_LANE_CHUNK
    nblk = tl // SUBLANES

    @pl.when(pl.program_id(1) == 0)
    def _():
        halo_ref[...] = jnp.zeros_like(halo_ref)
        halo_ref[CONV_HALO - (CONV_W - 1):, :] = conv0_ref[0]

    h_ref[...] = _rmsnorm(x_ref[...], g_ref[...]).astype(BF16)
    dt = _softplus(_dot(h_ref[...], wdt_ref[...]) + dtb_ref[...])
    for j in range(tl // CHUNK):
        cs = slice(j * CHUNK, (j + 1) * CHUNK)
        acum, acum_t, src_t, w_t = _chunk_decays(dt[cs, :], apad_ref[...])
        acum_ref[0, cs, :] = acum
        tr_ref[0, j, 0] = acum_t
        tr_ref[0, j, 1] = src_t
        tr_ref[0, j, 2] = w_t
    for c in range(0, CONV_DIM, wc):
        sl = slice(c, c + wc)
        zs = slice(c // 2, c // 2 + wc // 2)
        z_ref[0, :, zs] = _dot(h_ref[...], win_ref[:, zs]).astype(z_ref.dtype)
        cur = _dot(h_ref[...], win_ref[:, D_INNER + c:D_INNER + c + wc])
        ext = jnp.concatenate([halo_ref[:, sl], cur], axis=0).reshape(1, nblk + 1, SUBLANES, wc)
        xbc_ref[0, :, sl] = _conv_silu(ext, cw_ref, cb_ref, sl).reshape(tl, wc)
        halo_ref[:, sl] = cur[tl - CONV_HALO:, :]
    convnew_ref[0] = halo_ref[CONV_HALO - (CONV_W - 1):, :]


def _ssd_in_prompt(x2d, b, l, g, w_in, wdt, cw, cb, dtb, conv0, a_pad):
    tl = SSD_PIPE_TL
    tps = l // tl
    ncs = tl // CHUNK
    blk = lambda w: pl.BlockSpec((1, tl, w), lambda i, j: (i, j, 0))
    return pl.pallas_call(
        _ssd_in_prompt_kernel,
        grid=(b, l // tl),
        in_specs=[pl.BlockSpec((tl, D_MODEL), lambda i, j: (i * tps + j, 0)),
                  _resident((1, D_MODEL)), _resident(w_in.shape), _resident((D_MODEL, DT_PAD)),
                  _resident((CONV_W, CONV_DIM)), _resident((1, CONV_DIM)), _resident((1, DT_PAD)),
                  pl.BlockSpec((1, CONV_W - 1, CONV_DIM), lambda i, j: (i, 0, 0)),
                  _resident((1, DT_PAD))],
        out_specs=[blk(D_INNER), blk(DT_PAD),
                   pl.BlockSpec((1, ncs, 3, DT_PAD, CHUNK), lambda i, j: (i, j, 0, 0, 0)),
                   blk(CONV_DIM),
                   pl.BlockSpec((1, CONV_W - 1, CONV_DIM), lambda i, j: (i, 0, 0))],
        out_shape=[jax.ShapeDtypeStruct((b, l, D_INNER), BF16),
                   jax.ShapeDtypeStruct((b, l, DT_PAD), F32),
                   jax.ShapeDtypeStruct((b, l // CHUNK, 3, DT_PAD, CHUNK), F32),
                   jax.ShapeDtypeStruct((b, l, CONV_DIM), F32),
                   jax.ShapeDtypeStruct((b, CONV_W - 1, CONV_DIM), F32)],
        scratch_shapes=[pltpu.VMEM((CONV_HALO, CONV_DIM), F32), pltpu.VMEM((tl, D_MODEL), BF16)],
        compiler_params=_params(2),
        name="ssd_in_prompt",
    )(x2d, g.reshape(1, D_MODEL), w_in, wdt, cw, cb.reshape(1, CONV_DIM),
      dtb.reshape(1, DT_PAD), conv0, a_pad)


B_OFF = D_INNER
C_OFF = D_INNER + N_GROUPS * D_STATE


def _gate_norm(y, z, ng):
    yg = y * _silu(z)
    ms = jnp.mean(yg * yg, axis=-1, keepdims=True)
    return yg * lax.rsqrt(ms + EPS) * ng


def _head_rows(mat, g):
    return jnp.concatenate(
        [jnp.broadcast_to(mat[g * HEADS_PER_GROUP + r:g * HEADS_PER_GROUP + r + 1, :],
                          (HEAD_DIM, mat.shape[1])) for r in range(HEADS_PER_GROUP)], axis=0)


def _scan_prompt_kernel(xbc_ref, z_ref, acum_ref, tr_ref, de_ref, ng_ref, y_ref, hout_ref, *h_refs,
                        chunks):
    q = CHUNK
    step = pl.program_id(1)

    @pl.when(step == 0)
    def _():
        for h_ref in h_refs:
            h_ref[...] = jnp.zeros_like(h_ref)

    tril = (lax.broadcasted_iota(jnp.int32, (q, q), 0) >= lax.broadcasted_iota(jnp.int32, (q, q), 1))
    lane_g = lax.broadcasted_iota(jnp.int32, (q, GROUP_DIM), 1)
    lane_half = lax.broadcasted_iota(jnp.int32, (q, LANES), 1) < HEAD_DIM

    def chunk_body(ci, carry):
        ts = pl.ds(pl.multiple_of(ci * q, q), q)
        acum = acum_ref[0, ts, :]
        acum_t, src_t, w_t = tr_ref[0, ci, 0], tr_ref[0, ci, 1], tr_ref[0, ci, 2]
        cd_b = jnp.broadcast_to(jnp.exp2(acum_t[:, q - 1:q]), (DT_PAD, D_STATE))
        for g in range(N_GROUPS):
            gs = slice(g * GROUP_DIM, (g + 1) * GROUP_DIM)
            bg = xbc_ref[0, ts, B_OFF + g * D_STATE:B_OFF + (g + 1) * D_STATE].astype(BF16)
            cg = xbc_ref[0, ts, C_OFF + g * D_STATE:C_OFF + (g + 1) * D_STATE].astype(BF16)
            xg = xbc_ref[0, ts, gs]
            xgb = xg.astype(BF16)
            cb = _dot_nt(cg, bg)
            ms, cols = [], []
            for r in range(HEADS_PER_GROUP):
                hd = g * HEADS_PER_GROUP + r
                colf = jnp.broadcast_to(acum[:, hd:hd + 1], (q, q))
                rowf = jnp.broadcast_to(src_t[hd:hd + 1, :], (q, q))
                ms.append((cb * jnp.exp2(jnp.where(tril, colf - rowf, -jnp.inf))).astype(BF16))
                cols.append(colf)
            mcat = jnp.concatenate(ms, axis=1)
            zero = jnp.zeros_like(xgb)
            bd = jnp.concatenate(
                [jnp.where((lane_g >= r * HEAD_DIM) & (lane_g < (r + 1) * HEAD_DIM), xgb, zero)
                 for r in range(HEADS_PER_GROUP)], axis=0)
            y_diag = _dot(mcat, bd)
            hg = h_refs[g][...]
            y_off = _dot_nt(cg, hg.astype(BF16))
            acum_e = jnp.concatenate([jnp.where(lane_half, cols[0], cols[1]),
                                      jnp.where(lane_half, cols[2], cols[3])], axis=1)
            y = y_diag + y_off * jnp.exp2(acum_e) + de_ref[:, gs] * xg
            xg_t = xg.T
            wt = jnp.concatenate(
                [xg_t[r * HEAD_DIM:(r + 1) * HEAD_DIM, :] * w_t[g * HEADS_PER_GROUP + r:
                                                                g * HEADS_PER_GROUP + r + 1, :]
                 for r in range(HEADS_PER_GROUP)], axis=0).astype(BF16)
            h_refs[g][...] = hg * _head_rows(cd_b, g) + _dot(wt, bg)
            zg = z_ref[0, ts, gs].astype(F32)
            y_ref[0, ts, gs] = _gate_norm(y, zg, ng_ref[:, gs]).astype(y_ref.dtype)
        return carry

    lax.fori_loop(0, chunks, chunk_body, 0)

    @pl.when(step == pl.num_programs(1) - 1)
    def _():
        for g, h_ref in enumerate(h_refs):
            hout_ref[0, g * GROUP_DIM:(g + 1) * GROUP_DIM, :] = h_ref[...]


SCAN_CHUNKS_PER_STEP = 4


def _scan_prompt(xbc, z, acum, tr, d_e, ng):
    b, l, _ = xbc.shape
    chunks = SCAN_CHUNKS_PER_STEP
    tl = chunks * CHUNK
    blk = lambda w: pl.BlockSpec((1, tl, w), lambda i, j: (i, j, 0))
    return pl.pallas_call(
        functools.partial(_scan_prompt_kernel, chunks=chunks),
        grid=(b, l // tl),
        in_specs=[blk(CONV_DIM), blk(D_INNER), blk(DT_PAD),
                  pl.BlockSpec((1, chunks, 3, DT_PAD, CHUNK), lambda i, j: (i, j, 0, 0, 0)),
                  _resident((1, D_INNER)), _resident((1, D_INNER))],
        out_specs=[blk(D_INNER), pl.BlockSpec((1, D_INNER, D_STATE), lambda i, j: (i, 0, 0))],
        out_shape=[jax.ShapeDtypeStruct((b, l, D_INNER), BF16),
                   jax.ShapeDtypeStruct((b, D_INNER, D_STATE), F32)],
        scratch_shapes=[pltpu.VMEM((GROUP_DIM, D_STATE), F32)] * N_GROUPS,
        compiler_params=_params(2),
        name="ssd_scan_prompt",
    )(xbc, z, acum, tr, d_e, ng)


SAMPLE_LEN = 8
SCAN_NB = 16
SCAN_ROWS = SCAN_NB * SAMPLE_LEN
SCAN_STEP_NB = 8


def _scan_sample_kernel(xbc_ref, z_ref, dt_ref, h0_ref, apad_ref, ae_ref, de_ref, ng_ref,
                        gsum_ref, gexp_ref, y_ref, hout_ref,
                        wt_ref, ea_ref, yd_ref, acp_ref, p_ref):
    rows = SCAN_ROWS
    j = pl.program_id(1)

    @pl.when(j == 0)
    def _():
        shape3 = lambda w: (SCAN_NB, SAMPLE_LEN, w)
        tok = lax.broadcasted_iota(jnp.int32, (rows, D_INNER), 0) & (SAMPLE_LEN - 1)
        tok_p = lax.broadcasted_iota(jnp.int32, (rows, DT_PAD), 0) & (SAMPLE_LEN - 1)

        def cumsum_tokens(v, t):
            for sh in (1, 2, 4):
                v = v + jnp.where(t >= sh, pltpu.roll(v, sh, 0), 0.0)
            return v

        def bcast_token(v, s):
            w = v.shape[-1]
            v3 = v.reshape(shape3(w))
            return jnp.broadcast_to(v3[:, s:s + 1, :], shape3(w)).reshape(rows, w)

        xs = xbc_ref[:, :, 0:D_INNER].reshape(rows, D_INNER)
        bm = xbc_ref[:, :, B_OFF:C_OFF].reshape(rows, N_GROUPS * D_STATE)
        cm = xbc_ref[:, :, C_OFF:CONV_DIM].reshape(rows, N_GROUPS * D_STATE)
        dt_p = dt_ref[:, :, 0:DT_PAD].reshape(rows, DT_PAD)
        dt_e = dt_ref[:, :, DT_PAD:DT_PAD + D_INNER].reshape(rows, D_INNER)
        acp_ref[...] = cumsum_tokens(dt_p * apad_ref[...], tok_p)
        acum_e = cumsum_tokens(dt_e * ae_ref[...], tok)
        xdt = xs * dt_e
        w = xdt * jnp.exp(bcast_token(acum_e, SAMPLE_LEN - 1) - acum_e)
        wt_ref[...] = w.T.astype(BF16)
        ea_ref[...] = jnp.exp(acum_e)
        for s in range(SAMPLE_LEN):
            p_ref[s * rows:(s + 1) * rows, :] = (cm * bcast_token(bm, s)).astype(BF16)
        cb_sum = _dot(p_ref[...], gsum_ref[...])
        yd = de_ref[...] * xs
        for s in range(SAMPLE_LEN):
            cb_e = _dot(cb_sum[s * rows:(s + 1) * rows, :].astype(BF16), gexp_ref[...])
            diff = acum_e - bcast_token(acum_e, s)
            decay = jnp.exp(jnp.where(tok >= s, diff, -jnp.inf))
            yd = yd + cb_e * decay * bcast_token(xdt, s)
        yd_ref[...] = yd

    eye = (lax.broadcasted_iota(jnp.int32, (N_HEADS, DT_PAD), 0)
           == lax.broadcasted_iota(jnp.int32, (N_HEADS, DT_PAD), 1))
    rowid = lax.broadcasted_iota(jnp.int32, (rows, D_STATE), 0)
    for t in range(SCAN_STEP_NB):
        seq = j * SCAN_STEP_NB + t
        r0 = pl.multiple_of(seq * SAMPLE_LEN, SAMPLE_LEN)
        alast = acp_ref[pl.ds(r0 + SAMPLE_LEN - 1, 1), :]
        alast_col = jnp.sum(jnp.where(eye, jnp.broadcast_to(alast, (N_HEADS, DT_PAD)), 0.0),
                            axis=1, keepdims=True)
        cd_b = jnp.broadcast_to(jnp.exp(alast_col), (N_HEADS, D_STATE))
        mine = (rowid >= r0) & (rowid < r0 + SAMPLE_LEN)
        y_offs = []
        for g in range(N_GROUPS):
            gs = slice(g * GROUP_DIM, (g + 1) * GROUP_DIM)
            hg = h0_ref[t, gs, :]
            cg = xbc_ref[seq, :, C_OFF + g * D_STATE:C_OFF + (g + 1) * D_STATE].astype(BF16)
            y_offs.append(_dot_nt(cg, hg.astype(BF16)))
            b_all = xbc_ref[:, :, B_OFF + g * D_STATE:B_OFF + (g + 1) * D_STATE].reshape(
                rows, D_STATE)
            b_mine = jnp.where(mine, b_all, 0.0).astype(BF16)
            hout_ref[t, gs, :] = hg * _head_rows(cd_b, g) + _dot(wt_ref[gs, :], b_mine)
        y = yd_ref[pl.ds(r0, SAMPLE_LEN), :] + jnp.concatenate(y_offs, axis=1) * ea_ref[
            pl.ds(r0, SAMPLE_LEN), :]
        z = z_ref[seq]
        y_ref[seq] = jnp.concatenate(
            [_gate_norm(y[:, g * GROUP_DIM:(g + 1) * GROUP_DIM],
                        z[:, g * GROUP_DIM:(g + 1) * GROUP_DIM],
                        ng_ref[:, g * GROUP_DIM:(g + 1) * GROUP_DIM]) for g in range(N_GROUPS)],
            axis=1)


def _scan_sample(xbc, z, dt, h0, a_pad, a_e, d_e, ng, gsum, gexp):
    b = xbc.shape[0]
    dt_w = dt.shape[-1]
    blk = lambda w: pl.BlockSpec((SCAN_NB, SAMPLE_LEN, w), lambda i, j: (i, 0, 0))
    steps = SCAN_NB // SCAN_STEP_NB
    st = pl.BlockSpec((SCAN_STEP_NB, D_INNER, D_STATE), lambda i, j: (i * steps + j, 0, 0))
    return pl.pallas_call(
        _scan_sample_kernel,
        grid=(b // SCAN_NB, steps),
        in_specs=[blk(CONV_DIM), blk(D_INNER), blk(dt_w), st, _resident((1, DT_PAD)),
                  _resident((1, D_INNER)), _resident((1, D_INNER)), _resident((1, D_INNER)),
                  _resident(gsum.shape), _resident(gexp.shape)],
        out_specs=[blk(D_INNER), st],
        out_shape=[jax.ShapeDtypeStruct((b, SAMPLE_LEN, D_INNER), F32),
                   jax.ShapeDtypeStruct((b, D_INNER, D_STATE), F32)],
        scratch_shapes=[pltpu.VMEM((D_INNER, SCAN_ROWS), BF16),
                        pltpu.VMEM((SCAN_ROWS, D_INNER), F32),
                        pltpu.VMEM((SCAN_ROWS, D_INNER), F32),
                        pltpu.VMEM((SCAN_ROWS, DT_PAD), F32),
                        pltpu.VMEM((SAMPLE_LEN * SCAN_ROWS, N_GROUPS * D_STATE), BF16)],
        compiler_params=_params(2),
        name="ssd_scan_sample",
    )(xbc, z, dt, h0, a_pad, a_e, d_e, ng, gsum, gexp)


POOL_HALO = 2 * SUBLANES


def _pool_kernel(x_ref, g_ref, win_ref, wgrp_ref, scale_ref, wout_ref, buf0_ref,
                 o_ref, bufnew_ref, ext_ref, *, nb, tl, pos0):
    rows = nb * tl
    jt = pl.program_id(1)

    @pl.when(jt == 0)
    def _():
        ext_ref[:, 0:POOL_HALO, :] = jnp.zeros((nb, POOL_HALO, D_MODEL), F32)
        ext_ref[:, POOL_HALO - (MAX_WIN - 1):POOL_HALO, :] = buf0_ref[...]

    x = x_ref[...]
    h = _rmsnorm(x, g_ref[...]).astype(BF16)
    ext_ref[:, POOL_HALO:POOL_HALO + tl, :] = _dot(h, win_ref[...]).reshape(nb, tl, D_MODEL)
    nblk = tl // SUBLANES
    halo_blocks = POOL_HALO // SUBLANES
    shape4 = (nb, nblk, SUBLANES, POOL_GROUP_DIM)
    pos = (pos0 + jt * tl + SUBLANES * lax.broadcasted_iota(jnp.int32, shape4, 1)
           + lax.broadcasted_iota(jnp.int32, shape4, 2)).astype(F32)
    mixed = []
    for k, w in enumerate(POOL_WINDOWS):
        sl = slice(k * POOL_GROUP_DIM, (k + 1) * POOL_GROUP_DIM)
        ext = ext_ref[:, :, sl].reshape(nb, nblk + halo_blocks, SUBLANES, POOL_GROUP_DIM)
        tot, shift = ext, 1
        while shift < w:
            tot = tot + _shift_rows(tot, shift)
            shift *= 2
        u = ext[:, halo_blocks:]
        mean = tot[:, halo_blocks:] / jnp.minimum(jnp.float32(w), pos + 1.0)
        m = (mean - u).reshape(rows, POOL_GROUP_DIM).astype(BF16)
        mixed.append(_dot(m, wgrp_ref[k]))
    mixed = (jnp.concatenate(mixed, axis=1) * scale_ref[...]).astype(BF16)
    o_ref[...] = x + _dot(mixed, wout_ref[...])
    bufnew_ref[...] = ext_ref[:, tl + 1:tl + POOL_HALO, :]
    ext_ref[:, 0:POOL_HALO, :] = ext_ref[:, tl:tl + POOL_HALO, :]


def _pool(x2d, row0, b, l, g, w_in, w_grp, scale, w_out, buf0, *, nb, tl, pos0):
    assert row0 % (nb * tl) == 0 and (nb == 1 or l == tl)
    tile0, tps = row0 // (nb * tl), l // tl
    return pl.pallas_call(
        functools.partial(_pool_kernel, nb=nb, tl=tl, pos0=pos0),
        grid=(b // nb, tps),
        in_specs=[pl.BlockSpec((nb * tl, D_MODEL), lambda i, j: (tile0 + i * tps + j, 0)),
                  _resident((1, D_MODEL)), _resident((D_MODEL, D_MODEL)),
                  _resident(w_grp.shape), _resident((1, D_MODEL)), _resident((D_MODEL, D_MODEL)),
                  pl.BlockSpec((nb, MAX_WIN - 1, D_MODEL), lambda i, j: (i, 0, 0))],
        out_specs=[pl.BlockSpec((nb * tl, D_MODEL), lambda i, j: (i * tps + j, 0)),
                   pl.BlockSpec((nb, MAX_WIN - 1, D_MODEL), lambda i, j: (i, 0, 0))],
        out_shape=[jax.ShapeDtypeStruct((b * l, D_MODEL), F32),
                   jax.ShapeDtypeStruct((b, MAX_WIN - 1, D_MODEL), F32)],
        scratch_shapes=[pltpu.VMEM((nb, POOL_HALO + tl, D_MODEL), F32)],
        compiler_params=_params(2),
        name="pool_mixer",
    )(x2d, g.reshape(1, D_MODEL), w_in, w_grp, scale.reshape(1, D_MODEL), w_out, buf0)


def _expand_heads(v):
    return jnp.repeat(v.astype(F32), HEAD_DIM).reshape(1, D_INNER)


def _pad_heads(v):
    return jnp.pad(v.astype(F32), (0, DT_PAD - N_HEADS)).reshape(1, DT_PAD)


def _group_sum_matrix():
    m = np.zeros((N_GROUPS * D_STATE, LANES), np.float32)
    for g in range(N_GROUPS):
        m[g * D_STATE:(g + 1) * D_STATE, g] = 1.0
    return jnp.asarray(m, BF16)


def _group_expand_matrix():
    m = np.zeros((LANES, D_INNER), np.float32)
    for g in range(N_GROUPS):
        m[g, g * GROUP_DIM:(g + 1) * GROUP_DIM] = 1.0
    return jnp.asarray(m, BF16)


def _trunks(x_prompt, x_sample, ssm0_s, conv0_s, pool0_s, p):
    bp, lp, _ = x_prompt.shape
    bs, ls, _ = x_sample.shape
    rows = (bp * lp, bs * ls)
    bf = lambda a: a.astype(BF16)
    w_gate, w_up, w_down = bf(p["ffn_w_gate"]), bf(p["ffn_w_up"]), bf(p["ffn_w_down"])
    ffn = lambda xs, i, k, **kw: _ffn(xs, rows, p["ffn_norm"][i, k], w_gate, w_up, w_down, (i, k), **kw)
    x2d = ffn([x_prompt.reshape(rows[0], D_MODEL), x_sample.reshape(rows[1], D_MODEL)], 0, 0)

    w_in = p["ssd_w_in"][0]
    w_dt = w_in[:, D_INNER + CONV_DIM:]
    dt_bias = p["ssd_dt_bias"][0]
    wdt = jnp.pad(w_dt, ((0, 0), (0, DT_PAD - N_HEADS)))
    dtb = jnp.pad(dt_bias, (0, DT_PAD - N_HEADS))
    wdt_s = jnp.concatenate([wdt, jnp.repeat(w_dt, HEAD_DIM, axis=1)], axis=1)
    dtb_s = jnp.concatenate([dtb, jnp.repeat(dt_bias, HEAD_DIM)])
    a_neg = -jnp.exp(p["ssd_a_log"][0].astype(F32))
    a_pad, d_e = _pad_heads(a_neg), _expand_heads(p["ssd_d"][0])
    ng = p["ssd_norm"][0].reshape(1, D_INNER)
    common = (p["mix_norm"][0], bf(w_in))
    conv_w, conv_b = p["ssd_conv_w"][0], p["ssd_conv_b"][0]
    z, acum, tr, xbc, conv_p = _ssd_in_prompt(x2d, bp, lp, *common, bf(wdt), conv_w, conv_b, dtb,
                                              jnp.zeros((bp, CONV_W - 1, CONV_DIM), F32), a_pad)
    y_p, ssm_p = _scan_prompt(xbc, z, acum, tr, d_e, ng)
    z, xbc, dt, conv_s = _ssd_in(x2d, rows[0], bs, ls, *common, bf(wdt_s), conv_w, conv_b, dtb_s,
                                 conv0_s, nb=32, tl=SAMPLE_LEN)
    y_s, ssm_s = _scan_sample(xbc, z, dt, ssm0_s.reshape(bs, D_INNER, D_STATE), a_pad,
                              _expand_heads(a_neg), d_e, ng, _group_sum_matrix(),
                              _group_expand_matrix())
    x2d = ffn([x2d], 0, 1, pre=([y_p.reshape(rows[0], D_INNER), y_s.reshape(rows[1], D_INNER)],
                                bf(p["ssd_w_out"][0])))

    x2d = ffn([x2d], 1, 0)
    pool_w = (p["mix_norm"][1], bf(p["pool_w_in"][0]), bf(p["pool_w_group"][0]), p["pool_scale"][0],
              bf(p["pool_w_out"][0]))
    xm_p, pool_p = _pool(x2d, 0, bp, lp, *pool_w, jnp.zeros((bp, MAX_WIN - 1, D_MODEL), F32),
                         nb=1, tl=512, pos0=0)
    xm_s, pool_s = _pool(x2d, rows[0], bs, ls, *pool_w, pool0_s, nb=64, tl=SAMPLE_LEN, pos0=PAST_LEN)
    out_p, out_s = ffn([xm_p, xm_s], 1, 1, final_g=p["final_norm"], split_out=True)
    state = lambda a, b: a.reshape(b, N_HEADS, HEAD_DIM, D_STATE)[None]
    return (out_p.reshape(bp, lp, D_MODEL), out_s.reshape(bs, ls, D_MODEL),
            state(ssm_p, bp), conv_p[None], pool_p[None], state(ssm_s, bs), conv_s[None], pool_s[None])


def kernel(x_prompt, x_sample, state_ssm, state_conv, state_pool, ffn_norm, ffn_w_gate, ffn_w_up,
           ffn_w_down, mix_norm, ssd_w_in, ssd_conv_w, ssd_conv_b, ssd_dt_bias, ssd_a_log, ssd_d,
           ssd_norm, ssd_w_out, pool_w_in, pool_w_group, pool_scale, pool_w_out, final_norm):
    p = dict(ffn_norm=ffn_norm, ffn_w_gate=ffn_w_gate, ffn_w_up=ffn_w_up, ffn_w_down=ffn_w_down,
             mix_norm=mix_norm, ssd_w_in=ssd_w_in, ssd_conv_w=ssd_conv_w, ssd_conv_b=ssd_conv_b,
             ssd_dt_bias=ssd_dt_bias, ssd_a_log=ssd_a_log, ssd_d=ssd_d, ssd_norm=ssd_norm,
             ssd_w_out=ssd_w_out, pool_w_in=pool_w_in, pool_w_group=pool_w_group,
             pool_scale=pool_scale, pool_w_out=pool_w_out, final_norm=final_norm)
    return _trunks(x_prompt, x_sample, state_ssm[0], state_conv[0], state_pool[0], p)
```

```python
import functools

import numpy as np
import jax
import jax.numpy as jnp
from jax import lax
from jax.experimental import pallas as pl
from jax.experimental.pallas import tpu as pltpu

F32 = jnp.float32
BF16 = jnp.bfloat16

EPS = 1e-6
D_MODEL = 1024
D_FF = 2816
D_INNER = 2048
HEAD_DIM = 64
N_HEADS = 32
N_GROUPS = 8
HEADS_PER_GROUP = 4
GROUP_DIM = HEADS_PER_GROUP * HEAD_DIM
D_STATE = 128
CONV_W = 4
CONV_DIM = D_INNER + 2 * N_GROUPS * D_STATE
CHUNK = 128
POOL_WINDOWS = (2, 4, 8, 16)
POOL_GROUP_DIM = 256
MAX_WIN = 16
PAST_LEN = 16384
LANES = 128
SUBLANES = 8
DT_PAD = LANES
VMEM_LIMIT = 60 * 1024 * 1024

NT_DIMS = (((1,), (1,)), ((), ()))
LOG2E = 1.4426950408889634


def _resident(shape, lead=()):
    nd = len(shape)
    return pl.BlockSpec((None,) * len(lead) + tuple(shape), lambda *_: tuple(lead) + (0,) * nd,
                        pipeline_mode=pl.Buffered(1))


def _params(n_axes):
    return pltpu.CompilerParams(dimension_semantics=("arbitrary",) * n_axes,
                                vmem_limit_bytes=VMEM_LIMIT)


def _rmsnorm(x, g):
    ms = jnp.mean(x * x, axis=-1, keepdims=True)
    return x * lax.rsqrt(ms + EPS) * g


def _silu(x):
    return x * jax.nn.sigmoid(x)


def _softplus(x):
    return jnp.maximum(x, 0.0) + jnp.log(1.0 + jnp.exp(-jnp.abs(x)))


def _dot(a, b):
    return jnp.dot(a, b, preferred_element_type=F32)


def _dot_nt(a, b):
    return lax.dot_general(a, b, NT_DIMS, preferred_element_type=F32)


def _dot_exact01(t, x):
    hi = x.astype(BF16)
    r1 = x - hi.astype(F32)
    mid = r1.astype(BF16)
    lo = (r1 - mid.astype(F32)).astype(BF16)
    return _dot(t, hi) + _dot(t, mid) + _dot(t, lo)


FFN_TM = 512
FFN_TF = 256


def _ffn_kernel(*refs, n_x, n_y, final, n_out, tiles_a):
    refs = list(refs)
    x_refs = [refs.pop(0) for _ in range(n_x)]
    y_refs = [refs.pop(0) for _ in range(n_y)]
    wpre_ref = refs.pop(0) if n_y else None
    g_ref, wg_ref, wu_ref, wd_ref = refs[:4]
    del refs[:4]
    fg_ref = refs.pop(0) if final else None
    o_refs = [refs.pop(0) for _ in range(n_out)]
    h_ref, a_ref = refs
    second = pl.program_id(0) >= tiles_a

    def pick(rs, dtype):
        if len(rs) == 1:
            return rs[0][...].astype(dtype)
        return jnp.where(second, rs[1][...].astype(dtype), rs[0][...].astype(dtype))

    x = pick(x_refs, F32)
    if n_y:
        x = x + _dot(pick(y_refs, BF16), wpre_ref[...])
    h_ref[...] = _rmsnorm(x, g_ref[...]).astype(BF16)
    for f in range(0, D_FF, FFN_TF):
        h = h_ref[...]
        gate = _dot(h, wg_ref[:, f:f + FFN_TF].astype(BF16))
        up = _dot(h, wu_ref[:, f:f + FFN_TF].astype(BF16))
        a_ref[:, f:f + FFN_TF] = (_silu(gate) * up).astype(BF16)
    y = x + 0.5 * _dot(a_ref[...], wd_ref[...].astype(BF16))
    if final:
        y = _rmsnorm(y, fg_ref[...])
    if n_out == 1:
        o_refs[0][...] = y
    else:
        @pl.when(jnp.logical_not(second))
        def _():
            o_refs[0][...] = y

        @pl.when(second)
        def _():
            o_refs[1][...] = y


def _ffn(xs, rows, g, w_gate, w_up, w_down, idx, final_g=None, pre=None, split_out=False):
    tm = FFN_TM
    tiles_a, tiles_b = rows[0] // tm, rows[1] // tm
    assert rows[0] % tm == 0 and rows[1] % tm == 0
    final = final_g is not None

    def row_specs(arrs):
        if len(arrs) == 1:
            return [pl.BlockSpec((tm, arrs[0].shape[1]), lambda i: (i, 0))]
        return [pl.BlockSpec((tm, arrs[0].shape[1]), lambda i: (jnp.minimum(i, tiles_a - 1), 0)),
                pl.BlockSpec((tm, arrs[1].shape[1]), lambda i: (jnp.maximum(i - tiles_a, 0), 0))]

    in_specs, args = row_specs(xs), list(xs)
    ys = []
    if pre is not None:
        ys, wpre = pre
        in_specs += row_specs(ys) + [_resident(wpre.shape)]
        args += list(ys) + [wpre]
    in_specs += [_resident((1, D_MODEL)), _resident((D_MODEL, D_FF), idx),
                 _resident((D_MODEL, D_FF), idx), _resident((D_FF, D_MODEL), idx)]
    args += [g.reshape(1, D_MODEL), w_gate, w_up, w_down]
    if final:
        in_specs.append(_resident((1, D_MODEL)))
        args.append(final_g.reshape(1, D_MODEL))
    if split_out:
        outs = [jax.ShapeDtypeStruct((r, D_MODEL), F32) for r in rows]
    else:
        outs = [jax.ShapeDtypeStruct((rows[0] + rows[1], D_MODEL), F32)]
    res = pl.pallas_call(
        functools.partial(_ffn_kernel, n_x=len(xs), n_y=len(ys), final=final, n_out=len(outs),
                          tiles_a=tiles_a),
        grid=(tiles_a + tiles_b,),
        in_specs=in_specs,
        out_specs=row_specs(outs),
        out_shape=outs,
        scratch_shapes=[pltpu.VMEM((tm, D_MODEL), BF16), pltpu.VMEM((tm, D_FF), BF16)],
        compiler_params=_params(1),
        name="ffn" + ("_pre" if pre is not None else "") + ("_final" if final else ""),
    )(*args)
    return res if split_out else res[0]


CONV_HALO = SUBLANES
SSD_IN_LANE_CHUNK = 512


def _shift_rows(v, k):
    if k == SUBLANES:
        return jnp.concatenate([v[:, :1], v[:, :-1]], axis=1)
    r = pltpu.roll(v, k, 2)
    prev = jnp.concatenate([r[:, :1], r[:, :-1]], axis=1)
    sub = lax.broadcasted_iota(jnp.int32, v.shape, 2)
    return jnp.where(sub < k, prev, r)


def _chunk_decays(dt, a_pad):
    q = dt.shape[0]
    tril = (lax.broadcasted_iota(jnp.int32, (q, q), 0) >= lax.broadcasted_iota(jnp.int32, (q, q), 1))
    acum = _dot_exact01(tril.astype(BF16), dt * a_pad) * LOG2E
    acum_t = acum.T
    dt_t = dt.T
    src_t = acum_t - jnp.log2(dt_t)
    w_t = jnp.exp2(acum_t[:, q - 1:q] - acum_t) * dt_t
    return acum, acum_t, src_t, w_t


def _conv_silu(ext, cw_ref, cb_ref, sl):
    w = ext.shape[-1]
    tap = lambda k: cw_ref[k:k + 1, sl].reshape(1, 1, 1, w)
    ext1 = _shift_rows(ext, 1)
    p = ext * tap(3) + ext1 * tap(2)
    q = ext * tap(1) + ext1 * tap(0)
    return _silu((cb_ref[:, sl].reshape(1, 1, 1, w) + p + _shift_rows(q, 2))[:, 1:])


def _ssd_in_kernel(x_ref, g_ref, win_ref, wdt_ref, cw_ref, cb_ref, dtb_ref, conv0_ref,
                   z_ref, xbc_ref, dt_ref, convnew_ref, halo_ref, h_ref, *, nb, tl):
    rows = nb * tl
    wc = SSD_IN_LANE_CHUNK
    nblk = tl // SUBLANES

    @pl.when(pl.program_id(1) == 0)
    def _():
        halo_ref[...] = jnp.zeros_like(halo_ref)
        halo_ref[:, CONV_HALO - (CONV_W - 1):, :] = conv0_ref[...]

    h_ref[...] = _rmsnorm(x_ref[...], g_ref[...]).astype(BF16)
    dt = _softplus(_dot(h_ref[...], wdt_ref[...]) + dtb_ref[...])
    dt_ref[...] = dt.reshape(nb, tl, dt.shape[-1])
    for c in range(0, CONV_DIM, wc):
        sl = slice(c, c + wc)
        zs = slice(c // 2, c // 2 + wc // 2)
        z_ref[:, :, zs] = _dot(h_ref[...], win_ref[:, zs]).reshape(nb, tl, wc // 2)
        cur = _dot(h_ref[...], win_ref[:, D_INNER + c:D_INNER + c + wc]).reshape(nb, tl, wc)
        ext = jnp.concatenate([halo_ref[:, :, sl], cur], axis=1).reshape(nb, nblk + 1, SUBLANES, wc)
        xbc_ref[:, :, sl] = _conv_silu(ext, cw_ref, cb_ref, sl).reshape(nb, tl, wc)
        halo_ref[:, :, sl] = cur[:, tl - CONV_HALO:, :]
    convnew_ref[...] = halo_ref[:, CONV_HALO - (CONV_W - 1):, :]


def _ssd_in(x2d, row0, b, l, g, w_in, wdt, cw, cb, dtb, conv0, *, nb, tl):
    assert l == tl and row0 % (nb * tl) == 0
    tile0 = row0 // (nb * tl)
    dt_w = wdt.shape[1]
    blk = lambda w: pl.BlockSpec((nb, tl, w), lambda i, j: (i, j, 0))
    return pl.pallas_call(
        functools.partial(_ssd_in_kernel, nb=nb, tl=tl),
        grid=(b // nb, l // tl),
        in_specs=[pl.BlockSpec((nb * tl, D_MODEL), lambda i, j: (tile0 + i, 0)),
                  _resident((1, D_MODEL)), _resident(w_in.shape), _resident((D_MODEL, dt_w)),
                  _resident((CONV_W, CONV_DIM)), _resident((1, CONV_DIM)), _resident((1, dt_w)),
                  pl.BlockSpec((nb, CONV_W - 1, CONV_DIM), lambda i, j: (i, 0, 0))],
        out_specs=[blk(D_INNER), blk(CONV_DIM), blk(dt_w),
                   pl.BlockSpec((nb, CONV_W - 1, CONV_DIM), lambda i, j: (i, 0, 0))],
        out_shape=[jax.ShapeDtypeStruct((b, l, D_INNER), F32),
                   jax.ShapeDtypeStruct((b, l, CONV_DIM), F32),
                   jax.ShapeDtypeStruct((b, l, dt_w), F32),
                   jax.ShapeDtypeStruct((b, CONV_W - 1, CONV_DIM), F32)],
        scratch_shapes=[pltpu.VMEM((nb, CONV_HALO, CONV_DIM), F32),
                        pltpu.VMEM((nb * tl, D_MODEL), BF16)],
        compiler_params=_params(2),
        name="ssd_in",
    )(x2d, g.reshape(1, D_MODEL), w_in, wdt, cw, cb.reshape(1, CONV_DIM),
      dtb.reshape(1, dt_w), conv0)


SSD_PIPE_TL = 512


def _ssd_in_prompt_kernel(x_ref, g_ref, win_ref, wdt_ref, cw_ref, cb_ref, dtb_ref, conv0_ref,
                          apad_ref, z_ref, acum_ref, tr_ref, xbc_ref, convnew_ref, halo_ref, h_ref):
    tl = SSD_PIPE_TL
    wc = SSD_IN_LANE_CHUNK
    nblk = tl // SUBLANES

    @pl.when(pl.program_id(1) == 0)
    def _():
        halo_ref[...] = jnp.zeros_like(halo_ref)
        halo_ref[CONV_HALO - (CONV_W - 1):, :] = conv0_ref[0]

    h_ref[...] = _rmsnorm(x_ref[...], g_ref[...]).astype(BF16)
    dt = _softplus(_dot(h_ref[...], wdt_ref[...]) + dtb_ref[...])
    for j in range(tl // CHUNK):
        cs = slice(j * CHUNK, (j + 1) * CHUNK)
        acum, acum_t, src_t, w_t = _chunk_decays(dt[cs, :], apad_ref[...])
        acum_ref[0, cs, :] = acum
        tr_ref[0, j, 0] = acum_t
        tr_ref[0, j, 1] = src_t
        tr_ref[0, j, 2] = w_t
    for c in range(0, CONV_DIM, wc):
        sl = slice(c, c + wc)
        zs = slice(c // 2, c // 2 + wc // 2)
        z_ref[0, :, zs] = _dot(h_ref[...], win_ref[:, zs]).astype(z_ref.dtype)
        cur = _dot(h_ref[...], win_ref[:, D_INNER + c:D_INNER + c + wc])
        ext = jnp.concatenate([halo_ref[:, sl], cur], axis=0).reshape(1, nblk + 1, SUBLANES, wc)
        xbc_ref[0, :, sl] = _conv_silu(ext, cw_ref, cb_ref, sl).reshape(tl, wc)
        halo_ref[:, sl] = cur[tl - CONV_HALO:, :]
    convnew_ref[0] = halo_ref[CONV_HALO - (CONV_W - 1):, :]


def _ssd_in_prompt(x2d, b, l, g, w_in, wdt, cw, cb, dtb, conv0, a_pad):
    tl = SSD_PIPE_TL
    tps = l // tl
    ncs = tl // CHUNK
    blk = lambda w: pl.BlockSpec((1, tl, w), lambda i, j: (i, j, 0))
    return pl.pallas_call(
        _ssd_in_prompt_kernel,
        grid=(b, l // tl),
        in_specs=[pl.BlockSpec((tl, D_MODEL), lambda i, j: (i * tps + j, 0)),
                  _resident((1, D_MODEL)), _resident(w_in.shape), _resident((D_MODEL, DT_PAD)),
                  _resident((CONV_W, CONV_DIM)), _resident((1, CONV_DIM)), _resident((1, DT_PAD)),
                  pl.BlockSpec((1, CONV_W - 1, CONV_DIM), lambda i, j: (i, 0, 0)),
                  _resident((1, DT_PAD))],
        out_specs=[blk(D_INNER), blk(DT_PAD),
                   pl.BlockSpec((1, ncs, 3, DT_PAD, CHUNK), lambda i, j: (i, j, 0, 0, 0)),
                   blk(CONV_DIM),
                   pl.BlockSpec((1, CONV_W - 1, CONV_DIM), lambda i, j: (i, 0, 0))],
        out_shape=[jax.ShapeDtypeStruct((b, l, D_INNER), BF16),
                   jax.ShapeDtypeStruct((b, l, DT_PAD), F32),
                   jax.ShapeDtypeStruct((b, l // CHUNK, 3, DT_PAD, CHUNK), F32),
                   jax.ShapeDtypeStruct((b, l, CONV_DIM), F32),
                   jax.ShapeDtypeStruct((b, CONV_W - 1, CONV_DIM), F32)],
        scratch_shapes=[pltpu.VMEM((CONV_HALO, CONV_DIM), F32), pltpu.VMEM((tl, D_MODEL), BF16)],
        compiler_params=_params(2),
        name="ssd_in_prompt",
    )(x2d, g.reshape(1, D_MODEL), w_in, wdt, cw, cb.reshape(1, CONV_DIM),
      dtb.reshape(1, DT_PAD), conv0, a_pad)


B_OFF = D_INNER
C_OFF = D_INNER + N_GROUPS * D_STATE


def _gate_norm(y, z, ng):
    yg = y * _silu(z)
    ms = jnp.mean(yg * yg, axis=-1, keepdims=True)
    return yg * lax.rsqrt(ms + EPS) * ng


def _head_rows(mat, g):
    return jnp.concatenate(
        [jnp.broadcast_to(mat[g * HEADS_PER_GROUP + r:g * HEADS_PER_GROUP + r + 1, :],
                          (HEAD_DIM, mat.shape[1])) for r in range(HEADS_PER_GROUP)], axis=0)


def _scan_prompt_kernel(xbc_ref, z_ref, acum_ref, tr_ref, de_ref, ng_ref, y_ref, hout_ref, *h_refs,
                        chunks):
    q = CHUNK
    step = pl.program_id(1)

    @pl.when(step == 0)
    def _():
        for h_ref in h_refs:
            h_ref[...] = jnp.zeros_like(h_ref)

    tril = (lax.broadcasted_iota(jnp.int32, (q, q), 0) >= lax.broadcasted_iota(jnp.int32, (q, q), 1))
    lane_g = lax.broadcasted_iota(jnp.int32, (q, GROUP_DIM), 1)
    lane_half = lax.broadcasted_iota(jnp.int32, (q, LANES), 1) < HEAD_DIM

    def chunk_body(ci, carry):
        ts = pl.ds(pl.multiple_of(ci * q, q), q)
        acum = acum_ref[0, ts, :]
        acum_t, src_t, w_t = tr_ref[0, ci, 0], tr_ref[0, ci, 1], tr_ref[0, ci, 2]
        cd_b = jnp.broadcast_to(jnp.exp2(acum_t[:, q - 1:q]), (DT_PAD, D_STATE))
        for g in range(N_GROUPS):
            gs = slice(g * GROUP_DIM, (g + 1) * GROUP_DIM)
            bg = xbc_ref[0, ts, B_OFF + g * D_STATE:B_OFF + (g + 1) * D_STATE].astype(BF16)
            cg = xbc_ref[0, ts, C_OFF + g * D_STATE:C_OFF + (g + 1) * D_STATE].astype(BF16)
            xg = xbc_ref[0, ts, gs]
            xgb = xg.astype(BF16)
            cb = _dot_nt(cg, bg)
            ms, cols = [], []
            for r in range(HEADS_PER_GROUP):
                hd = g * HEADS_PER_GROUP + r
                colf = jnp.broadcast_to(acum[:, hd:hd + 1], (q, q))
                rowf = jnp.broadcast_to(src_t[hd:hd + 1, :], (q, q))
                ms.append((cb * jnp.exp2(jnp.where(tril, colf - rowf, -jnp.inf))).astype(BF16))
                cols.append(colf)
            mcat = jnp.concatenate(ms, axis=1)
            zero = jnp.zeros_like(xgb)
            bd = jnp.concatenate(
                [jnp.where((lane_g >= r * HEAD_DIM) & (lane_g < (r + 1) * HEAD_DIM), xgb, zero)
                 for r in range(HEADS_PER_GROUP)], axis=0)
            y_diag = _dot(mcat, bd)
            hg = h_refs[g][...]
            y_off = _dot_nt(cg, hg.astype(BF16))
            acum_e = jnp.concatenate([jnp.where(lane_half, cols[0], cols[1]),
                                      jnp.where(lane_half, cols[2], cols[3])], axis=1)
            y = y_diag + y_off * jnp.exp2(acum_e) + de_ref[:, gs] * xg
            xg_t = xg.T
            wt = jnp.concatenate(
                [xg_t[r * HEAD_DIM:(r + 1) * HEAD_DIM, :] * w_t[g * HEADS_PER_GROUP + r:
                                                                g * HEADS_PER_GROUP + r + 1, :]
                 for r in range(HEADS_PER_GROUP)], axis=0).astype(BF16)
            h_refs[g][...] = hg * _head_rows(cd_b, g) + _dot(wt, bg)
            zg = z_ref[0, ts, gs].astype(F32)
            y_ref[0, ts, gs] = _gate_norm(y, zg, ng_ref[:, gs]).astype(y_ref.dtype)
        return carry

    lax.fori_loop(0, chunks, chunk_body, 0)

    @pl.when(step == pl.num_programs(1) - 1)
    def _():
        for g, h_ref in enumerate(h_refs):
            hout_ref[0, g * GROUP_DIM:(g + 1) * GROUP_DIM, :] = h_ref[...]


SCAN_CHUNKS_PER_STEP = 4


def _scan_prompt(xbc, z, acum, tr, d_e, ng):
    b, l, _ = xbc.shape
    chunks = SCAN_CHUNKS_PER_STEP
    tl = chunks * CHUNK
    blk = lambda w: pl.BlockSpec((1, tl, w), lambda i, j: (i, j, 0))
    return pl.pallas_call(
        functools.partial(_scan_prompt_kernel, chunks=chunks),
        grid=(b, l // tl),
        in_specs=[blk(CONV_DIM), blk(D_INNER), blk(DT_PAD),
                  pl.BlockSpec((1, chunks, 3, DT_PAD, CHUNK), lambda i, j: (i, j, 0, 0, 0)),
                  _resident((1, D_INNER)), _resident((1, D_INNER))],
        out_specs=[blk(D_INNER), pl.BlockSpec((1, D_INNER, D_STATE), lambda i, j: (i, 0, 0))],
        out_shape=[jax.ShapeDtypeStruct((b, l, D_INNER), BF16),
                   jax.ShapeDtypeStruct((b, D_INNER, D_STATE), F32)],
        scratch_shapes=[pltpu.VMEM((GROUP_DIM, D_STATE), F32)] * N_GROUPS,
        compiler_params=_params(2),
        name="ssd_scan_prompt",
    )(xbc, z, acum, tr, d_e, ng)


SAMPLE_LEN = 8
SCAN_NB = 16
SCAN_ROWS = SCAN_NB * SAMPLE_LEN
SCAN_STEP_NB = 8


def _scan_sample_kernel(xbc_ref, z_ref, dt_ref, h0_ref, apad_ref, ae_ref, de_ref, ng_ref,
                        gsum_ref, gexp_ref, y_ref, hout_ref,
                        wt_ref, ea_ref, yd_ref, acp_ref, p_ref):
    rows = SCAN_ROWS
    j = pl.program_id(1)

    @pl.when(j == 0)
    def _():
        shape3 = lambda w: (SCAN_NB, SAMPLE_LEN, w)
        tok = lax.broadcasted_iota(jnp.int32, (rows, D_INNER), 0) & (SAMPLE_LEN - 1)
        tok_p = lax.broadcasted_iota(jnp.int32, (rows, DT_PAD), 0) & (SAMPLE_LEN - 1)

        def cumsum_tokens(v, t):
            for sh in (1, 2, 4):
                v = v + jnp.where(t >= sh, pltpu.roll(v, sh, 0), 0.0)
            return v

        def bcast_token(v, s):
            w = v.shape[-1]
            v3 = v.reshape(shape3(w))
            return jnp.broadcast_to(v3[:, s:s + 1, :], shape3(w)).reshape(rows, w)

        xs = xbc_ref[:, :, 0:D_INNER].reshape(rows, D_INNER)
        bm = xbc_ref[:, :, B_OFF:C_OFF].reshape(rows, N_GROUPS * D_STATE)
        cm = xbc_ref[:, :, C_OFF:CONV_DIM].reshape(rows, N_GROUPS * D_STATE)
        dt_p = dt_ref[:, :, 0:DT_PAD].reshape(rows, DT_PAD)
        dt_e = dt_ref[:, :, DT_PAD:DT_PAD + D_INNER].reshape(rows, D_INNER)
        acp_ref[...] = cumsum_tokens(dt_p * apad_ref[...], tok_p)
        acum_e = cumsum_tokens(dt_e * ae_ref[...], tok)
        xdt = xs * dt_e
        w = xdt * jnp.exp(bcast_token(acum_e, SAMPLE_LEN - 1) - acum_e)
        wt_ref[...] = w.T.astype(BF16)
        ea_ref[...] = jnp.exp(acum_e)
        for s in range(SAMPLE_LEN):
            p_ref[s * rows:(s + 1) * rows, :] = (cm * bcast_token(bm, s)).astype(BF16)
        cb_sum = _dot(p_ref[...], gsum_ref[...])
        yd = de_ref[...] * xs
        for s in range(SAMPLE_LEN):
            cb_e = _dot(cb_sum[s * rows:(s + 1) * rows, :].astype(BF16), gexp_ref[...])
            diff = acum_e - bcast_token(acum_e, s)
            decay = jnp.exp(jnp.where(tok >= s, diff, -jnp.inf))
            yd = yd + cb_e * decay * bcast_token(xdt, s)
        yd_ref[...] = yd

    eye = (lax.broadcasted_iota(jnp.int32, (N_HEADS, DT_PAD), 0)
           == lax.broadcasted_iota(jnp.int32, (N_HEADS, DT_PAD), 1))
    rowid = lax.broadcasted_iota(jnp.int32, (rows, D_STATE), 0)
    for t in range(SCAN_STEP_NB):
        seq = j * SCAN_STEP_NB + t
        r0 = pl.multiple_of(seq * SAMPLE_LEN, SAMPLE_LEN)
        alast = acp_ref[pl.ds(r0 + SAMPLE_LEN - 1, 1), :]
        alast_col = jnp.sum(jnp.where(eye, jnp.broadcast_to(alast, (N_HEADS, DT_PAD)), 0.0),
                            axis=1, keepdims=True)
        cd_b = jnp.broadcast_to(jnp.exp(alast_col), (N_HEADS, D_STATE))
        mine = (rowid >= r0) & (rowid < r0 + SAMPLE_LEN)
        y_offs = []
        for g in range(N_GROUPS):
            gs = slice(g * GROUP_DIM, (g + 1) * GROUP_DIM)
            hg = h0_ref[t, gs, :]
            cg = xbc_ref[seq, :, C_OFF + g * D_STATE:C_OFF + (g + 1) * D_STATE].astype(BF16)
            y_offs.append(_dot_nt(cg, hg.astype(BF16)))
            b_all = xbc_ref[:, :, B_OFF + g * D_STATE:B_OFF + (g + 1) * D_STATE].reshape(
                rows, D_STATE)
            b_mine = jnp.where(mine, b_all, 0.0).astype(BF16)
            hout_ref[t, gs, :] = hg * _head_rows(cd_b, g) + _dot(wt_ref[gs, :], b_mine)
        y = yd_ref[pl.ds(r0, SAMPLE_LEN), :] + jnp.concatenate(y_offs, axis=1) * ea_ref[
            pl.ds(r0, SAMPLE_LEN), :]
        z = z_ref[seq]
        y_ref[seq] = jnp.concatenate(
            [_gate_norm(y[:, g * GROUP_DIM:(g + 1) * GROUP_DIM],
                        z[:, g * GROUP_DIM:(g + 1) * GROUP_DIM],
                        ng_ref[:, g * GROUP_DIM:(g + 1) * GROUP_DIM]) for g in range(N_GROUPS)],
            axis=1)


def _scan_sample(xbc, z, dt, h0, a_pad, a_e, d_e, ng, gsum, gexp):
    b = xbc.shape[0]
    dt_w = dt.shape[-1]
    blk = lambda w: pl.BlockSpec((SCAN_NB, SAMPLE_LEN, w), lambda i, j: (i, 0, 0))
    steps = SCAN_NB // SCAN_STEP_NB
    st = pl.BlockSpec((SCAN_STEP_NB, D_INNER, D_STATE), lambda i, j: (i * steps + j, 0, 0))
    return pl.pallas_call(
        _scan_sample_kernel,
        grid=(b // SCAN_NB, steps),
        in_specs=[blk(CONV_DIM), blk(D_INNER), blk(dt_w), st, _resident((1, DT_PAD)),
                  _resident((1, D_INNER)), _resident((1, D_INNER)), _resident((1, D_INNER)),
                  _resident(gsum.shape), _resident(gexp.shape)],
        out_specs=[blk(D_INNER), st],
        out_shape=[jax.ShapeDtypeStruct((b, SAMPLE_LEN, D_INNER), F32),
                   jax.ShapeDtypeStruct((b, D_INNER, D_STATE), F32)],
        scratch_shapes=[pltpu.VMEM((D_INNER, SCAN_ROWS), BF16),
                        pltpu.VMEM((SCAN_ROWS, D_INNER), F32),
                        pltpu.VMEM((SCAN_ROWS, D_INNER), F32),
                        pltpu.VMEM((SCAN_ROWS, DT_PAD), F32),
                        pltpu.VMEM((SAMPLE_LEN * SCAN_ROWS, N_GROUPS * D_STATE), BF16)],
        compiler_params=_params(2),
        name="ssd_scan_sample",
    )(xbc, z, dt, h0, a_pad, a_e, d_e, ng, gsum, gexp)


POOL_HALO = 2 * SUBLANES


def _pool_kernel(x_ref, g_ref, win_ref, wgrp_ref, scale_ref, wout_ref, buf0_ref,
                 o_ref, bufnew_ref, ext_ref, *, nb, tl, pos0):
    rows = nb * tl
    jt = pl.program_id(1)

    @pl.when(jt == 0)
    def _():
        ext_ref[:, 0:POOL_HALO, :] = jnp.zeros((nb, POOL_HALO, D_MODEL), F32)
        ext_ref[:, POOL_HALO - (MAX_WIN - 1):POOL_HALO, :] = buf0_ref[...]

    x = x_ref[...]
    h = _rmsnorm(x, g_ref[...]).astype(BF16)
    ext_ref[:, POOL_HALO:POOL_HALO + tl, :] = _dot(h, win_ref[...]).reshape(nb, tl, D_MODEL)
    nblk = tl // SUBLANES
    halo_blocks = POOL_HALO // SUBLANES
    shape4 = (nb, nblk, SUBLANES, POOL_GROUP_DIM)
    pos = (pos0 + jt * tl + SUBLANES * lax.broadcasted_iota(jnp.int32, shape4, 1)
           + lax.broadcasted_iota(jnp.int32, shape4, 2)).astype(F32)
    mixed = []
    for k, w in enumerate(POOL_WINDOWS):
        sl = slice(k * POOL_GROUP_DIM, (k + 1) * POOL_GROUP_DIM)
        ext = ext_ref[:, :, sl].reshape(nb, nblk + halo_blocks, SUBLANES, POOL_GROUP_DIM)
        tot, shift = ext, 1
        while shift < w:
            tot = tot + _shift_rows(tot, shift)
            shift *= 2
        u = ext[:, halo_blocks:]
        mean = tot[:, halo_blocks:] / jnp.minimum(jnp.float32(w), pos + 1.0)
        m = (mean - u).reshape(rows, POOL_GROUP_DIM).astype(BF16)
        mixed.append(_dot(m, wgrp_ref[k]))
    mixed = (jnp.concatenate(mixed, axis=1) * scale_ref[...]).astype(BF16)
    o_ref[...] = x + _dot(mixed, wout_ref[...])
    bufnew_ref[...] = ext_ref[:, tl + 1:tl + POOL_HALO, :]
    ext_ref[:, 0:POOL_HALO, :] = ext_ref[:, tl:tl + POOL_HALO, :]


def _pool(x2d, row0, b, l, g, w_in, w_grp, scale, w_out, buf0, *, nb, tl, pos0):
    assert row0 % (nb * tl) == 0 and (nb == 1 or l == tl)
    tile0, tps = row0 // (nb * tl), l // tl
    return pl.pallas_call(
        functools.partial(_pool_kernel, nb=nb, tl=tl, pos0=pos0),
        grid=(b // nb, tps),
        in_specs=[pl.BlockSpec((nb * tl, D_MODEL), lambda i, j: (tile0 + i * tps + j, 0)),
                  _resident((1, D_MODEL)), _resident((D_MODEL, D_MODEL)),
                  _resident(w_grp.shape), _resident((1, D_MODEL)), _resident((D_MODEL, D_MODEL)),
                  pl.BlockSpec((nb, MAX_WIN - 1, D_MODEL), lambda i, j: (i, 0, 0))],
        out_specs=[pl.BlockSpec((nb * tl, D_MODEL), lambda i, j: (i * tps + j, 0)),
                   pl.BlockSpec((nb, MAX_WIN - 1, D_MODEL), lambda i, j: (i, 0, 0))],
        out_shape=[jax.ShapeDtypeStruct((b * l, D_MODEL), F32),
                   jax.ShapeDtypeStruct((b, MAX_WIN - 1, D_MODEL), F32)],
        scratch_shapes=[pltpu.VMEM((nb, POOL_HALO + tl, D_MODEL), F32)],
        compiler_params=_params(2),
        name="pool_mixer",
    )(x2d, g.reshape(1, D_MODEL), w_in, w_grp, scale.reshape(1, D_MODEL), w_out, buf0)


def _expand_heads(v):
    return jnp.repeat(v.astype(F32), HEAD_DIM).reshape(1, D_INNER)


def _pad_heads(v):
    return jnp.pad(v.astype(F32), (0, DT_PAD - N_HEADS)).reshape(1, DT_PAD)


def _group_sum_matrix():
    m = np.zeros((N_GROUPS * D_STATE, LANES), np.float32)
    for g in range(N_GROUPS):
        m[g * D_STATE:(g + 1) * D_STATE, g] = 1.0
    return jnp.asarray(m, BF16)


def _group_expand_matrix():
    m = np.zeros((LANES, D_INNER), np.float32)
    for g in range(N_GROUPS):
        m[g, g * GROUP_DIM:(g + 1) * GROUP_DIM] = 1.0
    return jnp.asarray(m, BF16)


def _trunks(x_prompt, x_sample, ssm0_s, conv0_s, pool0_s, p):
    bp, lp, _ = x_prompt.shape
    bs, ls, _ = x_sample.shape
    rows = (bp * lp, bs * ls)
    bf = lambda a: a.astype(BF16)
    w_gate, w_up, w_down = p["ffn_w_gate"], p["ffn_w_up"], p["ffn_w_down"]
    ffn = lambda xs, i, k, **kw: _ffn(xs, rows, p["ffn_norm"][i, k], w_gate, w_up, w_down, (i, k), **kw)
    x2d = ffn([x_prompt.reshape(rows[0], D_MODEL), x_sample.reshape(rows[1], D_MODEL)], 0, 0)

    w_in = p["ssd_w_in"][0]
    w_dt = w_in[:, D_INNER + CONV_DIM:]
    dt_bias = p["ssd_dt_bias"][0]
    wdt = jnp.pad(w_dt, ((0, 0), (0, DT_PAD - N_HEADS)))
    dtb = jnp.pad(dt_bias, (0, DT_PAD - N_HEADS))
    wdt_s = jnp.concatenate([wdt, jnp.repeat(w_dt, HEAD_DIM, axis=1)], axis=1)
    dtb_s = jnp.concatenate([dtb, jnp.repeat(dt_bias, HEAD_DIM)])
    a_neg = -jnp.exp(p["ssd_a_log"][0].astype(F32))
    a_pad, d_e = _pad_heads(a_neg), _expand_heads(p["ssd_d"][0])
    ng = p["ssd_norm"][0].reshape(1, D_INNER)
    common = (p["mix_norm"][0], bf(w_in))
    conv_w, conv_b = p["ssd_conv_w"][0], p["ssd_conv_b"][0]
    z, acum, tr, xbc, conv_p = _ssd_in_prompt(x2d, bp, lp, *common, bf(wdt), conv_w, conv_b, dtb,
                                              jnp.zeros((bp, CONV_W - 1, CONV_DIM), F32), a_pad)
    y_p, ssm_p = _scan_prompt(xbc, z, acum, tr, d_e, ng)
    z, xbc, dt, conv_s = _ssd_in(x2d, rows[0], bs, ls, *common, bf(wdt_s), conv_w, conv_b, dtb_s,
                                 conv0_s, nb=32, tl=SAMPLE_LEN)
    y_s, ssm_s = _scan_sample(xbc, z, dt, ssm0_s.reshape(bs, D_INNER, D_STATE), a_pad,
                              _expand_heads(a_neg), d_e, ng, _group_sum_matrix(),
                              _group_expand_matrix())
    x2d = ffn([x2d], 0, 1, pre=([y_p.reshape(rows[0], D_INNER), bf(y_s).reshape(rows[1], D_INNER)],
                                bf(p["ssd_w_out"][0])))

    x2d = ffn([x2d], 1, 0)
    pool_w = (p["mix_norm"][1], bf(p["pool_w_in"][0]), bf(p["pool_w_group"][0]), p["pool_scale"][0],
              bf(p["pool_w_out"][0]))
    xm_p, pool_p = _pool(x2d, 0, bp, lp, *pool_w, jnp.zeros((bp, MAX_WIN - 1, D_MODEL), F32),
                         nb=1, tl=512, pos0=0)
    xm_s, pool_s = _pool(x2d, rows[0], bs, ls, *pool_w, pool0_s, nb=64, tl=SAMPLE_LEN, pos0=PAST_LEN)
    out_p, out_s = ffn([xm_p, xm_s], 1, 1, final_g=p["final_norm"], split_out=True)
    state = lambda a, b: a.reshape(b, N_HEADS, HEAD_DIM, D_STATE)[None]
    return (out_p.reshape(bp, lp, D_MODEL), out_s.reshape(bs, ls, D_MODEL),
            state(ssm_p, bp), conv_p[None], pool_p[None], state(ssm_s, bs), conv_s[None], pool_s[None])


def kernel(x_prompt, x_sample, state_ssm, state_conv, state_pool, ffn_norm, ffn_w_gate, ffn_w_up,
           ffn_w_down, mix_norm, ssd_w_in, ssd_conv_w, ssd_conv_b, ssd_dt_bias, ssd_a_log, ssd_d,
           ssd_norm, ssd_w_out, pool_w_in, pool_w_group, pool_scale, pool_w_out, final_norm):
    p = dict(ffn_norm=ffn_norm, ffn_w_gate=ffn_w_gate, ffn_w_up=ffn_w_up, ffn_w_down=ffn_w_down,
             mix_norm=mix_norm, ssd_w_in=ssd_w_in, ssd_conv_w=ssd_conv_w, ssd_conv_b=ssd_conv_b,
             ssd_dt_bias=ssd_dt_bias, ssd_a_log=ssd_a_log, ssd_d=ssd_d, ssd_norm=ssd_norm,
             ssd_w_out=ssd_w_out, pool_w_in=pool_w_in, pool_w_group=pool_w_group,
             pool_scale=pool_scale, pool_w_out=pool_w_out, final_norm=final_norm)
    return _trunks(x_prompt, x_sample, state_ssm[0], state_conv[0], state_pool[0], p)
```

```python
import functools

import numpy as np
import jax
import jax.numpy as jnp
from jax import lax
from jax.experimental import pallas as pl
from jax.experimental.pallas import tpu as pltpu

F32 = jnp.float32
BF16 = jnp.bfloat16

EPS = 1e-6
D_MODEL = 1024
D_FF = 2816
D_INNER = 2048
HEAD_DIM = 64
N_HEADS = 32
N_GROUPS = 8
HEADS_PER_GROUP = 4
GROUP_DIM = HEADS_PER_GROUP * HEAD_DIM
D_STATE = 128
CONV_W = 4
CONV_DIM = D_INNER + 2 * N_GROUPS * D_STATE
CHUNK = 128
POOL_WINDOWS = (2, 4, 8, 16)
POOL_GROUP_DIM = 256
MAX_WIN = 16
PAST_LEN = 16384
LANES = 128
SUBLANES = 8
DT_PAD = LANES
VMEM_LIMIT = 60 * 1024 * 1024

NT_DIMS = (((1,), (1,)), ((), ()))
LOG2E = 1.4426950408889634


def _resident(shape, lead=()):
    nd = len(shape)
    return pl.BlockSpec((None,) * len(lead) + tuple(shape), lambda *_: tuple(lead) + (0,) * nd,
                        pipeline_mode=pl.Buffered(1))


def _params(n_axes):
    return pltpu.CompilerParams(dimension_semantics=("arbitrary",) * n_axes,
                                vmem_limit_bytes=VMEM_LIMIT)


def _rmsnorm(x, g):
    ms = jnp.mean(x * x, axis=-1, keepdims=True)
    return x * lax.rsqrt(ms + EPS) * g


def _silu(x):
    return x * jax.nn.sigmoid(x)


def _softplus(x):
    return jnp.maximum(x, 0.0) + jnp.log(1.0 + jnp.exp(-jnp.abs(x)))


def _dot(a, b):
    return jnp.dot(a, b, preferred_element_type=F32)


def _dot_nt(a, b):
    return lax.dot_general(a, b, NT_DIMS, preferred_element_type=F32)


def _dot_exact01(t, x):
    hi = x.astype(BF16)
    r1 = x - hi.astype(F32)
    mid = r1.astype(BF16)
    lo = (r1 - mid.astype(F32)).astype(BF16)
    return _dot(t, hi) + _dot(t, mid) + _dot(t, lo)


FFN_TM = 512
FFN_TF = 256


def _ffn_kernel(*refs, n_x, n_y, final, n_out, tiles_a):
    refs = list(refs)
    x_refs = [refs.pop(0) for _ in range(n_x)]
    y_refs = [refs.pop(0) for _ in range(n_y)]
    wpre_ref = refs.pop(0) if n_y else None
    g_ref, wg_ref, wu_ref, wd_ref = refs[:4]
    del refs[:4]
    fg_ref = refs.pop(0) if final else None
    o_refs = [refs.pop(0) for _ in range(n_out)]
    h_ref, a_ref = refs
    second = pl.program_id(0) >= tiles_a

    def pick(rs, dtype):
        if len(rs) == 1:
            return rs[0][...].astype(dtype)
        return jnp.where(second, rs[1][...].astype(dtype), rs[0][...].astype(dtype))

    x = pick(x_refs, F32)
    if n_y:
        x = x + _dot(pick(y_refs, BF16), wpre_ref[...])
    h_ref[...] = _rmsnorm(x, g_ref[...]).astype(BF16)
    for f in range(0, D_FF, FFN_TF):
        h = h_ref[...]
        gate = _dot(h, wg_ref[:, f:f + FFN_TF].astype(BF16))
        up = _dot(h, wu_ref[:, f:f + FFN_TF].astype(BF16))
        a_ref[:, f:f + FFN_TF] = (_silu(gate) * up).astype(BF16)
    y = x + 0.5 * _dot(a_ref[...], wd_ref[...].astype(BF16))
    if final:
        y = _rmsnorm(y, fg_ref[...])
    if n_out == 1:
        o_refs[0][...] = y
    else:
        @pl.when(jnp.logical_not(second))
        def _():
            o_refs[0][...] = y

        @pl.when(second)
        def _():
            o_refs[1][...] = y


def _ffn(xs, rows, g, w_gate, w_up, w_down, idx, final_g=None, pre=None, split_out=False):
    tm = FFN_TM
    tiles_a, tiles_b = rows[0] // tm, rows[1] // tm
    assert rows[0] % tm == 0 and rows[1] % tm == 0
    final = final_g is not None

    def row_specs(arrs):
        if len(arrs) == 1:
            return [pl.BlockSpec((tm, arrs[0].shape[1]), lambda i: (i, 0))]
        return [pl.BlockSpec((tm, arrs[0].shape[1]), lambda i: (jnp.minimum(i, tiles_a - 1), 0)),
                pl.BlockSpec((tm, arrs[1].shape[1]), lambda i: (jnp.maximum(i - tiles_a, 0), 0))]

    in_specs, args = row_specs(xs), list(xs)
    ys = []
    if pre is not None:
        ys, wpre = pre
        in_specs += row_specs(ys) + [_resident(wpre.shape)]
        args += list(ys) + [wpre]
    in_specs += [_resident((1, D_MODEL)), _resident((D_MODEL, D_FF), idx),
                 _resident((D_MODEL, D_FF), idx), _resident((D_FF, D_MODEL), idx)]
    args += [g.reshape(1, D_MODEL), w_gate, w_up, w_down]
    if final:
        in_specs.append(_resident((1, D_MODEL)))
        args.append(final_g.reshape(1, D_MODEL))
    if split_out:
        outs = [jax.ShapeDtypeStruct((r, D_MODEL), F32) for r in rows]
    else:
        outs = [jax.ShapeDtypeStruct((rows[0] + rows[1], D_MODEL), F32)]
    res = pl.pallas_call(
        functools.partial(_ffn_kernel, n_x=len(xs), n_y=len(ys), final=final, n_out=len(outs),
                          tiles_a=tiles_a),
        grid=(tiles_a + tiles_b,),
        in_specs=in_specs,
        out_specs=row_specs(outs),
        out_shape=outs,
        scratch_shapes=[pltpu.VMEM((tm, D_MODEL), BF16), pltpu.VMEM((tm, D_FF), BF16)],
        compiler_params=_params(1),
        name="ffn" + ("_pre" if pre is not None else "") + ("_final" if final else ""),
    )(*args)
    return res if split_out else res[0]


CONV_HALO = SUBLANES
SSD_IN_LANE_CHUNK = 512


def _shift_rows(v, k):
    if k == SUBLANES:
        return jnp.concatenate([v[:, :1], v[:, :-1]], axis=1)
    r = pltpu.roll(v, k, 2)
    prev = jnp.concatenate([r[:, :1], r[:, :-1]], axis=1)
    sub = lax.broadcasted_iota(jnp.int32, v.shape, 2)
    return jnp.where(sub < k, prev, r)


def _chunk_decays(dt, a_pad):
    q = dt.shape[0]
    tril = (lax.broadcasted_iota(jnp.int32, (q, q), 0) >= lax.broadcasted_iota(jnp.int32, (q, q), 1))
    acum = _dot_exact01(tril.astype(BF16), dt * a_pad) * LOG2E
    acum_t = acum.T
    dt_t = dt.T
    src_t = acum_t - jnp.log2(dt_t)
    w_t = jnp.exp2(acum_t[:, q - 1:q] - acum_t) * dt_t
    return acum, acum_t, src_t, w_t


def _conv_silu(ext, cw_ref, cb_ref, sl):
    w = ext.shape[-1]
    tap = lambda k: cw_ref[k:k + 1, sl].reshape(1, 1, 1, w)
    ext1 = _shift_rows(ext, 1)
    p = ext * tap(3) + ext1 * tap(2)
    q = ext * tap(1) + ext1 * tap(0)
    return _silu((cb_ref[:, sl].reshape(1, 1, 1, w) + p + _shift_rows(q, 2))[:, 1:])


def _ssd_in_kernel(x_ref, g_ref, win_ref, wdt_ref, cw_ref, cb_ref, dtb_ref, conv0_ref,
                   z_ref, xbc_ref, dt_ref, convnew_ref, halo_ref, h_ref, *, nb, tl):
    rows = nb * tl
    wc = SSD_IN_LANE_CHUNK
    nblk = tl // SUBLANES

    @pl.when(pl.program_id(1) == 0)
    def _():
        halo_ref[...] = jnp.zeros_like(halo_ref)
        halo_ref[:, CONV_HALO - (CONV_W - 1):, :] = conv0_ref[...]

    h_ref[...] = _rmsnorm(x_ref[...], g_ref[...]).astype(BF16)
    dt = _softplus(_dot(h_ref[...], wdt_ref[...]) + dtb_ref[...])
    dt_ref[...] = dt.reshape(nb, tl, dt.shape[-1])
    for c in range(0, CONV_DIM, wc):
        sl = slice(c, c + wc)
        zs = slice(c // 2, c // 2 + wc // 2)
        z_ref[:, :, zs] = _dot(h_ref[...], win_ref[:, zs]).reshape(nb, tl, wc // 2)
        cur = _dot(h_ref[...], win_ref[:, D_INNER + c:D_INNER + c + wc]).reshape(nb, tl, wc)
        ext = jnp.concatenate([halo_ref[:, :, sl], cur], axis=1).reshape(nb, nblk + 1, SUBLANES, wc)
        xbc_ref[:, :, sl] = _conv_silu(ext, cw_ref, cb_ref, sl).reshape(nb, tl, wc)
        halo_ref[:, :, sl] = cur[:, tl - CONV_HALO:, :]
    convnew_ref[...] = halo_ref[:, CONV_HALO - (CONV_W - 1):, :]


def _ssd_in(x2d, row0, b, l, g, w_in, wdt, cw, cb, dtb, conv0, *, nb, tl):
    assert l == tl and row0 % (nb * tl) == 0
    tile0 = row0 // (nb * tl)
    dt_w = wdt.shape[1]
    blk = lambda w: pl.BlockSpec((nb, tl, w), lambda i, j: (i, j, 0))
    return pl.pallas_call(
        functools.partial(_ssd_in_kernel, nb=nb, tl=tl),
        grid=(b // nb, l // tl),
        in_specs=[pl.BlockSpec((nb * tl, D_MODEL), lambda i, j: (tile0 + i, 0)),
                  _resident((1, D_MODEL)), _resident(w_in.shape), _resident((D_MODEL, dt_w)),
                  _resident((CONV_W, CONV_DIM)), _resident((1, CONV_DIM)), _resident((1, dt_w)),
                  pl.BlockSpec((nb, CONV_W - 1, CONV_DIM), lambda i, j: (i, 0, 0))],
        out_specs=[blk(D_INNER), blk(CONV_DIM), blk(dt_w),
                   pl.BlockSpec((nb, CONV_W - 1, CONV_DIM), lambda i, j: (i, 0, 0))],
        out_shape=[jax.ShapeDtypeStruct((b, l, D_INNER), F32),
                   jax.ShapeDtypeStruct((b, l, CONV_DIM), F32),
                   jax.ShapeDtypeStruct((b, l, dt_w), F32),
                   jax.ShapeDtypeStruct((b, CONV_W - 1, CONV_DIM), F32)],
        scratch_shapes=[pltpu.VMEM((nb, CONV_HALO, CONV_DIM), F32),
                        pltpu.VMEM((nb * tl, D_MODEL), BF16)],
        compiler_params=_params(2),
        name="ssd_in",
    )(x2d, g.reshape(1, D_MODEL), w_in, wdt, cw, cb.reshape(1, CONV_DIM),
      dtb.reshape(1, dt_w), conv0)


SSD_PIPE_TL = 512


def _ssd_in_prompt_kernel(x_ref, g_ref, win_ref, wdt_ref, cw_ref, cb_ref, dtb_ref, conv0_ref,
                          apad_ref, z_ref, acum_ref, tr_ref, xbc_ref, convnew_ref, halo_ref, h_ref):
    tl = SSD_PIPE_TL
    wc = SSD_IN_LANE_CHUNK
    nblk = tl // SUBLANES

    @pl.when(pl.program_id(1) == 0)
    def _():
        halo_ref[...] = jnp.zeros_like(halo_ref)
        halo_ref[CONV_HALO - (CONV_W - 1):, :] = conv0_ref[0]

    h_ref[...] = _rmsnorm(x_ref[...], g_ref[...]).astype(BF16)
    dt = _softplus(_dot(h_ref[...], wdt_ref[...]) + dtb_ref[...])
    for j in range(tl // CHUNK):
        cs = slice(j * CHUNK, (j + 1) * CHUNK)
        acum, acum_t, src_t, w_t = _chunk_decays(dt[cs, :], apad_ref[...])
        acum_ref[0, cs, :] = acum
        tr_ref[0, j, 0] = acum_t
        tr_ref[0, j, 1] = src_t
        tr_ref[0, j, 2] = w_t
    for c in range(0, CONV_DIM, wc):
        sl = slice(c, c + wc)
        zs = slice(c // 2, c // 2 + wc // 2)
        z_ref[0, :, zs] = _dot(h_ref[...], win_ref[:, zs]).astype(z_ref.dtype)
        cur = _dot(h_ref[...], win_ref[:, D_INNER + c:D_INNER + c + wc])
        ext = jnp.concatenate([halo_ref[:, sl], cur], axis=0).reshape(1, nblk + 1, SUBLANES, wc)
        xbc_ref[0, :, sl] = _conv_silu(ext, cw_ref, cb_ref, sl).reshape(tl, wc)
        halo_ref[:, sl] = cur[tl - CONV_HALO:, :]
    convnew_ref[0] = halo_ref[CONV_HALO - (CONV_W - 1):, :]


def _ssd_in_prompt(x2d, b, l, g, w_in, wdt, cw, cb, dtb, conv0, a_pad):
    tl = SSD_PIPE_TL
    tps = l // tl
    ncs = tl // CHUNK
    blk = lambda w: pl.BlockSpec((1, tl, w), lambda i, j: (i, j, 0))
    return pl.pallas_call(
        _ssd_in_prompt_kernel,
        grid=(b, l // tl),
        in_specs=[pl.BlockSpec((tl, D_MODEL), lambda i, j: (i * tps + j, 0)),
                  _resident((1, D_MODEL)), _resident(w_in.shape), _resident((D_MODEL, DT_PAD)),
                  _resident((CONV_W, CONV_DIM)), _resident((1, CONV_DIM)), _resident((1, DT_PAD)),
                  pl.BlockSpec((1, CONV_W - 1, CONV_DIM), lambda i, j: (i, 0, 0)),
                  _resident((1, DT_PAD))],
        out_specs=[blk(D_INNER), blk(DT_PAD),
                   pl.BlockSpec((1, ncs, 3, DT_PAD, CHUNK), lambda i, j: (i, j, 0, 0, 0)),
                   blk(CONV_DIM),
                   pl.BlockSpec((1, CONV_W - 1, CONV_DIM), lambda i, j: (i, 0, 0))],
        out_shape=[jax.ShapeDtypeStruct((b, l, D_INNER), BF16),
                   jax.ShapeDtypeStruct((b, l, DT_PAD), F32),
                   jax.ShapeDtypeStruct((b, l // CHUNK, 3, DT_PAD, CHUNK), F32),
                   jax.ShapeDtypeStruct((b, l, CONV_DIM), F32),
                   jax.ShapeDtypeStruct((b, CONV_W - 1, CONV_DIM), F32)],
        scratch_shapes=[pltpu.VMEM((CONV_HALO, CONV_DIM), F32), pltpu.VMEM((tl, D_MODEL), BF16)],
        compiler_params=_params(2),
        name="ssd_in_prompt",
    )(x2d, g.reshape(1, D_MODEL), w_in, wdt, cw, cb.reshape(1, CONV_DIM),
      dtb.reshape(1, DT_PAD), conv0, a_pad)


B_OFF = D_INNER
C_OFF = D_INNER + N_GROUPS * D_STATE


def _gate_norm(y, z, ng):
    yg = y * _silu(z)
    ms = jnp.mean(yg * yg, axis=-1, keepdims=True)
    return yg * lax.rsqrt(ms + EPS) * ng


def _head_rows(mat, g):
    return jnp.concatenate(
        [jnp.broadcast_to(mat[g * HEADS_PER_GROUP + r:g * HEADS_PER_GROUP + r + 1, :],
                          (HEAD_DIM, mat.shape[1])) for r in range(HEADS_PER_GROUP)], axis=0)


def _scan_prompt_kernel(xbc_ref, z_ref, acum_ref, tr_ref, de_ref, ng_ref, y_ref, hout_ref, *h_refs,
                        chunks):
    q = CHUNK
    step = pl.program_id(1)

    @pl.when(step == 0)
    def _():
        for h_ref in h_refs:
            h_ref[...] = jnp.zeros_like(h_ref)

    tril = (lax.broadcasted_iota(jnp.int32, (q, q), 0) >= lax.broadcasted_iota(jnp.int32, (q, q), 1))
    lane_g = lax.broadcasted_iota(jnp.int32, (q, GROUP_DIM), 1)
    lane_half = lax.broadcasted_iota(jnp.int32, (q, LANES), 1) < HEAD_DIM

    def chunk_body(ci, carry):
        ts = pl.ds(pl.multiple_of(ci * q, q), q)
        acum = acum_ref[0, ts, :]
        acum_t, src_t, w_t = tr_ref[0, ci, 0], tr_ref[0, ci, 1], tr_ref[0, ci, 2]
        cd_b = jnp.broadcast_to(jnp.exp2(acum_t[:, q - 1:q]), (DT_PAD, D_STATE))
        for g in range(N_GROUPS):
            gs = slice(g * GROUP_DIM, (g + 1) * GROUP_DIM)
            bg = xbc_ref[0, ts, B_OFF + g * D_STATE:B_OFF + (g + 1) * D_STATE].astype(BF16)
            cg = xbc_ref[0, ts, C_OFF + g * D_STATE:C_OFF + (g + 1) * D_STATE].astype(BF16)
            xg = xbc_ref[0, ts, gs]
            xgb = xg.astype(BF16)
            cb = _dot_nt(cg, bg)
            ms, cols = [], []
            for r in range(HEADS_PER_GROUP):
                hd = g * HEADS_PER_GROUP + r
                colf = jnp.broadcast_to(acum[:, hd:hd + 1], (q, q))
                rowf = jnp.broadcast_to(src_t[hd:hd + 1, :], (q, q))
                ms.append((cb * jnp.exp2(jnp.where(tril, colf - rowf, -jnp.inf))).astype(BF16))
                cols.append(colf)
            zero = jnp.zeros_like(xgb)
            y_diag = sum(
                _dot(ms[r], jnp.where((lane_g >= r * HEAD_DIM) & (lane_g < (r + 1) * HEAD_DIM),
                                      xgb, zero)) for r in range(HEADS_PER_GROUP))
            hg = h_refs[g][...]
            y_off = _dot_nt(cg, hg.astype(BF16))
            acum_e = jnp.concatenate([jnp.where(lane_half, cols[0], cols[1]),
                                      jnp.where(lane_half, cols[2], cols[3])], axis=1)
            y = y_diag + y_off * jnp.exp2(acum_e) + de_ref[:, gs] * xg
            xg_t = xg.T
            wt = jnp.concatenate(
                [xg_t[r * HEAD_DIM:(r + 1) * HEAD_DIM, :] * w_t[g * HEADS_PER_GROUP + r:
                                                                g * HEADS_PER_GROUP + r + 1, :]
                 for r in range(HEADS_PER_GROUP)], axis=0).astype(BF16)
            h_refs[g][...] = hg * _head_rows(cd_b, g) + _dot(wt, bg)
            zg = z_ref[0, ts, gs].astype(F32)
            y_ref[0, ts, gs] = _gate_norm(y, zg, ng_ref[:, gs]).astype(y_ref.dtype)
        return carry

    lax.fori_loop(0, chunks, chunk_body, 0)

    @pl.when(step == pl.num_programs(1) - 1)
    def _():
        for g, h_ref in enumerate(h_refs):
            hout_ref[0, g * GROUP_DIM:(g + 1) * GROUP_DIM, :] = h_ref[...]


SCAN_CHUNKS_PER_STEP = 4


def _scan_prompt(xbc, z, acum, tr, d_e, ng):
    b, l, _ = xbc.shape
    chunks = SCAN_CHUNKS_PER_STEP
    tl = chunks * CHUNK
    blk = lambda w: pl.BlockSpec((1, tl, w), lambda i, j: (i, j, 0))
    return pl.pallas_call(
        functools.partial(_scan_prompt_kernel, chunks=chunks),
        grid=(b, l // tl),
        in_specs=[blk(CONV_DIM), blk(D_INNER), blk(DT_PAD),
                  pl.BlockSpec((1, chunks, 3, DT_PAD, CHUNK), lambda i, j: (i, j, 0, 0, 0)),
                  _resident((1, D_INNER)), _resident((1, D_INNER))],
        out_specs=[blk(D_INNER), pl.BlockSpec((1, D_INNER, D_STATE), lambda i, j: (i, 0, 0))],
        out_shape=[jax.ShapeDtypeStruct((b, l, D_INNER), BF16),
                   jax.ShapeDtypeStruct((b, D_INNER, D_STATE), F32)],
        scratch_shapes=[pltpu.VMEM((GROUP_DIM, D_STATE), F32)] * N_GROUPS,
        compiler_params=_params(2),
        name="ssd_scan_prompt",
    )(xbc, z, acum, tr, d_e, ng)


SAMPLE_LEN = 8
SCAN_NB = 16
SCAN_ROWS = SCAN_NB * SAMPLE_LEN
SCAN_STEP_NB = 8


def _scan_sample_kernel(xbc_ref, z_ref, dt_ref, h0_ref, apad_ref, ae_ref, de_ref, ng_ref,
                        gsum_ref, gexp_ref, y_ref, hout_ref,
                        wt_ref, ea_ref, yd_ref, acp_ref, p_ref):
    rows = SCAN_ROWS
    j = pl.program_id(1)

    @pl.when(j == 0)
    def _():
        shape3 = lambda w: (SCAN_NB, SAMPLE_LEN, w)
        tok = lax.broadcasted_iota(jnp.int32, (rows, D_INNER), 0) & (SAMPLE_LEN - 1)
        tok_p = lax.broadcasted_iota(jnp.int32, (rows, DT_PAD), 0) & (SAMPLE_LEN - 1)

        def cumsum_tokens(v, t):
            for sh in (1, 2, 4):
                v = v + jnp.where(t >= sh, pltpu.roll(v, sh, 0), 0.0)
            return v

        def bcast_token(v, s):
            w = v.shape[-1]
            v3 = v.reshape(shape3(w))
            return jnp.broadcast_to(v3[:, s:s + 1, :], shape3(w)).reshape(rows, w)

        xs = xbc_ref[:, :, 0:D_INNER].reshape(rows, D_INNER)
        bm = xbc_ref[:, :, B_OFF:C_OFF].reshape(rows, N_GROUPS * D_STATE)
        cm = xbc_ref[:, :, C_OFF:CONV_DIM].reshape(rows, N_GROUPS * D_STATE)
        dt_p = dt_ref[:, :, 0:DT_PAD].reshape(rows, DT_PAD)
        dt_e = dt_ref[:, :, DT_PAD:DT_PAD + D_INNER].reshape(rows, D_INNER)
        acp_ref[...] = cumsum_tokens(dt_p * apad_ref[...], tok_p)
        acum_e = cumsum_tokens(dt_e * ae_ref[...], tok)
        xdt = xs * dt_e
        w = xdt * jnp.exp(bcast_token(acum_e, SAMPLE_LEN - 1) - acum_e)
        wt_ref[...] = w.T.astype(BF16)
        ea_ref[...] = jnp.exp(acum_e)
        for s in range(SAMPLE_LEN):
            p_ref[s * rows:(s + 1) * rows, :] = (cm * bcast_token(bm, s)).astype(BF16)
        cb_sum = _dot(p_ref[...], gsum_ref[...])
        yd = de_ref[...] * xs
        for s in range(SAMPLE_LEN):
            cb_e = _dot(cb_sum[s * rows:(s + 1) * rows, :].astype(BF16), gexp_ref[...])
            diff = acum_e - bcast_token(acum_e, s)
            decay = jnp.exp(jnp.where(tok >= s, diff, -jnp.inf))
            yd = yd + cb_e * decay * bcast_token(xdt, s)
        yd_ref[...] = yd

    eye = (lax.broadcasted_iota(jnp.int32, (N_HEADS, DT_PAD), 0)
           == lax.broadcasted_iota(jnp.int32, (N_HEADS, DT_PAD), 1))
    rowid = lax.broadcasted_iota(jnp.int32, (rows, D_STATE), 0)
    for t in range(SCAN_STEP_NB):
        seq = j * SCAN_STEP_NB + t
        r0 = pl.multiple_of(seq * SAMPLE_LEN, SAMPLE_LEN)
        alast = acp_ref[pl.ds(r0 + SAMPLE_LEN - 1, 1), :]
        alast_col = jnp.sum(jnp.where(eye, jnp.broadcast_to(alast, (N_HEADS, DT_PAD)), 0.0),
                            axis=1, keepdims=True)
        cd_b = jnp.broadcast_to(jnp.exp(alast_col), (N_HEADS, D_STATE))
        mine = (rowid >= r0) & (rowid < r0 + SAMPLE_LEN)
        y_offs = []
        for g in range(N_GROUPS):
            gs = slice(g * GROUP_DIM, (g + 1) * GROUP_DIM)
            hg = h0_ref[t, gs, :]
            cg = xbc_ref[seq, :, C_OFF + g * D_STATE:C_OFF + (g + 1) * D_STATE].astype(BF16)
            y_offs.append(_dot_nt(cg, hg.astype(BF16)))
            b_all = xbc_ref[:, :, B_OFF + g * D_STATE:B_OFF + (g + 1) * D_STATE].reshape(
                rows, D_STATE)
            b_mine = jnp.where(mine, b_all, 0.0).astype(BF16)
            hout_ref[t, gs, :] = hg * _head_rows(cd_b, g) + _dot(wt_ref[gs, :], b_mine)
        y = yd_ref[pl.ds(r0, SAMPLE_LEN), :] + jnp.concatenate(y_offs, axis=1) * ea_ref[
            pl.ds(r0, SAMPLE_LEN), :]
        z = z_ref[seq]
        y_ref[seq] = jnp.concatenate(
            [_gate_norm(y[:, g * GROUP_DIM:(g + 1) * GROUP_DIM],
                        z[:, g * GROUP_DIM:(g + 1) * GROUP_DIM],
                        ng_ref[:, g * GROUP_DIM:(g + 1) * GROUP_DIM]) for g in range(N_GROUPS)],
            axis=1)


def _scan_sample(xbc, z, dt, h0, a_pad, a_e, d_e, ng, gsum, gexp):
    b = xbc.shape[0]
    dt_w = dt.shape[-1]
    blk = lambda w: pl.BlockSpec((SCAN_NB, SAMPLE_LEN, w), lambda i, j: (i, 0, 0))
    steps = SCAN_NB // SCAN_STEP_NB
    st = pl.BlockSpec((SCAN_STEP_NB, D_INNER, D_STATE), lambda i, j: (i * steps + j, 0, 0))
    return pl.pallas_call(
        _scan_sample_kernel,
        grid=(b // SCAN_NB, steps),
        in_specs=[blk(CONV_DIM), blk(D_INNER), blk(dt_w), st, _resident((1, DT_PAD)),
                  _resident((1, D_INNER)), _resident((1, D_INNER)), _resident((1, D_INNER)),
                  _resident(gsum.shape), _resident(gexp.shape)],
        out_specs=[blk(D_INNER), st],
        out_shape=[jax.ShapeDtypeStruct((b, SAMPLE_LEN, D_INNER), F32),
                   jax.ShapeDtypeStruct((b, D_INNER, D_STATE), F32)],
        scratch_shapes=[pltpu.VMEM((D_INNER, SCAN_ROWS), BF16),
                        pltpu.VMEM((SCAN_ROWS, D_INNER), F32),
                        pltpu.VMEM((SCAN_ROWS, D_INNER), F32),
                        pltpu.VMEM((SCAN_ROWS, DT_PAD), F32),
                        pltpu.VMEM((SAMPLE_LEN * SCAN_ROWS, N_GROUPS * D_STATE), BF16)],
        compiler_params=_params(2),
        name="ssd_scan_sample",
    )(xbc, z, dt, h0, a_pad, a_e, d_e, ng, gsum, gexp)


POOL_HALO = 2 * SUBLANES


def _pool_kernel(x_ref, g_ref, win_ref, wgrp_ref, scale_ref, wout_ref, buf0_ref,
                 o_ref, bufnew_ref, ext_ref, *, nb, tl, pos0):
    rows = nb * tl
    jt = pl.program_id(1)

    @pl.when(jt == 0)
    def _():
        ext_ref[:, 0:POOL_HALO, :] = jnp.zeros((nb, POOL_HALO, D_MODEL), F32)
        ext_ref[:, POOL_HALO - (MAX_WIN - 1):POOL_HALO, :] = buf0_ref[...]

    x = x_ref[...]
    h = _rmsnorm(x, g_ref[...]).astype(BF16)
    ext_ref[:, POOL_HALO:POOL_HALO + tl, :] = _dot(h, win_ref[...]).reshape(nb, tl, D_MODEL)
    nblk = tl // SUBLANES
    halo_blocks = POOL_HALO // SUBLANES
    shape4 = (nb, nblk, SUBLANES, POOL_GROUP_DIM)
    pos = (pos0 + jt * tl + SUBLANES * lax.broadcasted_iota(jnp.int32, shape4, 1)
           + lax.broadcasted_iota(jnp.int32, shape4, 2)).astype(F32)
    out = x
    for k, w in enumerate(POOL_WINDOWS):
        sl = slice(k * POOL_GROUP_DIM, (k + 1) * POOL_GROUP_DIM)
        ext = ext_ref[:, :, sl].reshape(nb, nblk + halo_blocks, SUBLANES, POOL_GROUP_DIM)
        tot, shift = ext, 1
        while shift < w:
            tot = tot + _shift_rows(tot, shift)
            shift *= 2
        u = ext[:, halo_blocks:]
        mean = tot[:, halo_blocks:] / jnp.minimum(jnp.float32(w), pos + 1.0)
        m = (mean - u).reshape(rows, POOL_GROUP_DIM).astype(BF16)
        mixed = (_dot(m, wgrp_ref[k]) * scale_ref[:, sl]).astype(BF16)
        out = out + _dot(mixed, wout_ref[sl, :])
    o_ref[...] = out
    bufnew_ref[...] = ext_ref[:, tl + 1:tl + POOL_HALO, :]
    ext_ref[:, 0:POOL_HALO, :] = ext_ref[:, tl:tl + POOL_HALO, :]


def _pool(x2d, row0, b, l, g, w_in, w_grp, scale, w_out, buf0, *, nb, tl, pos0):
    assert row0 % (nb * tl) == 0 and (nb == 1 or l == tl)
    tile0, tps = row0 // (nb * tl), l // tl
    return pl.pallas_call(
        functools.partial(_pool_kernel, nb=nb, tl=tl, pos0=pos0),
        grid=(b // nb, tps),
        in_specs=[pl.BlockSpec((nb * tl, D_MODEL), lambda i, j: (tile0 + i * tps + j, 0)),
                  _resident((1, D_MODEL)), _resident((D_MODEL, D_MODEL)),
                  _resident(w_grp.shape), _resident((1, D_MODEL)), _resident((D_MODEL, D_MODEL)),
                  pl.BlockSpec((nb, MAX_WIN - 1, D_MODEL), lambda i, j: (i, 0, 0))],
        out_specs=[pl.BlockSpec((nb * tl, D_MODEL), lambda i, j: (i * tps + j, 0)),
                   pl.BlockSpec((nb, MAX_WIN - 1, D_MODEL), lambda i, j: (i, 0, 0))],
        out_shape=[jax.ShapeDtypeStruct((b * l, D_MODEL), F32),
                   jax.ShapeDtypeStruct((b, MAX_WIN - 1, D_MODEL), F32)],
        scratch_shapes=[pltpu.VMEM((nb, POOL_HALO + tl, D_MODEL), F32)],
        compiler_params=_params(2),
        name="pool_mixer",
    )(x2d, g.reshape(1, D_MODEL), w_in, w_grp, scale.reshape(1, D_MODEL), w_out, buf0)


def _expand_heads(v):
    return jnp.repeat(v.astype(F32), HEAD_DIM).reshape(1, D_INNER)


def _pad_heads(v):
    return jnp.pad(v.astype(F32), (0, DT_PAD - N_HEADS)).reshape(1, DT_PAD)


def _group_sum_matrix():
    m = np.zeros((N_GROUPS * D_STATE, LANES), np.float32)
    for g in range(N_GROUPS):
        m[g * D_STATE:(g + 1) * D_STATE, g] = 1.0
    return jnp.asarray(m, BF16)


def _group_expand_matrix():
    m = np.zeros((LANES, D_INNER), np.float32)
    for g in range(N_GROUPS):
        m[g, g * GROUP_DIM:(g + 1) * GROUP_DIM] = 1.0
    return jnp.asarray(m, BF16)


def _trunks(x_prompt, x_sample, ssm0_s, conv0_s, pool0_s, p):
    bp, lp, _ = x_prompt.shape
    bs, ls, _ = x_sample.shape
    rows = (bp * lp, bs * ls)
    bf = lambda a: a.astype(BF16)
    w_gate, w_up, w_down = p["ffn_w_gate"], p["ffn_w_up"], p["ffn_w_down"]
    ffn = lambda xs, i, k, **kw: _ffn(xs, rows, p["ffn_norm"][i, k], w_gate, w_up, w_down, (i, k), **kw)
    x2d = ffn([x_prompt.reshape(rows[0], D_MODEL), x_sample.reshape(rows[1], D_MODEL)], 0, 0)

    w_in = p["ssd_w_in"][0]
    w_dt = w_in[:, D_INNER + CONV_DIM:]
    dt_bias = p["ssd_dt_bias"][0]
    wdt = jnp.pad(w_dt, ((0, 0), (0, DT_PAD - N_HEADS)))
    dtb = jnp.pad(dt_bias, (0, DT_PAD - N_HEADS))
    wdt_s = jnp.concatenate([wdt, jnp.repeat(w_dt, HEAD_DIM, axis=1)], axis=1)
    dtb_s = jnp.concatenate([dtb, jnp.repeat(dt_bias, HEAD_DIM)])
    a_neg = -jnp.exp(p["ssd_a_log"][0].astype(F32))
    a_pad, d_e = _pad_heads(a_neg), _expand_heads(p["ssd_d"][0])
    ng = p["ssd_norm"][0].reshape(1, D_INNER)
    common = (p["mix_norm"][0], bf(w_in))
    conv_w, conv_b = p["ssd_conv_w"][0], p["ssd_conv_b"][0]
    z, acum, tr, xbc, conv_p = _ssd_in_prompt(x2d, bp, lp, *common, bf(wdt), conv_w, conv_b, dtb,
                                              jnp.zeros((bp, CONV_W - 1, CONV_DIM), F32), a_pad)
    y_p, ssm_p = _scan_prompt(xbc, z, acum, tr, d_e, ng)
    z, xbc, dt, conv_s = _ssd_in(x2d, rows[0], bs, ls, *common, bf(wdt_s), conv_w, conv_b, dtb_s,
                                 conv0_s, nb=32, tl=SAMPLE_LEN)
    y_s, ssm_s = _scan_sample(xbc, z, dt, ssm0_s.reshape(bs, D_INNER, D_STATE), a_pad,
                              _expand_heads(a_neg), d_e, ng, _group_sum_matrix(),
                              _group_expand_matrix())
    x2d = ffn([x2d], 0, 1, pre=([y_p.reshape(rows[0], D_INNER), bf(y_s).reshape(rows[1], D_INNER)],
                                bf(p["ssd_w_out"][0])))

    x2d = ffn([x2d], 1, 0)
    pool_w = (p["mix_norm"][1], bf(p["pool_w_in"][0]), bf(p["pool_w_group"][0]), p["pool_scale"][0],
              bf(p["pool_w_out"][0]))
    xm_p, pool_p = _pool(x2d, 0, bp, lp, *pool_w, jnp.zeros((bp, MAX_WIN - 1, D_MODEL), F32),
                         nb=1, tl=512, pos0=0)
    xm_s, pool_s = _pool(x2d, rows[0], bs, ls, *pool_w, pool0_s, nb=64, tl=SAMPLE_LEN, pos0=PAST_LEN)
    out_p, out_s = ffn([xm_p, xm_s], 1, 1, final_g=p["final_norm"], split_out=True)
    state = lambda a, b: a.reshape(b, N_HEADS, HEAD_DIM, D_STATE)[None]
    return (out_p.reshape(bp, lp, D_MODEL), out_s.reshape(bs, ls, D_MODEL),
            state(ssm_p, bp), conv_p[None], pool_p[None], state(ssm_s, bs), conv_s[None], pool_s[None])


def kernel(x_prompt, x_sample, state_ssm, state_conv, state_pool, ffn_norm, ffn_w_gate, ffn_w_up,
           ffn_w_down, mix_norm, ssd_w_in, ssd_conv_w, ssd_conv_b, ssd_dt_bias, ssd_a_log, ssd_d,
           ssd_norm, ssd_w_out, pool_w_in, pool_w_group, pool_scale, pool_w_out, final_norm):
    p = dict(ffn_norm=ffn_norm, ffn_w_gate=ffn_w_gate, ffn_w_up=ffn_w_up, ffn_w_down=ffn_w_down,
             mix_norm=mix_norm, ssd_w_in=ssd_w_in, ssd_conv_w=ssd_conv_w, ssd_conv_b=ssd_conv_b,
             ssd_dt_bias=ssd_dt_bias, ssd_a_log=ssd_a_log, ssd_d=ssd_d, ssd_norm=ssd_norm,
             ssd_w_out=ssd_w_out, pool_w_in=pool_w_in, pool_w_group=pool_w_group,
             pool_scale=pool_scale, pool_w_out=pool_w_out, final_norm=final_norm)
    return _trunks(x_prompt, x_sample, state_ssm[0], state_conv[0], state_pool[0], p)
```

```python
import functools

import numpy as np
import jax
import jax.numpy as jnp
from jax import lax
from jax.experimental import pallas as pl
from jax.experimental.pallas import tpu as pltpu

F32 = jnp.float32
BF16 = jnp.bfloat16

EPS = 1e-6
D_MODEL = 1024
D_FF = 2816
D_INNER = 2048
HEAD_DIM = 64
N_HEADS = 32
N_GROUPS = 8
HEADS_PER_GROUP = 4
GROUP_DIM = HEADS_PER_GROUP * HEAD_DIM
D_STATE = 128
CONV_W = 4
CONV_DIM = D_INNER + 2 * N_GROUPS * D_STATE
CHUNK = 128
POOL_WINDOWS = (2, 4, 8, 16)
POOL_GROUP_DIM = 256
MAX_WIN = 16
PAST_LEN = 16384
LANES = 128
SUBLANES = 8
DT_PAD = LANES
VMEM_LIMIT = 60 * 1024 * 1024

NT_DIMS = (((1,), (1,)), ((), ()))
LOG2E = 1.4426950408889634


def _resident(shape, lead=()):
    nd = len(shape)
    return pl.BlockSpec((None,) * len(lead) + tuple(shape), lambda *_: tuple(lead) + (0,) * nd,
                        pipeline_mode=pl.Buffered(1))


def _params(n_axes):
    return pltpu.CompilerParams(dimension_semantics=("arbitrary",) * n_axes,
                                vmem_limit_bytes=VMEM_LIMIT)


def _rmsnorm(x, g):
    ms = jnp.mean(x * x, axis=-1, keepdims=True)
    return x * lax.rsqrt(ms + EPS) * g


def _silu(x):
    return x * jax.nn.sigmoid(x)


def _softplus(x):
    return jnp.maximum(x, 0.0) + jnp.log(1.0 + jnp.exp(-jnp.abs(x)))


def _dot(a, b):
    return jnp.dot(a, b, preferred_element_type=F32)


def _dot_nt(a, b):
    return lax.dot_general(a, b, NT_DIMS, preferred_element_type=F32)


def _dot_exact01(t, x):
    hi = x.astype(BF16)
    r1 = x - hi.astype(F32)
    mid = r1.astype(BF16)
    lo = (r1 - mid.astype(F32)).astype(BF16)
    return _dot(t, hi) + _dot(t, mid) + _dot(t, lo)


FFN_TM = 512
FFN_TF = 256


def _ffn_kernel(*refs, n_x, n_y, final, n_out, tiles_a):
    refs = list(refs)
    x_refs = [refs.pop(0) for _ in range(n_x)]
    y_refs = [refs.pop(0) for _ in range(n_y)]
    wpre_ref = refs.pop(0) if n_y else None
    g_ref, wg_ref, wu_ref, wd_ref = refs[:4]
    del refs[:4]
    fg_ref = refs.pop(0) if final else None
    o_refs = [refs.pop(0) for _ in range(n_out)]
    h_ref, a_ref = refs
    second = pl.program_id(0) >= tiles_a

    def pick(rs, dtype):
        if len(rs) == 1:
            return rs[0][...].astype(dtype)
        return jnp.where(second, rs[1][...].astype(dtype), rs[0][...].astype(dtype))

    x = pick(x_refs, F32)
    if n_y:
        x = x + _dot(pick(y_refs, BF16), wpre_ref[...])
    h_ref[...] = _rmsnorm(x, g_ref[...]).astype(BF16)
    for f in range(0, D_FF, FFN_TF):
        h = h_ref[...]
        gate = _dot(h, wg_ref[:, f:f + FFN_TF].astype(BF16))
        up = _dot(h, wu_ref[:, f:f + FFN_TF].astype(BF16))
        a_ref[:, f:f + FFN_TF] = (_silu(gate) * up).astype(BF16)
    y = x + 0.5 * _dot(a_ref[...], wd_ref[...].astype(BF16))
    if final:
        y = _rmsnorm(y, fg_ref[...])
    if n_out == 1:
        o_refs[0][...] = y
    else:
        @pl.when(jnp.logical_not(second))
        def _():
            o_refs[0][...] = y

        @pl.when(second)
        def _():
            o_refs[1][...] = y


def _ffn(xs, rows, g, w_gate, w_up, w_down, idx, final_g=None, pre=None, split_out=False):
    tm = FFN_TM
    tiles_a, tiles_b = rows[0] // tm, rows[1] // tm
    assert rows[0] % tm == 0 and rows[1] % tm == 0
    final = final_g is not None

    def row_specs(arrs):
        if len(arrs) == 1:
            return [pl.BlockSpec((tm, arrs[0].shape[1]), lambda i: (i, 0))]
        return [pl.BlockSpec((tm, arrs[0].shape[1]), lambda i: (jnp.minimum(i, tiles_a - 1), 0)),
                pl.BlockSpec((tm, arrs[1].shape[1]), lambda i: (jnp.maximum(i - tiles_a, 0), 0))]

    in_specs, args = row_specs(xs), list(xs)
    ys = []
    if pre is not None:
        ys, wpre = pre
        in_specs += row_specs(ys) + [_resident(wpre.shape)]
        args += list(ys) + [wpre]
    in_specs += [_resident((1, D_MODEL)), _resident((D_MODEL, D_FF), idx),
                 _resident((D_MODEL, D_FF), idx), _resident((D_FF, D_MODEL), idx)]
    args += [g.reshape(1, D_MODEL), w_gate, w_up, w_down]
    if final:
        in_specs.append(_resident((1, D_MODEL)))
        args.append(final_g.reshape(1, D_MODEL))
    if split_out:
        outs = [jax.ShapeDtypeStruct((r, D_MODEL), F32) for r in rows]
    else:
        outs = [jax.ShapeDtypeStruct((rows[0] + rows[1], D_MODEL), F32)]
    res = pl.pallas_call(
        functools.partial(_ffn_kernel, n_x=len(xs), n_y=len(ys), final=final, n_out=len(outs),
                          tiles_a=tiles_a),
        grid=(tiles_a + tiles_b,),
        in_specs=in_specs,
        out_specs=row_specs(outs),
        out_shape=outs,
        scratch_shapes=[pltpu.VMEM((tm, D_MODEL), BF16), pltpu.VMEM((tm, D_FF), BF16)],
        compiler_params=_params(1),
        name="ffn" + ("_pre" if pre is not None else "") + ("_final" if final else ""),
    )(*args)
    return res if split_out else res[0]


CONV_HALO = SUBLANES
SSD_IN_LANE_CHUNK = 512


def _shift_rows(v, k):
    if k == SUBLANES:
        return jnp.concatenate([v[:, :1], v[:, :-1]], axis=1)
    r = pltpu.roll(v, k, 2)
    prev = jnp.concatenate([r[:, :1], r[:, :-1]], axis=1)
    sub = lax.broadcasted_iota(jnp.int32, v.shape, 2)
    return jnp.where(sub < k, prev, r)


def _chunk_decays(dt, a_pad):
    q = dt.shape[0]
    tril = (lax.broadcasted_iota(jnp.int32, (q, q), 0) >= lax.broadcasted_iota(jnp.int32, (q, q), 1))
    acum = _dot_exact01(tril.astype(BF16), dt * a_pad) * LOG2E
    acum_t = acum.T
    dt_t = dt.T
    src_t = acum_t - jnp.log2(dt_t)
    w_t = jnp.exp2(acum_t[:, q - 1:q] - acum_t) * dt_t
    return acum, acum_t, src_t, w_t


def _conv_silu(ext, cw_ref, cb_ref, sl):
    w = ext.shape[-1]
    tap = lambda k: cw_ref[k:k + 1, sl].reshape(1, 1, 1, w)
    ext1 = _shift_rows(ext, 1)
    p = ext * tap(3) + ext1 * tap(2)
    q = ext * tap(1) + ext1 * tap(0)
    return _silu((cb_ref[:, sl].reshape(1, 1, 1, w) + p + _shift_rows(q, 2))[:, 1:])


def _ssd_in_kernel(x_ref, g_ref, win_ref, wdt_ref, cw_ref, cb_ref, dtb_ref, conv0_ref,
                   z_ref, xbc_ref, dt_ref, convnew_ref, halo_ref, h_ref, *, nb, tl):
    rows = nb * tl
    wc = SSD_IN_LANE_CHUNK
    nblk = tl // SUBLANES

    @pl.when(pl.program_id(1) == 0)
    def _():
        halo_ref[...] = jnp.zeros_like(halo_ref)
        halo_ref[:, CONV_HALO - (CONV_W - 1):, :] = conv0_ref[...]

    h_ref[...] = _rmsnorm(x_ref[...], g_ref[...]).astype(BF16)
    dt = _softplus(_dot(h_ref[...], wdt_ref[...]) + dtb_ref[...])
    dt_ref[...] = dt.reshape(nb, tl, dt.shape[-1])
    for c in range(0, CONV_DIM, wc):
        sl = slice(c, c + wc)
        zs = slice(c // 2, c // 2 + wc // 2)
        z_ref[:, :, zs] = _dot(h_ref[...], win_ref[:, zs]).reshape(nb, tl, wc // 2)
        cur = _dot(h_ref[...], win_ref[:, D_INNER + c:D_INNER + c + wc]).reshape(nb, tl, wc)
        ext = jnp.concatenate([halo_ref[:, :, sl], cur], axis=1).reshape(nb, nblk + 1, SUBLANES, wc)
        xbc_ref[:, :, sl] = _conv_silu(ext, cw_ref, cb_ref, sl).reshape(nb, tl, wc)
        halo_ref[:, :, sl] = cur[:, tl - CONV_HALO:, :]
    convnew_ref[...] = halo_ref[:, CONV_HALO - (CONV_W - 1):, :]


def _ssd_in(x2d, row0, b, l, g, w_in, wdt, cw, cb, dtb, conv0, *, nb, tl):
    assert l == tl and row0 % (nb * tl) == 0
    tile0 = row0 // (nb * tl)
    dt_w = wdt.shape[1]
    blk = lambda w: pl.BlockSpec((nb, tl, w), lambda i, j: (i, j, 0))
    return pl.pallas_call(
        functools.partial(_ssd_in_kernel, nb=nb, tl=tl),
        grid=(b // nb, l // tl),
        in_specs=[pl.BlockSpec((nb * tl, D_MODEL), lambda i, j: (tile0 + i, 0)),
                  _resident((1, D_MODEL)), _resident(w_in.shape), _resident((D_MODEL, dt_w)),
                  _resident((CONV_W, CONV_DIM)), _resident((1, CONV_DIM)), _resident((1, dt_w)),
                  pl.BlockSpec((nb, CONV_W - 1, CONV_DIM), lambda i, j: (i, 0, 0))],
        out_specs=[blk(D_INNER), blk(CONV_DIM), blk(dt_w),
                   pl.BlockSpec((nb, CONV_W - 1, CONV_DIM), lambda i, j: (i, 0, 0))],
        out_shape=[jax.ShapeDtypeStruct((b, l, D_INNER), F32),
                   jax.ShapeDtypeStruct((b, l, CONV_DIM), F32),
                   jax.ShapeDtypeStruct((b, l, dt_w), F32),
                   jax.ShapeDtypeStruct((b, CONV_W - 1, CONV_DIM), F32)],
        scratch_shapes=[pltpu.VMEM((nb, CONV_HALO, CONV_DIM), F32),
                        pltpu.VMEM((nb * tl, D_MODEL), BF16)],
        compiler_params=_params(2),
        name="ssd_in",
    )(x2d, g.reshape(1, D_MODEL), w_in, wdt, cw, cb.reshape(1, CONV_DIM),
      dtb.reshape(1, dt_w), conv0)


SSD_PIPE_TL = 512


def _ssd_in_prompt_kernel(x_ref, g_ref, win_ref, wdt_ref, cw_ref, cb_ref, dtb_ref, conv0_ref,
                          apad_ref, z_ref, acum_ref, tr_ref, xbc_ref, convnew_ref, halo_ref, h_ref):
    tl = SSD_PIPE_TL
    wc = SSD_IN_LANE_CHUNK
    nblk = tl // SUBLANES

    @pl.when(pl.program_id(1) == 0)
    def _():
        halo_ref[...] = jnp.zeros_like(halo_ref)
        halo_ref[CONV_HALO - (CONV_W - 1):, :] = conv0_ref[0]

    h_ref[...] = _rmsnorm(x_ref[...], g_ref[...]).astype(BF16)
    dt = _softplus(_dot(h_ref[...], wdt_ref[...]) + dtb_ref[...])
    for j in range(tl // CHUNK):
        cs = slice(j * CHUNK, (j + 1) * CHUNK)
        acum, acum_t, src_t, w_t = _chunk_decays(dt[cs, :], apad_ref[...])
        acum_ref[0, cs, :] = acum
        tr_ref[0, j, 0] = acum_t
        tr_ref[0, j, 1] = src_t
        tr_ref[0, j, 2] = w_t
    for c in range(0, CONV_DIM, wc):
        sl = slice(c, c + wc)
        zs = slice(c // 2, c // 2 + wc // 2)
        z_ref[0, :, zs] = _dot(h_ref[...], win_ref[:, zs]).astype(z_ref.dtype)
        cur = _dot(h_ref[...], win_ref[:, D_INNER + c:D_INNER + c + wc])
        ext = jnp.concatenate([halo_ref[:, sl], cur], axis=0).reshape(1, nblk + 1, SUBLANES, wc)
        xbc_ref[0, :, sl] = _conv_silu(ext, cw_ref, cb_ref, sl).reshape(tl, wc)
        halo_ref[:, sl] = cur[tl - CONV_HALO:, :]
    convnew_ref[0] = halo_ref[CONV_HALO - (CONV_W - 1):, :]


def _ssd_in_prompt(x2d, b, l, g, w_in, wdt, cw, cb, dtb, conv0, a_pad):
    tl = SSD_PIPE_TL
    tps = l // tl
    ncs = tl // CHUNK
    blk = lambda w: pl.BlockSpec((1, tl, w), lambda i, j: (i, j, 0))
    return pl.pallas_call(
        _ssd_in_prompt_kernel,
        grid=(b, l // tl),
        in_specs=[pl.BlockSpec((tl, D_MODEL), lambda i, j: (i * tps + j, 0)),
                  _resident((1, D_MODEL)), _resident(w_in.shape), _resident((D_MODEL, DT_PAD)),
                  _resident((CONV_W, CONV_DIM)), _resident((1, CONV_DIM)), _resident((1, DT_PAD)),
                  pl.BlockSpec((1, CONV_W - 1, CONV_DIM), lambda i, j: (i, 0, 0)),
                  _resident((1, DT_PAD))],
        out_specs=[blk(D_INNER), blk(DT_PAD),
                   pl.BlockSpec((1, ncs, 3, DT_PAD, CHUNK), lambda i, j: (i, j, 0, 0, 0)),
                   blk(CONV_DIM),
                   pl.BlockSpec((1, CONV_W - 1, CONV_DIM), lambda i, j: (i, 0, 0))],
        out_shape=[jax.ShapeDtypeStruct((b, l, D_INNER), BF16),
                   jax.ShapeDtypeStruct((b, l, DT_PAD), F32),
                   jax.ShapeDtypeStruct((b, l // CHUNK, 3, DT_PAD, CHUNK), F32),
                   jax.ShapeDtypeStruct((b, l, CONV_DIM), F32),
                   jax.ShapeDtypeStruct((b, CONV_W - 1, CONV_DIM), F32)],
        scratch_shapes=[pltpu.VMEM((CONV_HALO, CONV_DIM), F32), pltpu.VMEM((tl, D_MODEL), BF16)],
        compiler_params=_params(2),
        name="ssd_in_prompt",
    )(x2d, g.reshape(1, D_MODEL), w_in, wdt, cw, cb.reshape(1, CONV_DIM),
      dtb.reshape(1, DT_PAD), conv0, a_pad)


B_OFF = D_INNER
C_OFF = D_INNER + N_GROUPS * D_STATE


def _gate_norm(y, z, ng):
    yg = y * _silu(z)
    ms = jnp.mean(yg * yg, axis=-1, keepdims=True)
    return yg * lax.rsqrt(ms + EPS) * ng


def _head_rows(mat, g):
    return jnp.concatenate(
        [jnp.broadcast_to(mat[g * HEADS_PER_GROUP + r:g * HEADS_PER_GROUP + r + 1, :],
                          (HEAD_DIM, mat.shape[1])) for r in range(HEADS_PER_GROUP)], axis=0)


def _scan_prompt_kernel(xbc_ref, z_ref, acum_ref, tr_ref, de_ref, ng_ref, y_ref, hout_ref, *h_refs,
                        chunks):
    q = CHUNK
    step = pl.program_id(1)

    @pl.when(step == 0)
    def _():
        for h_ref in h_refs:
            h_ref[...] = jnp.zeros_like(h_ref)

    tril = (lax.broadcasted_iota(jnp.int32, (q, q), 0) >= lax.broadcasted_iota(jnp.int32, (q, q), 1))
    lane_g = lax.broadcasted_iota(jnp.int32, (q, GROUP_DIM), 1)
    lane_half = lax.broadcasted_iota(jnp.int32, (q, LANES), 1) < HEAD_DIM

    def chunk_body(ci, carry):
        ts = pl.ds(pl.multiple_of(ci * q, q), q)
        acum = acum_ref[0, ts, :]
        acum_t, src_t, w_t = tr_ref[0, ci, 0], tr_ref[0, ci, 1], tr_ref[0, ci, 2]
        cd_b = jnp.broadcast_to(jnp.exp2(acum_t[:, q - 1:q]), (DT_PAD, D_STATE))
        for g in range(N_GROUPS):
            gs = slice(g * GROUP_DIM, (g + 1) * GROUP_DIM)
            bg = xbc_ref[0, ts, B_OFF + g * D_STATE:B_OFF + (g + 1) * D_STATE].astype(BF16)
            cg = xbc_ref[0, ts, C_OFF + g * D_STATE:C_OFF + (g + 1) * D_STATE].astype(BF16)
            xg = xbc_ref[0, ts, gs]
            xgb = xg.astype(BF16)
            cb = _dot_nt(cg, bg)
            ms, cols = [], []
            for r in range(HEADS_PER_GROUP):
                hd = g * HEADS_PER_GROUP + r
                colf = jnp.broadcast_to(acum[:, hd:hd + 1], (q, q))
                rowf = jnp.broadcast_to(src_t[hd:hd + 1, :], (q, q))
                ms.append((cb * jnp.exp2(jnp.where(tril, colf - rowf, -jnp.inf))).astype(BF16))
                cols.append(colf)
            zero = jnp.zeros_like(xgb)
            y_diag = sum(
                _dot(ms[r], jnp.where((lane_g >= r * HEAD_DIM) & (lane_g < (r + 1) * HEAD_DIM),
                                      xgb, zero)) for r in range(HEADS_PER_GROUP))
            hg = h_refs[g][...]
            y_off = _dot_nt(cg, hg.astype(BF16))
            acum_e = jnp.concatenate([jnp.where(lane_half, cols[0], cols[1]),
                                      jnp.where(lane_half, cols[2], cols[3])], axis=1)
            y = y_diag + y_off * jnp.exp2(acum_e) + de_ref[:, gs] * xg
            xg_t = xg.T
            wt = jnp.concatenate(
                [xg_t[r * HEAD_DIM:(r + 1) * HEAD_DIM, :] * w_t[g * HEADS_PER_GROUP + r:
                                                                g * HEADS_PER_GROUP + r + 1, :]
                 for r in range(HEADS_PER_GROUP)], axis=0).astype(BF16)
            h_refs[g][...] = hg * _head_rows(cd_b, g) + _dot(wt, bg)
            zg = z_ref[0, ts, gs].astype(F32)
            y_ref[0, ts, gs] = _gate_norm(y, zg, ng_ref[:, gs]).astype(y_ref.dtype)
        return carry

    lax.fori_loop(0, chunks, chunk_body, 0)

    @pl.when(step == pl.num_programs(1) - 1)
    def _():
        for g, h_ref in enumerate(h_refs):
            hout_ref[0, g * GROUP_DIM:(g + 1) * GROUP_DIM, :] = h_ref[...]


SCAN_CHUNKS_PER_STEP = 4


def _scan_prompt(xbc, z, acum, tr, d_e, ng):
    b, l, _ = xbc.shape
    chunks = SCAN_CHUNKS_PER_STEP
    tl = chunks * CHUNK
    blk = lambda w: pl.BlockSpec((1, tl, w), lambda i, j: (i, j, 0))
    return pl.pallas_call(
        functools.partial(_scan_prompt_kernel, chunks=chunks),
        grid=(b, l // tl),
        in_specs=[blk(CONV_DIM), blk(D_INNER), blk(DT_PAD),
                  pl.BlockSpec((1, chunks, 3, DT_PAD, CHUNK), lambda i, j: (i, j, 0, 0, 0)),
                  _resident((1, D_INNER)), _resident((1, D_INNER))],
        out_specs=[blk(D_INNER), pl.BlockSpec((1, D_INNER, D_STATE), lambda i, j: (i, 0, 0))],
        out_shape=[jax.ShapeDtypeStruct((b, l, D_INNER), BF16),
                   jax.ShapeDtypeStruct((b, D_INNER, D_STATE), F32)],
        scratch_shapes=[pltpu.VMEM((GROUP_DIM, D_STATE), F32)] * N_GROUPS,
        compiler_params=_params(2),
        name="ssd_scan_prompt",
    )(xbc, z, acum, tr, d_e, ng)


SAMPLE_LEN = 8
SCAN_NB = 16
SCAN_ROWS = SCAN_NB * SAMPLE_LEN
SCAN_STEP_NB = 8


def _scan_sample_kernel(xbc_ref, z_ref, dt_ref, h0_ref, apad_ref, ae_ref, de_ref, ng_ref,
                        gsum_ref, gexp_ref, y_ref, hout_ref,
                        wt_ref, ea_ref, ac_ref, xdt_ref, acp_ref, p_ref):
    rows = SCAN_ROWS
    srows = SCAN_STEP_NB * SAMPLE_LEN
    j = pl.program_id(1)

    def bcast_token(v, s):
        n, w = v.shape[0] // SAMPLE_LEN, v.shape[-1]
        v3 = v.reshape(n, SAMPLE_LEN, w)
        return jnp.broadcast_to(v3[:, s:s + 1, :], (n, SAMPLE_LEN, w)).reshape(n * SAMPLE_LEN, w)

    @pl.when(j == 0)
    def _():
        tok = lax.broadcasted_iota(jnp.int32, (rows, D_INNER), 0) & (SAMPLE_LEN - 1)
        tok_p = lax.broadcasted_iota(jnp.int32, (rows, DT_PAD), 0) & (SAMPLE_LEN - 1)

        def cumsum_tokens(v, t):
            for sh in (1, 2, 4):
                v = v + jnp.where(t >= sh, pltpu.roll(v, sh, 0), 0.0)
            return v

        xs = xbc_ref[:, :, 0:D_INNER].reshape(rows, D_INNER)
        dt_p = dt_ref[:, :, 0:DT_PAD].reshape(rows, DT_PAD)
        dt_e = dt_ref[:, :, DT_PAD:DT_PAD + D_INNER].reshape(rows, D_INNER)
        acp_ref[...] = cumsum_tokens(dt_p * apad_ref[...], tok_p)
        acum_e = cumsum_tokens(dt_e * ae_ref[...], tok)
        xdt = xs * dt_e
        w = xdt * jnp.exp(bcast_token(acum_e, SAMPLE_LEN - 1) - acum_e)
        wt_ref[...] = w.T.astype(BF16)
        ea_ref[...] = jnp.exp(acum_e)
        ac_ref[...] = acum_e
        xdt_ref[...] = xdt

    seq0 = j * SCAN_STEP_NB
    step_rows = pl.ds(pl.multiple_of(seq0 * SAMPLE_LEN, srows), srows)
    step_seqs = pl.ds(seq0, SCAN_STEP_NB)
    tok = lax.broadcasted_iota(jnp.int32, (srows, D_INNER), 0) & (SAMPLE_LEN - 1)
    xs = xbc_ref[step_seqs, :, 0:D_INNER].reshape(srows, D_INNER)
    bm = xbc_ref[step_seqs, :, B_OFF:C_OFF].reshape(srows, N_GROUPS * D_STATE)
    cm = xbc_ref[step_seqs, :, C_OFF:CONV_DIM].reshape(srows, N_GROUPS * D_STATE)
    acum_e = ac_ref[step_rows, :]
    xdt = xdt_ref[step_rows, :]
    for s in range(SAMPLE_LEN):
        p_ref[s * srows:(s + 1) * srows, :] = (cm * bcast_token(bm, s)).astype(BF16)
    cb_sum = _dot(p_ref[...], gsum_ref[...])
    cb_e = _dot(cb_sum.astype(BF16), gexp_ref[...])
    yd = de_ref[...] * xs
    for s in range(SAMPLE_LEN):
        diff = acum_e - bcast_token(acum_e, s)
        decay = jnp.exp(jnp.where(tok >= s, diff, -jnp.inf))
        yd = yd + cb_e[s * srows:(s + 1) * srows, :] * decay * bcast_token(xdt, s)

    eye = (lax.broadcasted_iota(jnp.int32, (N_HEADS, DT_PAD), 0)
           == lax.broadcasted_iota(jnp.int32, (N_HEADS, DT_PAD), 1))
    rowid = lax.broadcasted_iota(jnp.int32, (rows, D_STATE), 0)
    for t in range(SCAN_STEP_NB):
        seq = seq0 + t
        r0 = pl.multiple_of(seq * SAMPLE_LEN, SAMPLE_LEN)
        alast = acp_ref[pl.ds(r0 + SAMPLE_LEN - 1, 1), :]
        alast_col = jnp.sum(jnp.where(eye, jnp.broadcast_to(alast, (N_HEADS, DT_PAD)), 0.0),
                            axis=1, keepdims=True)
        cd_b = jnp.broadcast_to(jnp.exp(alast_col), (N_HEADS, D_STATE))
        mine = (rowid >= r0) & (rowid < r0 + SAMPLE_LEN)
        y_offs = []
        for g in range(N_GROUPS):
            gs = slice(g * GROUP_DIM, (g + 1) * GROUP_DIM)
            hg = h0_ref[t, gs, :]
            cg = xbc_ref[seq, :, C_OFF + g * D_STATE:C_OFF + (g + 1) * D_STATE].astype(BF16)
            y_offs.append(_dot_nt(cg, hg.astype(BF16)))
            b_all = xbc_ref[:, :, B_OFF + g * D_STATE:B_OFF + (g + 1) * D_STATE].reshape(
                rows, D_STATE)
            b_mine = jnp.where(mine, b_all, 0.0).astype(BF16)
            hout_ref[t, gs, :] = hg * _head_rows(cd_b, g) + _dot(wt_ref[gs, :], b_mine)
        y = yd[t * SAMPLE_LEN:(t + 1) * SAMPLE_LEN, :] + jnp.concatenate(y_offs, axis=1) * ea_ref[
            pl.ds(r0, SAMPLE_LEN), :]
        z = z_ref[seq]
        y_ref[seq] = jnp.concatenate(
            [_gate_norm(y[:, g * GROUP_DIM:(g + 1) * GROUP_DIM],
                        z[:, g * GROUP_DIM:(g + 1) * GROUP_DIM],
                        ng_ref[:, g * GROUP_DIM:(g + 1) * GROUP_DIM]) for g in range(N_GROUPS)],
            axis=1)


def _scan_sample(xbc, z, dt, h0, a_pad, a_e, d_e, ng, gsum, gexp):
    b = xbc.shape[0]
    dt_w = dt.shape[-1]
    blk = lambda w: pl.BlockSpec((SCAN_NB, SAMPLE_LEN, w), lambda i, j: (i, 0, 0))
    steps = SCAN_NB // SCAN_STEP_NB
    st = pl.BlockSpec((SCAN_STEP_NB, D_INNER, D_STATE), lambda i, j: (i * steps + j, 0, 0))
    return pl.pallas_call(
        _scan_sample_kernel,
        grid=(b // SCAN_NB, steps),
        in_specs=[blk(CONV_DIM), blk(D_INNER), blk(dt_w), st, _resident((1, DT_PAD)),
                  _resident((1, D_INNER)), _resident((1, D_INNER)), _resident((1, D_INNER)),
                  _resident(gsum.shape), _resident(gexp.shape)],
        out_specs=[blk(D_INNER), st],
        out_shape=[jax.ShapeDtypeStruct((b, SAMPLE_LEN, D_INNER), F32),
                   jax.ShapeDtypeStruct((b, D_INNER, D_STATE), F32)],
        scratch_shapes=[pltpu.VMEM((D_INNER, SCAN_ROWS), BF16),
                        pltpu.VMEM((SCAN_ROWS, D_INNER), F32),
                        pltpu.VMEM((SCAN_ROWS, D_INNER), F32),
                        pltpu.VMEM((SCAN_ROWS, D_INNER), F32),
                        pltpu.VMEM((SCAN_ROWS, DT_PAD), F32),
                        pltpu.VMEM((SAMPLE_LEN * SCAN_STEP_NB * SAMPLE_LEN, N_GROUPS * D_STATE),
                                   BF16)],
        compiler_params=_params(2),
        name="ssd_scan_sample",
    )(xbc, z, dt, h0, a_pad, a_e, d_e, ng, gsum, gexp)


POOL_HALO = 2 * SUBLANES


def _pool_kernel(x_ref, g_ref, win_ref, wgrp_ref, scale_ref, wout_ref, buf0_ref,
                 o_ref, bufnew_ref, ext_ref, *, nb, tl, pos0):
    rows = nb * tl
    jt = pl.program_id(1)

    @pl.when(jt == 0)
    def _():
        ext_ref[:, 0:POOL_HALO, :] = jnp.zeros((nb, POOL_HALO, D_MODEL), F32)
        ext_ref[:, POOL_HALO - (MAX_WIN - 1):POOL_HALO, :] = buf0_ref[...]

    x = x_ref[...]
    h = _rmsnorm(x, g_ref[...]).astype(BF16)
    ext_ref[:, POOL_HALO:POOL_HALO + tl, :] = _dot(h, win_ref[...]).reshape(nb, tl, D_MODEL)
    nblk = tl // SUBLANES
    halo_blocks = POOL_HALO // SUBLANES
    shape4 = (nb, nblk, SUBLANES, POOL_GROUP_DIM)
    pos = (pos0 + jt * tl + SUBLANES * lax.broadcasted_iota(jnp.int32, shape4, 1)
           + lax.broadcasted_iota(jnp.int32, shape4, 2)).astype(F32)
    out = x
    for k, w in enumerate(POOL_WINDOWS):
        sl = slice(k * POOL_GROUP_DIM, (k + 1) * POOL_GROUP_DIM)
        ext = ext_ref[:, :, sl].reshape(nb, nblk + halo_blocks, SUBLANES, POOL_GROUP_DIM)
        tot, shift = ext, 1
        while shift < w:
            tot = tot + _shift_rows(tot, shift)
            shift *= 2
        u = ext[:, halo_blocks:]
        mean = tot[:, halo_blocks:] / jnp.minimum(jnp.float32(w), pos + 1.0)
        m = (mean - u).reshape(rows, POOL_GROUP_DIM).astype(BF16)
        mixed = (_dot(m, wgrp_ref[k]) * scale_ref[:, sl]).astype(BF16)
        out = out + _dot(mixed, wout_ref[sl, :])
    o_ref[...] = out
    bufnew_ref[...] = ext_ref[:, tl + 1:tl + POOL_HALO, :]
    ext_ref[:, 0:POOL_HALO, :] = ext_ref[:, tl:tl + POOL_HALO, :]


def _pool(x2d, row0, b, l, g, w_in, w_grp, scale, w_out, buf0, *, nb, tl, pos0):
    assert row0 % (nb * tl) == 0 and (nb == 1 or l == tl)
    tile0, tps = row0 // (nb * tl), l // tl
    return pl.pallas_call(
        functools.partial(_pool_kernel, nb=nb, tl=tl, pos0=pos0),
        grid=(b // nb, tps),
        in_specs=[pl.BlockSpec((nb * tl, D_MODEL), lambda i, j: (tile0 + i * tps + j, 0)),
                  _resident((1, D_MODEL)), _resident((D_MODEL, D_MODEL)),
                  _resident(w_grp.shape), _resident((1, D_MODEL)), _resident((D_MODEL, D_MODEL)),
                  pl.BlockSpec((nb, MAX_WIN - 1, D_MODEL), lambda i, j: (i, 0, 0))],
        out_specs=[pl.BlockSpec((nb * tl, D_MODEL), lambda i, j: (i * tps + j, 0)),
                   pl.BlockSpec((nb, MAX_WIN - 1, D_MODEL), lambda i, j: (i, 0, 0))],
        out_shape=[jax.ShapeDtypeStruct((b * l, D_MODEL), F32),
                   jax.ShapeDtypeStruct((b, MAX_WIN - 1, D_MODEL), F32)],
        scratch_shapes=[pltpu.VMEM((nb, POOL_HALO + tl, D_MODEL), F32)],
        compiler_params=_params(2),
        name="pool_mixer",
    )(x2d, g.reshape(1, D_MODEL), w_in, w_grp, scale.reshape(1, D_MODEL), w_out, buf0)


def _expand_heads(v):
    return jnp.repeat(v.astype(F32), HEAD_DIM).reshape(1, D_INNER)


def _pad_heads(v):
    return jnp.pad(v.astype(F32), (0, DT_PAD - N_HEADS)).reshape(1, DT_PAD)


def _group_sum_matrix():
    m = np.zeros((N_GROUPS * D_STATE, LANES), np.float32)
    for g in range(N_GROUPS):
        m[g * D_STATE:(g + 1) * D_STATE, g] = 1.0
    return jnp.asarray(m, BF16)


def _group_expand_matrix():
    m = np.zeros((LANES, D_INNER), np.float32)
    for g in range(N_GROUPS):
        m[g, g * GROUP_DIM:(g + 1) * GROUP_DIM] = 1.0
    return jnp.asarray(m, BF16)


def _trunks(x_prompt, x_sample, ssm0_s, conv0_s, pool0_s, p):
    bp, lp, _ = x_prompt.shape
    bs, ls, _ = x_sample.shape
    rows = (bp * lp, bs * ls)
    bf = lambda a: a.astype(BF16)
    w_gate, w_up, w_down = p["ffn_w_gate"], p["ffn_w_up"], p["ffn_w_down"]
    ffn = lambda xs, i, k, **kw: _ffn(xs, rows, p["ffn_norm"][i, k], w_gate, w_up, w_down, (i, k), **kw)
    x2d = ffn([x_prompt.reshape(rows[0], D_MODEL), x_sample.reshape(rows[1], D_MODEL)], 0, 0)

    w_in = p["ssd_w_in"][0]
    w_dt = w_in[:, D_INNER + CONV_DIM:]
    dt_bias = p["ssd_dt_bias"][0]
    wdt = jnp.pad(w_dt, ((0, 0), (0, DT_PAD - N_HEADS)))
    dtb = jnp.pad(dt_bias, (0, DT_PAD - N_HEADS))
    wdt_s = jnp.concatenate([wdt, jnp.repeat(w_dt, HEAD_DIM, axis=1)], axis=1)
    dtb_s = jnp.concatenate([dtb, jnp.repeat(dt_bias, HEAD_DIM)])
    a_neg = -jnp.exp(p["ssd_a_log"][0].astype(F32))
    a_pad, d_e = _pad_heads(a_neg), _expand_heads(p["ssd_d"][0])
    ng = p["ssd_norm"][0].reshape(1, D_INNER)
    common = (p["mix_norm"][0], bf(w_in))
    conv_w, conv_b = p["ssd_conv_w"][0], p["ssd_conv_b"][0]
    z, acum, tr, xbc, conv_p = _ssd_in_prompt(x2d, bp, lp, *common, bf(wdt), conv_w, conv_b, dtb,
                                              jnp.zeros((bp, CONV_W - 1, CONV_DIM), F32), a_pad)
    y_p, ssm_p = _scan_prompt(xbc, z, acum, tr, d_e, ng)
    z, xbc, dt, conv_s = _ssd_in(x2d, rows[0], bs, ls, *common, bf(wdt_s), conv_w, conv_b, dtb_s,
                                 conv0_s, nb=32, tl=SAMPLE_LEN)
    y_s, ssm_s = _scan_sample(xbc, z, dt, ssm0_s.reshape(bs, D_INNER, D_STATE), a_pad,
                              _expand_heads(a_neg), d_e, ng, _group_sum_matrix(),
                              _group_expand_matrix())
    x2d = ffn([x2d], 0, 1, pre=([y_p.reshape(rows[0], D_INNER), bf(y_s).reshape(rows[1], D_INNER)],
                                bf(p["ssd_w_out"][0])))

    x2d = ffn([x2d], 1, 0)
    pool_w = (p["mix_norm"][1], bf(p["pool_w_in"][0]), bf(p["pool_w_group"][0]), p["pool_scale"][0],
              bf(p["pool_w_out"][0]))
    xm_p, pool_p = _pool(x2d, 0, bp, lp, *pool_w, jnp.zeros((bp, MAX_WIN - 1, D_MODEL), F32),
                         nb=1, tl=512, pos0=0)
    xm_s, pool_s = _pool(x2d, rows[0], bs, ls, *pool_w, pool0_s, nb=64, tl=SAMPLE_LEN, pos0=PAST_LEN)
    out_p, out_s = ffn([xm_p, xm_s], 1, 1, final_g=p["final_norm"], split_out=True)
    state = lambda a, b: a.reshape(b, N_HEADS, HEAD_DIM, D_STATE)[None]
    return (out_p.reshape(bp, lp, D_MODEL), out_s.reshape(bs, ls, D_MODEL),
            state(ssm_p, bp), conv_p[None], pool_p[None], state(ssm_s, bs), conv_s[None], pool_s[None])


def kernel(x_prompt, x_sample, state_ssm, state_conv, state_pool, ffn_norm, ffn_w_gate, ffn_w_up,
           ffn_w_down, mix_norm, ssd_w_in, ssd_conv_w, ssd_conv_b, ssd_dt_bias, ssd_a_log, ssd_d,
           ssd_norm, ssd_w_out, pool_w_in, pool_w_group, pool_scale, pool_w_out, final_norm):
    p = dict(ffn_norm=ffn_norm, ffn_w_gate=ffn_w_gate, ffn_w_up=ffn_w_up, ffn_w_down=ffn_w_down,
             mix_norm=mix_norm, ssd_w_in=ssd_w_in, ssd_conv_w=ssd_conv_w, ssd_conv_b=ssd_conv_b,
             ssd_dt_bias=ssd_dt_bias, ssd_a_log=ssd_a_log, ssd_d=ssd_d, ssd_norm=ssd_norm,
             ssd_w_out=ssd_w_out, pool_w_in=pool_w_in, pool_w_group=pool_w_group,
             pool_scale=pool_scale, pool_w_out=pool_w_out, final_norm=final_norm)
    return _trunks(x_prompt, x_sample, state_ssm[0], state_conv[0], state_pool[0], p)
```

```python
import functools

import numpy as np
import jax
import jax.numpy as jnp
from jax import lax
from jax.experimental import pallas as pl
from jax.experimental.pallas import tpu as pltpu

F32 = jnp.float32
BF16 = jnp.bfloat16

EPS = 1e-6
D_MODEL = 1024
D_FF = 2816
D_INNER = 2048
HEAD_DIM = 64
N_HEADS = 32
N_GROUPS = 8
HEADS_PER_GROUP = 4
GROUP_DIM = HEADS_PER_GROUP * HEAD_DIM
D_STATE = 128
CONV_W = 4
CONV_DIM = D_INNER + 2 * N_GROUPS * D_STATE
CHUNK = 128
POOL_WINDOWS = (2, 4, 8, 16)
POOL_GROUP_DIM = 256
MAX_WIN = 16
PAST_LEN = 16384
LANES = 128
SUBLANES = 8
DT_PAD = LANES
VMEM_LIMIT = 60 * 1024 * 1024

NT_DIMS = (((1,), (1,)), ((), ()))
LOG2E = 1.4426950408889634


def _resident(shape, lead=()):
    nd = len(shape)
    return pl.BlockSpec((None,) * len(lead) + tuple(shape), lambda *_: tuple(lead) + (0,) * nd,
                        pipeline_mode=pl.Buffered(1))


def _params(n_axes):
    return pltpu.CompilerParams(dimension_semantics=("arbitrary",) * n_axes,
                                vmem_limit_bytes=VMEM_LIMIT)


def _rmsnorm(x, g):
    ms = jnp.mean(x * x, axis=-1, keepdims=True)
    return x * lax.rsqrt(ms + EPS) * g


def _silu(x):
    return x * jax.nn.sigmoid(x)


def _softplus(x):
    return jnp.maximum(x, 0.0) + jnp.log(1.0 + jnp.exp(-jnp.abs(x)))


def _dot(a, b):
    return jnp.dot(a, b, preferred_element_type=F32)


def _dot_nt(a, b):
    return lax.dot_general(a, b, NT_DIMS, preferred_element_type=F32)


def _dot_exact01(t, x):
    hi = x.astype(BF16)
    r1 = x - hi.astype(F32)
    mid = r1.astype(BF16)
    lo = (r1 - mid.astype(F32)).astype(BF16)
    return _dot(t, hi) + _dot(t, mid) + _dot(t, lo)


FFN_TM = 512
FFN_TF = 256


def _ffn_kernel(*refs, n_x, n_y, final, n_out, tiles_a):
    refs = list(refs)
    x_refs = [refs.pop(0) for _ in range(n_x)]
    y_refs = [refs.pop(0) for _ in range(n_y)]
    wpre_ref = refs.pop(0) if n_y else None
    g_ref, wg_ref, wu_ref, wd_ref = refs[:4]
    del refs[:4]
    fg_ref = refs.pop(0) if final else None
    o_refs = [refs.pop(0) for _ in range(n_out)]
    h_ref, a_ref = refs
    second = pl.program_id(0) >= tiles_a

    def pick(rs, dtype):
        if len(rs) == 1:
            return rs[0][...].astype(dtype)
        return jnp.where(second, rs[1][...].astype(dtype), rs[0][...].astype(dtype))

    x = pick(x_refs, F32)
    if n_y:
        x = x + _dot(pick(y_refs, BF16), wpre_ref[...])
    h_ref[...] = _rmsnorm(x, g_ref[...]).astype(BF16)
    for f in range(0, D_FF, FFN_TF):
        h = h_ref[...]
        gate = _dot(h, wg_ref[:, f:f + FFN_TF].astype(BF16))
        up = _dot(h, wu_ref[:, f:f + FFN_TF].astype(BF16))
        a_ref[:, f:f + FFN_TF] = (_silu(gate) * up).astype(BF16)
    y = x + 0.5 * _dot(a_ref[...], wd_ref[...].astype(BF16))
    if final:
        y = _rmsnorm(y, fg_ref[...])
    if n_out == 1:
        o_refs[0][...] = y
    else:
        @pl.when(jnp.logical_not(second))
        def _():
            o_refs[0][...] = y

        @pl.when(second)
        def _():
            o_refs[1][...] = y


def _ffn(xs, rows, g, w_gate, w_up, w_down, idx, final_g=None, pre=None, split_out=False):
    tm = FFN_TM
    tiles_a, tiles_b = rows[0] // tm, rows[1] // tm
    assert rows[0] % tm == 0 and rows[1] % tm == 0
    final = final_g is not None

    def row_specs(arrs):
        if len(arrs) == 1:
            return [pl.BlockSpec((tm, arrs[0].shape[1]), lambda i: (i, 0))]
        return [pl.BlockSpec((tm, arrs[0].shape[1]), lambda i: (jnp.minimum(i, tiles_a - 1), 0)),
                pl.BlockSpec((tm, arrs[1].shape[1]), lambda i: (jnp.maximum(i - tiles_a, 0), 0))]

    in_specs, args = row_specs(xs), list(xs)
    ys = []
    if pre is not None:
        ys, wpre = pre
        in_specs += row_specs(ys) + [_resident(wpre.shape)]
        args += list(ys) + [wpre]
    in_specs += [_resident((1, D_MODEL)), _resident((D_MODEL, D_FF), idx),
                 _resident((D_MODEL, D_FF), idx), _resident((D_FF, D_MODEL), idx)]
    args += [g.reshape(1, D_MODEL), w_gate, w_up, w_down]
    if final:
        in_specs.append(_resident((1, D_MODEL)))
        args.append(final_g.reshape(1, D_MODEL))
    if split_out:
        outs = [jax.ShapeDtypeStruct((r, D_MODEL), F32) for r in rows]
    else:
        outs = [jax.ShapeDtypeStruct((rows[0] + rows[1], D_MODEL), F32)]
    res = pl.pallas_call(
        functools.partial(_ffn_kernel, n_x=len(xs), n_y=len(ys), final=final, n_out=len(outs),
                          tiles_a=tiles_a),
        grid=(tiles_a + tiles_b,),
        in_specs=in_specs,
        out_specs=row_specs(outs),
        out_shape=outs,
        scratch_shapes=[pltpu.VMEM((tm, D_MODEL), BF16), pltpu.VMEM((tm, D_FF), BF16)],
        compiler_params=_params(1),
        name="ffn" + ("_pre" if pre is not None else "") + ("_final" if final else ""),
    )(*args)
    return res if split_out else res[0]


CONV_HALO = SUBLANES
SSD_IN_LANE_CHUNK = 512


def _shift_rows(v, k):
    if k == SUBLANES:
        return jnp.concatenate([v[:, :1], v[:, :-1]], axis=1)
    r = pltpu.roll(v, k, 2)
    prev = jnp.concatenate([r[:, :1], r[:, :-1]], axis=1)
    sub = lax.broadcasted_iota(jnp.int32, v.shape, 2)
    return jnp.where(sub < k, prev, r)


def _chunk_decays(dt, a_pad):
    q = dt.shape[0]
    tril = (lax.broadcasted_iota(jnp.int32, (q, q), 0) >= lax.broadcasted_iota(jnp.int32, (q, q), 1))
    acum = _dot_exact01(tril.astype(BF16), dt * a_pad) * LOG2E
    acum_t = acum.T
    dt_t = dt.T
    src_t = acum_t - jnp.log2(dt_t)
    w_t = jnp.exp2(acum_t[:, q - 1:q] - acum_t) * dt_t
    return acum, acum_t, src_t, w_t


def _conv_silu(ext, cw_ref, cb_ref, sl):
    w = ext.shape[-1]
    tap = lambda k: cw_ref[k:k + 1, sl].reshape(1, 1, 1, w)
    ext1 = _shift_rows(ext, 1)
    p = ext * tap(3) + ext1 * tap(2)
    q = ext * tap(1) + ext1 * tap(0)
    return _silu((cb_ref[:, sl].reshape(1, 1, 1, w) + p + _shift_rows(q, 2))[:, 1:])


def _ssd_in_kernel(x_ref, g_ref, win_ref, wdt_ref, cw_ref, cb_ref, dtb_ref, conv0_ref,
                   z_ref, xbc_ref, dt_ref, convnew_ref, halo_ref, h_ref, *, nb, tl):
    rows = nb * tl
    wc = SSD_IN_LANE_CHUNK
    nblk = tl // SUBLANES

    @pl.when(pl.program_id(1) == 0)
    def _():
        halo_ref[...] = jnp.zeros_like(halo_ref)
        halo_ref[:, CONV_HALO - (CONV_W - 1):, :] = conv0_ref[...]

    h_ref[...] = _rmsnorm(x_ref[...], g_ref[...]).astype(BF16)
    dt = _softplus(_dot(h_ref[...], wdt_ref[...]) + dtb_ref[...])
    dt_ref[...] = dt.reshape(nb, tl, dt.shape[-1])
    for c in range(0, CONV_DIM, wc):
        sl = slice(c, c + wc)
        zs = slice(c // 2, c // 2 + wc // 2)
        z_ref[:, :, zs] = _dot(h_ref[...], win_ref[:, zs]).reshape(nb, tl, wc // 2)
        cur = _dot(h_ref[...], win_ref[:, D_INNER + c:D_INNER + c + wc]).reshape(nb, tl, wc)
        ext = jnp.concatenate([halo_ref[:, :, sl], cur], axis=1).reshape(nb, nblk + 1, SUBLANES, wc)
        xbc_ref[:, :, sl] = _conv_silu(ext, cw_ref, cb_ref, sl).reshape(nb, tl, wc)
        halo_ref[:, :, sl] = cur[:, tl - CONV_HALO:, :]
    convnew_ref[...] = halo_ref[:, CONV_HALO - (CONV_W - 1):, :]


def _ssd_in(x2d, row0, b, l, g, w_in, wdt, cw, cb, dtb, conv0, *, nb, tl):
    assert l == tl and row0 % (nb * tl) == 0
    tile0 = row0 // (nb * tl)
    dt_w = wdt.shape[1]
    blk = lambda w: pl.BlockSpec((nb, tl, w), lambda i, j: (i, j, 0))
    return pl.pallas_call(
        functools.partial(_ssd_in_kernel, nb=nb, tl=tl),
        grid=(b // nb, l // tl),
        in_specs=[pl.BlockSpec((nb * tl, D_MODEL), lambda i, j: (tile0 + i, 0)),
                  _resident((1, D_MODEL)), _resident(w_in.shape), _resident((D_MODEL, dt_w)),
                  _resident((CONV_W, CONV_DIM)), _resident((1, CONV_DIM)), _resident((1, dt_w)),
                  pl.BlockSpec((nb, CONV_W - 1, CONV_DIM), lambda i, j: (i, 0, 0))],
        out_specs=[blk(D_INNER), blk(CONV_DIM), blk(dt_w),
                   pl.BlockSpec((nb, CONV_W - 1, CONV_DIM), lambda i, j: (i, 0, 0))],
        out_shape=[jax.ShapeDtypeStruct((b, l, D_INNER), F32),
                   jax.ShapeDtypeStruct((b, l, CONV_DIM), F32),
                   jax.ShapeDtypeStruct((b, l, dt_w), F32),
                   jax.ShapeDtypeStruct((b, CONV_W - 1, CONV_DIM), F32)],
        scratch_shapes=[pltpu.VMEM((nb, CONV_HALO, CONV_DIM), F32),
                        pltpu.VMEM((nb * tl, D_MODEL), BF16)],
        compiler_params=_params(2),
        name="ssd_in",
    )(x2d, g.reshape(1, D_MODEL), w_in, wdt, cw, cb.reshape(1, CONV_DIM),
      dtb.reshape(1, dt_w), conv0)


SSD_PIPE_TL = 512


def _ssd_in_prompt_kernel(x_ref, g_ref, win_ref, wdt_ref, cw_ref, cb_ref, dtb_ref, conv0_ref,
                          apad_ref, z_ref, acum_ref, tr_ref, xbc_ref, convnew_ref, halo_ref, h_ref):
    tl = SSD_PIPE_TL
    wc = SSD_IN_LANE_CHUNK
    nblk = tl // SUBLANES

    @pl.when(pl.program_id(1) == 0)
    def _():
        halo_ref[...] = jnp.zeros_like(halo_ref)
        halo_ref[CONV_HALO - (CONV_W - 1):, :] = conv0_ref[0]

    h_ref[...] = _rmsnorm(x_ref[...], g_ref[...]).astype(BF16)
    dt = _softplus(_dot(h_ref[...], wdt_ref[...]) + dtb_ref[...])
    for j in range(tl // CHUNK):
        cs = slice(j * CHUNK, (j + 1) * CHUNK)
        acum, acum_t, src_t, w_t = _chunk_decays(dt[cs, :], apad_ref[...])
        acum_ref[0, cs, :] = acum
        tr_ref[0, j, 0] = acum_t
        tr_ref[0, j, 1] = src_t
        tr_ref[0, j, 2] = w_t
    for c in range(0, CONV_DIM, wc):
        sl = slice(c, c + wc)
        zs = slice(c // 2, c // 2 + wc // 2)
        z_ref[0, :, zs] = _dot(h_ref[...], win_ref[:, zs]).astype(z_ref.dtype)
        cur = _dot(h_ref[...], win_ref[:, D_INNER + c:D_INNER + c + wc])
        ext = jnp.concatenate([halo_ref[:, sl], cur], axis=0).reshape(1, nblk + 1, SUBLANES, wc)
        xbc_ref[0, :, sl] = _conv_silu(ext, cw_ref, cb_ref, sl).reshape(tl, wc)
        halo_ref[:, sl] = cur[tl - CONV_HALO:, :]
    convnew_ref[0] = halo_ref[CONV_HALO - (CONV_W - 1):, :]


def _ssd_in_prompt(x2d, b, l, g, w_in, wdt, cw, cb, dtb, conv0, a_pad):
    tl = SSD_PIPE_TL
    tps = l // tl
    ncs = tl // CHUNK
    blk = lambda w: pl.BlockSpec((1, tl, w), lambda i, j: (i, j, 0))
    return pl.pallas_call(
        _ssd_in_prompt_kernel,
        grid=(b, l // tl),
        in_specs=[pl.BlockSpec((tl, D_MODEL), lambda i, j: (i * tps + j, 0)),
                  _resident((1, D_MODEL)), _resident(w_in.shape), _resident((D_MODEL, DT_PAD)),
                  _resident((CONV_W, CONV_DIM)), _resident((1, CONV_DIM)), _resident((1, DT_PAD)),
                  pl.BlockSpec((1, CONV_W - 1, CONV_DIM), lambda i, j: (i, 0, 0)),
                  _resident((1, DT_PAD))],
        out_specs=[blk(D_INNER), blk(DT_PAD),
                   pl.BlockSpec((1, ncs, 3, DT_PAD, CHUNK), lambda i, j: (i, j, 0, 0, 0)),
                   blk(CONV_DIM),
                   pl.BlockSpec((1, CONV_W - 1, CONV_DIM), lambda i, j: (i, 0, 0))],
        out_shape=[jax.ShapeDtypeStruct((b, l, D_INNER), BF16),
                   jax.ShapeDtypeStruct((b, l, DT_PAD), F32),
                   jax.ShapeDtypeStruct((b, l // CHUNK, 3, DT_PAD, CHUNK), F32),
                   jax.ShapeDtypeStruct((b, l, CONV_DIM), F32),
                   jax.ShapeDtypeStruct((b, CONV_W - 1, CONV_DIM), F32)],
        scratch_shapes=[pltpu.VMEM((CONV_HALO, CONV_DIM), F32), pltpu.VMEM((tl, D_MODEL), BF16)],
        compiler_params=_params(2),
        name="ssd_in_prompt",
    )(x2d, g.reshape(1, D_MODEL), w_in, wdt, cw, cb.reshape(1, CONV_DIM),
      dtb.reshape(1, DT_PAD), conv0, a_pad)


B_OFF = D_INNER
C_OFF = D_INNER + N_GROUPS * D_STATE


def _gate_norm(y, z, ng):
    yg = y * _silu(z)
    ms = jnp.mean(yg * yg, axis=-1, keepdims=True)
    return yg * lax.rsqrt(ms + EPS) * ng


def _head_rows(mat, g):
    return jnp.concatenate(
        [jnp.broadcast_to(mat[g * HEADS_PER_GROUP + r:g * HEADS_PER_GROUP + r + 1, :],
                          (HEAD_DIM, mat.shape[1])) for r in range(HEADS_PER_GROUP)], axis=0)


def _scan_prompt_kernel(xbc_ref, z_ref, acum_ref, tr_ref, de_ref, ng_ref, y_ref, hout_ref, *h_refs,
                        chunks):
    q = CHUNK
    step = pl.program_id(1)

    @pl.when(step == 0)
    def _():
        for h_ref in h_refs:
            h_ref[...] = jnp.zeros_like(h_ref)

    tril = (lax.broadcasted_iota(jnp.int32, (q, q), 0) >= lax.broadcasted_iota(jnp.int32, (q, q), 1))
    lane_g = lax.broadcasted_iota(jnp.int32, (q, GROUP_DIM), 1)
    lane_half = lax.broadcasted_iota(jnp.int32, (q, LANES), 1) < HEAD_DIM

    def chunk_body(groups, ci, carry):
        ts = pl.ds(pl.multiple_of(ci * q, q), q)
        acum = acum_ref[0, ts, :]
        acum_t, src_t, w_t = tr_ref[0, ci, 0], tr_ref[0, ci, 1], tr_ref[0, ci, 2]
        cd_b = jnp.broadcast_to(jnp.exp2(acum_t[:, q - 1:q]), (DT_PAD, D_STATE))
        for g in groups:
            gs = slice(g * GROUP_DIM, (g + 1) * GROUP_DIM)
            bg = xbc_ref[0, ts, B_OFF + g * D_STATE:B_OFF + (g + 1) * D_STATE].astype(BF16)
            cg = xbc_ref[0, ts, C_OFF + g * D_STATE:C_OFF + (g + 1) * D_STATE].astype(BF16)
            xg = xbc_ref[0, ts, gs]
            xgb = xg.astype(BF16)
            cb = _dot_nt(cg, bg)
            ms, cols = [], []
            for r in range(HEADS_PER_GROUP):
                hd = g * HEADS_PER_GROUP + r
                colf = jnp.broadcast_to(acum[:, hd:hd + 1], (q, q))
                rowf = jnp.broadcast_to(src_t[hd:hd + 1, :], (q, q))
                ms.append((cb * jnp.exp2(jnp.where(tril, colf - rowf, -jnp.inf))).astype(BF16))
                cols.append(colf)
            zero = jnp.zeros_like(xgb)
            y_diag = sum(
                _dot(ms[r], jnp.where((lane_g >= r * HEAD_DIM) & (lane_g < (r + 1) * HEAD_DIM),
                                      xgb, zero)) for r in range(HEADS_PER_GROUP))
            hg = h_refs[g][...]
            y_off = _dot_nt(cg, hg.astype(BF16))
            acum_e = jnp.concatenate([jnp.where(lane_half, cols[0], cols[1]),
                                      jnp.where(lane_half, cols[2], cols[3])], axis=1)
            y = y_diag + y_off * jnp.exp2(acum_e) + de_ref[:, gs] * xg
            xg_t = xg.T
            wt = jnp.concatenate(
                [xg_t[r * HEAD_DIM:(r + 1) * HEAD_DIM, :] * w_t[g * HEADS_PER_GROUP + r:
                                                                g * HEADS_PER_GROUP + r + 1, :]
                 for r in range(HEADS_PER_GROUP)], axis=0).astype(BF16)
            h_refs[g][...] = hg * _head_rows(cd_b, g) + _dot(wt, bg)
            zg = z_ref[0, ts, gs].astype(F32)
            y_ref[0, ts, gs] = _gate_norm(y, zg, ng_ref[:, gs]).astype(y_ref.dtype)
        return carry

    for g0 in range(0, N_GROUPS, SCAN_GROUPS_PER_PASS):
        groups = range(g0, g0 + SCAN_GROUPS_PER_PASS)
        lax.fori_loop(0, chunks, functools.partial(chunk_body, groups), 0)

    @pl.when(step == pl.num_programs(1) - 1)
    def _():
        for g, h_ref in enumerate(h_refs):
            hout_ref[0, g * GROUP_DIM:(g + 1) * GROUP_DIM, :] = h_ref[...]


SCAN_CHUNKS_PER_STEP = 4
SCAN_GROUPS_PER_PASS = 4


def _scan_prompt(xbc, z, acum, tr, d_e, ng):
    b, l, _ = xbc.shape
    chunks = SCAN_CHUNKS_PER_STEP
    tl = chunks * CHUNK
    blk = lambda w: pl.BlockSpec((1, tl, w), lambda i, j: (i, j, 0))
    return pl.pallas_call(
        functools.partial(_scan_prompt_kernel, chunks=chunks),
        grid=(b, l // tl),
        in_specs=[blk(CONV_DIM), blk(D_INNER), blk(DT_PAD),
                  pl.BlockSpec((1, chunks, 3, DT_PAD, CHUNK), lambda i, j: (i, j, 0, 0, 0)),
                  _resident((1, D_INNER)), _resident((1, D_INNER))],
        out_specs=[blk(D_INNER), pl.BlockSpec((1, D_INNER, D_STATE), lambda i, j: (i, 0, 0))],
        out_shape=[jax.ShapeDtypeStruct((b, l, D_INNER), BF16),
                   jax.ShapeDtypeStruct((b, D_INNER, D_STATE), F32)],
        scratch_shapes=[pltpu.VMEM((GROUP_DIM, D_STATE), F32)] * N_GROUPS,
        compiler_params=_params(2),
        name="ssd_scan_prompt",
    )(xbc, z, acum, tr, d_e, ng)


SAMPLE_LEN = 8
SCAN_NB = 16
SCAN_ROWS = SCAN_NB * SAMPLE_LEN
SCAN_STEP_NB = 8


def _scan_sample_kernel(xbc_ref, z_ref, dt_ref, h0_ref, apad_ref, ae_ref, de_ref, ng_ref,
                        gsum_ref, gexp_ref, y_ref, hout_ref,
                        wt_ref, ea_ref, ac_ref, xdt_ref, acp_ref, p_ref):
    rows = SCAN_ROWS
    srows = SCAN_STEP_NB * SAMPLE_LEN
    j = pl.program_id(1)

    def bcast_token(v, s):
        n, w = v.shape[0] // SAMPLE_LEN, v.shape[-1]
        v3 = v.reshape(n, SAMPLE_LEN, w)
        return jnp.broadcast_to(v3[:, s:s + 1, :], (n, SAMPLE_LEN, w)).reshape(n * SAMPLE_LEN, w)

    @pl.when(j == 0)
    def _():
        tok = lax.broadcasted_iota(jnp.int32, (rows, D_INNER), 0) & (SAMPLE_LEN - 1)
        tok_p = lax.broadcasted_iota(jnp.int32, (rows, DT_PAD), 0) & (SAMPLE_LEN - 1)

        def cumsum_tokens(v, t):
            for sh in (1, 2, 4):
                v = v + jnp.where(t >= sh, pltpu.roll(v, sh, 0), 0.0)
            return v

        xs = xbc_ref[:, :, 0:D_INNER].reshape(rows, D_INNER)
        dt_p = dt_ref[:, :, 0:DT_PAD].reshape(rows, DT_PAD)
        dt_e = dt_ref[:, :, DT_PAD:DT_PAD + D_INNER].reshape(rows, D_INNER)
        acp_ref[...] = cumsum_tokens(dt_p * apad_ref[...], tok_p)
        acum_e = cumsum_tokens(dt_e * ae_ref[...], tok)
        xdt = xs * dt_e
        w = xdt * jnp.exp(bcast_token(acum_e, SAMPLE_LEN - 1) - acum_e)
        wt_ref[...] = w.T.astype(BF16)
        ea_ref[...] = jnp.exp(acum_e)
        ac_ref[...] = acum_e
        xdt_ref[...] = xdt

    seq0 = j * SCAN_STEP_NB
    step_rows = pl.ds(pl.multiple_of(seq0 * SAMPLE_LEN, srows), srows)
    step_seqs = pl.ds(seq0, SCAN_STEP_NB)
    tok = lax.broadcasted_iota(jnp.int32, (srows, D_INNER), 0) & (SAMPLE_LEN - 1)
    xs = xbc_ref[step_seqs, :, 0:D_INNER].reshape(srows, D_INNER)
    bm = xbc_ref[step_seqs, :, B_OFF:C_OFF].reshape(srows, N_GROUPS * D_STATE)
    cm = xbc_ref[step_seqs, :, C_OFF:CONV_DIM].reshape(srows, N_GROUPS * D_STATE)
    acum_e = ac_ref[step_rows, :]
    xdt = xdt_ref[step_rows, :]
    for s in range(SAMPLE_LEN):
        p_ref[s * srows:(s + 1) * srows, :] = (cm * bcast_token(bm, s)).astype(BF16)
    cb_sum = _dot(p_ref[...], gsum_ref[...])
    cb_e = _dot(cb_sum.astype(BF16), gexp_ref[...])
    yd = de_ref[...] * xs
    for s in range(SAMPLE_LEN):
        diff = acum_e - bcast_token(acum_e, s)
        decay = jnp.exp(jnp.where(tok >= s, diff, -jnp.inf))
        yd = yd + cb_e[s * srows:(s + 1) * srows, :] * decay * bcast_token(xdt, s)

    eye = (lax.broadcasted_iota(jnp.int32, (N_HEADS, DT_PAD), 0)
           == lax.broadcasted_iota(jnp.int32, (N_HEADS, DT_PAD), 1))
    rowid = lax.broadcasted_iota(jnp.int32, (rows, D_STATE), 0)
    for t in range(SCAN_STEP_NB):
        seq = seq0 + t
        r0 = pl.multiple_of(seq * SAMPLE_LEN, SAMPLE_LEN)
        alast = acp_ref[pl.ds(r0 + SAMPLE_LEN - 1, 1), :]
        alast_col = jnp.sum(jnp.where(eye, jnp.broadcast_to(alast, (N_HEADS, DT_PAD)), 0.0),
                            axis=1, keepdims=True)
        cd_b = jnp.broadcast_to(jnp.exp(alast_col), (N_HEADS, D_STATE))
        mine = (rowid >= r0) & (rowid < r0 + SAMPLE_LEN)
        y_offs = []
        for g in range(N_GROUPS):
            gs = slice(g * GROUP_DIM, (g + 1) * GROUP_DIM)
            hg = h0_ref[t, gs, :]
            cg = xbc_ref[seq, :, C_OFF + g * D_STATE:C_OFF + (g + 1) * D_STATE].astype(BF16)
            y_offs.append(_dot_nt(cg, hg.astype(BF16)))
            b_all = xbc_ref[:, :, B_OFF + g * D_STATE:B_OFF + (g + 1) * D_STATE].reshape(
                rows, D_STATE)
            b_mine = jnp.where(mine, b_all, 0.0).astype(BF16)
            hout_ref[t, gs, :] = hg * _head_rows(cd_b, g) + _dot(wt_ref[gs, :], b_mine)
        y = yd[t * SAMPLE_LEN:(t + 1) * SAMPLE_LEN, :] + jnp.concatenate(y_offs, axis=1) * ea_ref[
            pl.ds(r0, SAMPLE_LEN), :]
        z = z_ref[seq]
        y_ref[seq] = jnp.concatenate(
            [_gate_norm(y[:, g * GROUP_DIM:(g + 1) * GROUP_DIM],
                        z[:, g * GROUP_DIM:(g + 1) * GROUP_DIM],
                        ng_ref[:, g * GROUP_DIM:(g + 1) * GROUP_DIM]) for g in range(N_GROUPS)],
            axis=1)


def _scan_sample(xbc, z, dt, h0, a_pad, a_e, d_e, ng, gsum, gexp):
    b = xbc.shape[0]
    dt_w = dt.shape[-1]
    blk = lambda w: pl.BlockSpec((SCAN_NB, SAMPLE_LEN, w), lambda i, j: (i, 0, 0))
    steps = SCAN_NB // SCAN_STEP_NB
    st = pl.BlockSpec((SCAN_STEP_NB, D_INNER, D_STATE), lambda i, j: (i * steps + j, 0, 0))
    return pl.pallas_call(
        _scan_sample_kernel,
        grid=(b // SCAN_NB, steps),
        in_specs=[blk(CONV_DIM), blk(D_INNER), blk(dt_w), st, _resident((1, DT_PAD)),
                  _resident((1, D_INNER)), _resident((1, D_INNER)), _resident((1, D_INNER)),
                  _resident(gsum.shape), _resident(gexp.shape)],
        out_specs=[blk(D_INNER), st],
        out_shape=[jax.ShapeDtypeStruct((b, SAMPLE_LEN, D_INNER), F32),
                   jax.ShapeDtypeStruct((b, D_INNER, D_STATE), F32)],
        scratch_shapes=[pltpu.VMEM((D_INNER, SCAN_ROWS), BF16),
                        pltpu.VMEM((SCAN_ROWS, D_INNER), F32),
                        pltpu.VMEM((SCAN_ROWS, D_INNER), F32),
                        pltpu.VMEM((SCAN_ROWS, D_INNER), F32),
                        pltpu.VMEM((SCAN_ROWS, DT_PAD), F32),
                        pltpu.VMEM((SAMPLE_LEN * SCAN_STEP_NB * SAMPLE_LEN, N_GROUPS * D_STATE),
                                   BF16)],
        compiler_params=_params(2),
        name="ssd_scan_sample",
    )(xbc, z, dt, h0, a_pad, a_e, d_e, ng, gsum, gexp)


POOL_HALO = 2 * SUBLANES


def _pool_kernel(x_ref, g_ref, win_ref, wgrp_ref, scale_ref, wout_ref, buf0_ref,
                 o_ref, bufnew_ref, ext_ref, *, nb, tl, pos0):
    rows = nb * tl
    jt = pl.program_id(1)

    @pl.when(jt == 0)
    def _():
        ext_ref[:, 0:POOL_HALO, :] = jnp.zeros((nb, POOL_HALO, D_MODEL), F32)
        ext_ref[:, POOL_HALO - (MAX_WIN - 1):POOL_HALO, :] = buf0_ref[...]

    x = x_ref[...]
    h = _rmsnorm(x, g_ref[...]).astype(BF16)
    ext_ref[:, POOL_HALO:POOL_HALO + tl, :] = _dot(h, win_ref[...]).reshape(nb, tl, D_MODEL)
    nblk = tl // SUBLANES
    halo_blocks = POOL_HALO // SUBLANES
    shape4 = (nb, nblk, SUBLANES, POOL_GROUP_DIM)
    pos = (pos0 + jt * tl + SUBLANES * lax.broadcasted_iota(jnp.int32, shape4, 1)
           + lax.broadcasted_iota(jnp.int32, shape4, 2)).astype(F32)
    out = x
    for k, w in enumerate(POOL_WINDOWS):
        sl = slice(k * POOL_GROUP_DIM, (k + 1) * POOL_GROUP_DIM)
        ext = ext_ref[:, :, sl].reshape(nb, nblk + halo_blocks, SUBLANES, POOL_GROUP_DIM)
        tot, shift = ext, 1
        while shift < w:
            tot = tot + _shift_rows(tot, shift)
            shift *= 2
        u = ext[:, halo_blocks:]
        mean = tot[:, halo_blocks:] / jnp.minimum(jnp.float32(w), pos + 1.0)
        m = (mean - u).reshape(rows, POOL_GROUP_DIM).astype(BF16)
        mixed = (_dot(m, wgrp_ref[k]) * scale_ref[:, sl]).astype(BF16)
        out = out + _dot(mixed, wout_ref[sl, :])
    o_ref[...] = out
    bufnew_ref[...] = ext_ref[:, tl + 1:tl + POOL_HALO, :]
    ext_ref[:, 0:POOL_HALO, :] = ext_ref[:, tl:tl + POOL_HALO, :]


def _pool(x2d, row0, b, l, g, w_in, w_grp, scale, w_out, buf0, *, nb, tl, pos0):
    assert row0 % (nb * tl) == 0 and (nb == 1 or l == tl)
    tile0, tps = row0 // (nb * tl), l // tl
    return pl.pallas_call(
        functools.partial(_pool_kernel, nb=nb, tl=tl, pos0=pos0),
        grid=(b // nb, tps),
        in_specs=[pl.BlockSpec((nb * tl, D_MODEL), lambda i, j: (tile0 + i * tps + j, 0)),
                  _resident((1, D_MODEL)), _resident((D_MODEL, D_MODEL)),
                  _resident(w_grp.shape), _resident((1, D_MODEL)), _resident((D_MODEL, D_MODEL)),
                  pl.BlockSpec((nb, MAX_WIN - 1, D_MODEL), lambda i, j: (i, 0, 0))],
        out_specs=[pl.BlockSpec((nb * tl, D_MODEL), lambda i, j: (i * tps + j, 0)),
                   pl.BlockSpec((nb, MAX_WIN - 1, D_MODEL), lambda i, j: (i, 0, 0))],
        out_shape=[jax.ShapeDtypeStruct((b * l, D_MODEL), F32),
                   jax.ShapeDtypeStruct((b, MAX_WIN - 1, D_MODEL), F32)],
        scratch_shapes=[pltpu.VMEM((nb, POOL_HALO + tl, D_MODEL), F32)],
        compiler_params=_params(2),
        name="pool_mixer",
    )(x2d, g.reshape(1, D_MODEL), w_in, w_grp, scale.reshape(1, D_MODEL), w_out, buf0)


def _expand_heads(v):
    return jnp.repeat(v.astype(F32), HEAD_DIM).reshape(1, D_INNER)


def _pad_heads(v):
    return jnp.pad(v.astype(F32), (0, DT_PAD - N_HEADS)).reshape(1, DT_PAD)


def _group_sum_matrix():
    m = np.zeros((N_GROUPS * D_STATE, LANES), np.float32)
    for g in range(N_GROUPS):
        m[g * D_STATE:(g + 1) * D_STATE, g] = 1.0
    return jnp.asarray(m, BF16)


def _group_expand_matrix():
    m = np.zeros((LANES, D_INNER), np.float32)
    for g in range(N_GROUPS):
        m[g, g * GROUP_DIM:(g + 1) * GROUP_DIM] = 1.0
    return jnp.asarray(m, BF16)


def _trunks(x_prompt, x_sample, ssm0_s, conv0_s, pool0_s, p):
    bp, lp, _ = x_prompt.shape
    bs, ls, _ = x_sample.shape
    rows = (bp * lp, bs * ls)
    bf = lambda a: a.astype(BF16)
    w_gate, w_up, w_down = p["ffn_w_gate"], p["ffn_w_up"], p["ffn_w_down"]
    ffn = lambda xs, i, k, **kw: _ffn(xs, rows, p["ffn_norm"][i, k], w_gate, w_up, w_down, (i, k), **kw)
    x2d = ffn([x_prompt.reshape(rows[0], D_MODEL), x_sample.reshape(rows[1], D_MODEL)], 0, 0)

    w_in = p["ssd_w_in"][0]
    w_dt = w_in[:, D_INNER + CONV_DIM:]
    dt_bias = p["ssd_dt_bias"][0]
    wdt = jnp.pad(w_dt, ((0, 0), (0, DT_PAD - N_HEADS)))
    dtb = jnp.pad(dt_bias, (0, DT_PAD - N_HEADS))
    wdt_s = jnp.concatenate([wdt, jnp.repeat(w_dt, HEAD_DIM, axis=1)], axis=1)
    dtb_s = jnp.concatenate([dtb, jnp.repeat(dt_bias, HEAD_DIM)])
    a_neg = -jnp.exp(p["ssd_a_log"][0].astype(F32))
    a_pad, d_e = _pad_heads(a_neg), _expand_heads(p["ssd_d"][0])
    ng = p["ssd_norm"][0].reshape(1, D_INNER)
    common = (p["mix_norm"][0], bf(w_in))
    conv_w, conv_b = p["ssd_conv_w"][0], p["ssd_conv_b"][0]
    z, acum, tr, xbc, conv_p = _ssd_in_prompt(x2d, bp, lp, *common, bf(wdt), conv_w, conv_b, dtb,
                                              jnp.zeros((bp, CONV_W - 1, CONV_DIM), F32), a_pad)
    y_p, ssm_p = _scan_prompt(xbc, z, acum, tr, d_e, ng)
    z, xbc, dt, conv_s = _ssd_in(x2d, rows[0], bs, ls, *common, bf(wdt_s), conv_w, conv_b, dtb_s,
                                 conv0_s, nb=32, tl=SAMPLE_LEN)
    y_s, ssm_s = _scan_sample(xbc, z, dt, ssm0_s.reshape(bs, D_INNER, D_STATE), a_pad,
                              _expand_heads(a_neg), d_e, ng, _group_sum_matrix(),
                              _group_expand_matrix())
    x2d = ffn([x2d], 0, 1, pre=([y_p.reshape(rows[0], D_INNER), bf(y_s).reshape(rows[1], D_INNER)],
                                bf(p["ssd_w_out"][0])))

    x2d = ffn([x2d], 1, 0)
    pool_w = (p["mix_norm"][1], bf(p["pool_w_in"][0]), bf(p["pool_w_group"][0]), p["pool_scale"][0],
              bf(p["pool_w_out"][0]))
    xm_p, pool_p = _pool(x2d, 0, bp, lp, *pool_w, jnp.zeros((bp, MAX_WIN - 1, D_MODEL), F32),
                         nb=1, tl=512, pos0=0)
    xm_s, pool_s = _pool(x2d, rows[0], bs, ls, *pool_w, pool0_s, nb=64, tl=SAMPLE_LEN, pos0=PAST_LEN)
    out_p, out_s = ffn([xm_p, xm_s], 1, 1, final_g=p["final_norm"], split_out=True)
    state = lambda a, b: a.reshape(b, N_HEADS, HEAD_DIM, D_STATE)[None]
    return (out_p.reshape(bp, lp, D_MODEL), out_s.reshape(bs, ls, D_MODEL),
            state(ssm_p, bp), conv_p[None], pool_p[None], state(ssm_s, bs), conv_s[None], pool_s[None])


def kernel(x_prompt, x_sample, state_ssm, state_conv, state_pool, ffn_norm, ffn_w_gate, ffn_w_up,
           ffn_w_down, mix_norm, ssd_w_in, ssd_conv_w, ssd_conv_b, ssd_dt_bias, ssd_a_log, ssd_d,
           ssd_norm, ssd_w_out, pool_w_in, pool_w_group, pool_scale, pool_w_out, final_norm):
    p = dict(ffn_norm=ffn_norm, ffn_w_gate=ffn_w_gate, ffn_w_up=ffn_w_up, ffn_w_down=ffn_w_down,
             mix_norm=mix_norm, ssd_w_in=ssd_w_in, ssd_conv_w=ssd_conv_w, ssd_conv_b=ssd_conv_b,
             ssd_dt_bias=ssd_dt_bias, ssd_a_log=ssd_a_log, ssd_d=ssd_d, ssd_norm=ssd_norm,
             ssd_w_out=ssd_w_out, pool_w_in=pool_w_in, pool_w_group=pool_w_group,
             pool_scale=pool_scale, pool_w_out=pool_w_out, final_norm=final_norm)
    return _trunks(x_prompt, x_sample, state_ssm[0], state_conv[0], state_pool[0], p)
```

```python
import functools

import numpy as np
import jax
import jax.numpy as jnp
from jax import lax
from jax.experimental import pallas as pl
from jax.experimental.pallas import tpu as pltpu

F32 = jnp.float32
BF16 = jnp.bfloat16

EPS = 1e-6
D_MODEL = 1024
D_FF = 2816
D_INNER = 2048
HEAD_DIM = 64
N_HEADS = 32
N_GROUPS = 8
HEADS_PER_GROUP = 4
GROUP_DIM = HEADS_PER_GROUP * HEAD_DIM
D_STATE = 128
CONV_W = 4
CONV_DIM = D_INNER + 2 * N_GROUPS * D_STATE
CHUNK = 128
POOL_WINDOWS = (2, 4, 8, 16)
POOL_GROUP_DIM = 256
MAX_WIN = 16
PAST_LEN = 16384
LANES = 128
SUBLANES = 8
DT_PAD = LANES
VMEM_LIMIT = 60 * 1024 * 1024

NT_DIMS = (((1,), (1,)), ((), ()))
LOG2E = 1.4426950408889634


def _resident(shape, lead=()):
    nd = len(shape)
    return pl.BlockSpec((None,) * len(lead) + tuple(shape), lambda *_: tuple(lead) + (0,) * nd,
                        pipeline_mode=pl.Buffered(1))


def _params(n_axes):
    return pltpu.CompilerParams(dimension_semantics=("arbitrary",) * n_axes,
                                vmem_limit_bytes=VMEM_LIMIT)


def _rmsnorm(x, g):
    ms = jnp.mean(x * x, axis=-1, keepdims=True)
    return x * lax.rsqrt(ms + EPS) * g


def _silu(x):
    return x * jax.nn.sigmoid(x)


def _softplus(x):
    return jnp.maximum(x, 0.0) + jnp.log(1.0 + jnp.exp(-jnp.abs(x)))


def _dot(a, b):
    return jnp.dot(a, b, preferred_element_type=F32)


def _dot_nt(a, b):
    return lax.dot_general(a, b, NT_DIMS, preferred_element_type=F32)


def _dot_exact01(t, x):
    hi = x.astype(BF16)
    r1 = x - hi.astype(F32)
    mid = r1.astype(BF16)
    lo = (r1 - mid.astype(F32)).astype(BF16)
    return _dot(t, hi) + _dot(t, mid) + _dot(t, lo)


FFN_TM = 512
FFN_TF = 256


def _ffn_kernel(*refs, n_x, n_y, final, n_out, tiles_a):
    refs = list(refs)
    x_refs = [refs.pop(0) for _ in range(n_x)]
    y_refs = [refs.pop(0) for _ in range(n_y)]
    wpre_ref = refs.pop(0) if n_y else None
    g_ref, wg_ref, wu_ref, wd_ref = refs[:4]
    del refs[:4]
    fg_ref = refs.pop(0) if final else None
    o_refs = [refs.pop(0) for _ in range(n_out)]
    h_ref, a_ref = refs
    second = pl.program_id(0) >= tiles_a

    def pick(rs, dtype):
        if len(rs) == 1:
            return rs[0][...].astype(dtype)
        return jnp.where(second, rs[1][...].astype(dtype), rs[0][...].astype(dtype))

    x = pick(x_refs, F32)
    if n_y:
        x = x + _dot(pick(y_refs, BF16), wpre_ref[...])
    h_ref[...] = _rmsnorm(x, g_ref[...]).astype(BF16)
    for f in range(0, D_FF, FFN_TF):
        h = h_ref[...]
        gate = _dot(h, wg_ref[:, f:f + FFN_TF].astype(BF16))
        up = _dot(h, wu_ref[:, f:f + FFN_TF].astype(BF16))
        a_ref[:, f:f + FFN_TF] = (_silu(gate) * up).astype(BF16)
    y = x + 0.5 * _dot(a_ref[...], wd_ref[...].astype(BF16))
    if final:
        y = _rmsnorm(y, fg_ref[...])
    if n_out == 1:
        o_refs[0][...] = y
    else:
        @pl.when(jnp.logical_not(second))
        def _():
            o_refs[0][...] = y

        @pl.when(second)
        def _():
            o_refs[1][...] = y


def _ffn(xs, rows, g, w_gate, w_up, w_down, idx, final_g=None, pre=None, split_out=False):
    tm = FFN_TM
    tiles_a, tiles_b = rows[0] // tm, rows[1] // tm
    assert rows[0] % tm == 0 and rows[1] % tm == 0
    final = final_g is not None

    def row_specs(arrs):
        if len(arrs) == 1:
            return [pl.BlockSpec((tm, arrs[0].shape[1]), lambda i: (i, 0))]
        return [pl.BlockSpec((tm, arrs[0].shape[1]), lambda i: (jnp.minimum(i, tiles_a - 1), 0)),
                pl.BlockSpec((tm, arrs[1].shape[1]), lambda i: (jnp.maximum(i - tiles_a, 0), 0))]

    in_specs, args = row_specs(xs), list(xs)
    ys = []
    if pre is not None:
        ys, wpre = pre
        in_specs += row_specs(ys) + [_resident(wpre.shape)]
        args += list(ys) + [wpre]
    in_specs += [_resident((1, D_MODEL)), _resident((D_MODEL, D_FF), idx),
                 _resident((D_MODEL, D_FF), idx), _resident((D_FF, D_MODEL), idx)]
    args += [g.reshape(1, D_MODEL), w_gate, w_up, w_down]
    if final:
        in_specs.append(_resident((1, D_MODEL)))
        args.append(final_g.reshape(1, D_MODEL))
    if split_out:
        outs = [jax.ShapeDtypeStruct((r, D_MODEL), F32) for r in rows]
    else:
        outs = [jax.ShapeDtypeStruct((rows[0] + rows[1], D_MODEL), F32)]
    res = pl.pallas_call(
        functools.partial(_ffn_kernel, n_x=len(xs), n_y=len(ys), final=final, n_out=len(outs),
                          tiles_a=tiles_a),
        grid=(tiles_a + tiles_b,),
        in_specs=in_specs,
        out_specs=row_specs(outs),
        out_shape=outs,
        scratch_shapes=[pltpu.VMEM((tm, D_MODEL), BF16), pltpu.VMEM((tm, D_FF), BF16)],
        compiler_params=_params(1),
        name="ffn" + ("_pre" if pre is not None else "") + ("_final" if final else ""),
    )(*args)
    return res if split_out else res[0]


CONV_HALO = SUBLANES
SSD_IN_LANE_CHUNK = 512


def _shift_rows(v, k):
    if k == SUBLANES:
        return jnp.concatenate([v[:, :1], v[:, :-1]], axis=1)
    r = pltpu.roll(v, k, 2)
    prev = jnp.concatenate([r[:, :1], r[:, :-1]], axis=1)
    sub = lax.broadcasted_iota(jnp.int32, v.shape, 2)
    return jnp.where(sub < k, prev, r)


def _chunk_decays(dt, a_pad):
    q = dt.shape[0]
    tril = (lax.broadcasted_iota(jnp.int32, (q, q), 0) >= lax.broadcasted_iota(jnp.int32, (q, q), 1))
    acum = _dot_exact01(tril.astype(BF16), dt * a_pad) * LOG2E
    acum_t = acum.T
    dt_t = dt.T
    src_t = acum_t - jnp.log2(dt_t)
    w_t = jnp.exp2(acum_t[:, q - 1:q] - acum_t) * dt_t
    return acum, acum_t, src_t, w_t


def _conv_silu(ext, cw_ref, cb_ref, sl):
    w = ext.shape[-1]
    tap = lambda k: cw_ref[k:k + 1, sl].reshape(1, 1, 1, w)
    ext1 = _shift_rows(ext, 1)
    p = ext * tap(3) + ext1 * tap(2)
    q = ext * tap(1) + ext1 * tap(0)
    return _silu((cb_ref[:, sl].reshape(1, 1, 1, w) + p + _shift_rows(q, 2))[:, 1:])


def _ssd_in_kernel(x_ref, g_ref, win_ref, wdt_ref, cw_ref, cb_ref, dtb_ref, conv0_ref,
                   z_ref, xbc_ref, dt_ref, convnew_ref, halo_ref, h_ref, *, nb, tl):
    rows = nb * tl
    wc = SSD_IN_LANE_CHUNK
    nblk = tl // SUBLANES

    @pl.when(pl.program_id(1) == 0)
    def _():
        halo_ref[...] = jnp.zeros_like(halo_ref)
        halo_ref[:, CONV_HALO - (CONV_W - 1):, :] = conv0_ref[...]

    h_ref[...] = _rmsnorm(x_ref[...], g_ref[...]).astype(BF16)
    dt = _softplus(_dot(h_ref[...], wdt_ref[...]) + dtb_ref[...])
    dt_ref[...] = dt.reshape(nb, tl, dt.shape[-1])
    for c in range(0, CONV_DIM, wc):
        sl = slice(c, c + wc)
        zs = slice(c // 2, c // 2 + wc // 2)
        z_ref[:, :, zs] = _dot(h_ref[...], win_ref[:, zs]).reshape(nb, tl, wc // 2)
        cur = _dot(h_ref[...], win_ref[:, D_INNER + c:D_INNER + c + wc]).reshape(nb, tl, wc)
        ext = jnp.concatenate([halo_ref[:, :, sl], cur], axis=1).reshape(nb, nblk + 1, SUBLANES, wc)
        xbc_ref[:, :, sl] = _conv_silu(ext, cw_ref, cb_ref, sl).reshape(nb, tl, wc)
        halo_ref[:, :, sl] = cur[:, tl - CONV_HALO:, :]
    convnew_ref[...] = halo_ref[:, CONV_HALO - (CONV_W - 1):, :]


def _ssd_in(x2d, row0, b, l, g, w_in, wdt, cw, cb, dtb, conv0, *, nb, tl):
    assert l == tl and row0 % (nb * tl) == 0
    tile0 = row0 // (nb * tl)
    dt_w = wdt.shape[1]
    blk = lambda w: pl.BlockSpec((nb, tl, w), lambda i, j: (i, j, 0))
    return pl.pallas_call(
        functools.partial(_ssd_in_kernel, nb=nb, tl=tl),
        grid=(b // nb, l // tl),
        in_specs=[pl.BlockSpec((nb * tl, D_MODEL), lambda i, j: (tile0 + i, 0)),
                  _resident((1, D_MODEL)), _resident(w_in.shape), _resident((D_MODEL, dt_w)),
                  _resident((CONV_W, CONV_DIM)), _resident((1, CONV_DIM)), _resident((1, dt_w)),
                  pl.BlockSpec((None, nb, CONV_W - 1, CONV_DIM), lambda i, j: (0, i, 0, 0))],
        out_specs=[blk(D_INNER), blk(CONV_DIM), blk(dt_w),
                   pl.BlockSpec((None, nb, CONV_W - 1, CONV_DIM), lambda i, j: (0, i, 0, 0))],
        out_shape=[jax.ShapeDtypeStruct((b, l, D_INNER), F32),
                   jax.ShapeDtypeStruct((b, l, CONV_DIM), F32),
                   jax.ShapeDtypeStruct((b, l, dt_w), F32),
                   jax.ShapeDtypeStruct((1, b, CONV_W - 1, CONV_DIM), F32)],
        scratch_shapes=[pltpu.VMEM((nb, CONV_HALO, CONV_DIM), F32),
                        pltpu.VMEM((nb * tl, D_MODEL), BF16)],
        compiler_params=_params(2),
        name="ssd_in",
    )(x2d, g.reshape(1, D_MODEL), w_in, wdt, cw, cb.reshape(1, CONV_DIM),
      dtb.reshape(1, dt_w), conv0)


SSD_PIPE_TL = 512


def _ssd_in_prompt_kernel(x_ref, g_ref, win_ref, wdt_ref, cw_ref, cb_ref, dtb_ref, conv0_ref,
                          apad_ref, z_ref, acum_ref, tr_ref, xbc_ref, convnew_ref, halo_ref, h_ref):
    tl = SSD_PIPE_TL
    wc = SSD_IN_LANE_CHUNK
    nblk = tl // SUBLANES

    @pl.when(pl.program_id(1) == 0)
    def _():
        halo_ref[...] = jnp.zeros_like(halo_ref)
        halo_ref[CONV_HALO - (CONV_W - 1):, :] = conv0_ref[0]

    h_ref[...] = _rmsnorm(x_ref[...], g_ref[...]).astype(BF16)
    dt = _softplus(_dot(h_ref[...], wdt_ref[...]) + dtb_ref[...])
    for j in range(tl // CHUNK):
        cs = slice(j * CHUNK, (j + 1) * CHUNK)
        acum, acum_t, src_t, w_t = _chunk_decays(dt[cs, :], apad_ref[...])
        acum_ref[0, cs, :] = acum
        tr_ref[0, j, 0] = acum_t
        tr_ref[0, j, 1] = src_t
        tr_ref[0, j, 2] = w_t
    for c in range(0, CONV_DIM, wc):
        sl = slice(c, c + wc)
        zs = slice(c // 2, c // 2 + wc // 2)
        z_ref[0, :, zs] = _dot(h_ref[...], win_ref[:, zs]).astype(z_ref.dtype)
        cur = _dot(h_ref[...], win_ref[:, D_INNER + c:D_INNER + c + wc])
        ext = jnp.concatenate([halo_ref[:, sl], cur], axis=0).reshape(1, nblk + 1, SUBLANES, wc)
        xbc_ref[0, :, sl] = _conv_silu(ext, cw_ref, cb_ref, sl).reshape(tl, wc)
        halo_ref[:, sl] = cur[tl - CONV_HALO:, :]
    convnew_ref[0] = halo_ref[CONV_HALO - (CONV_W - 1):, :]


def _ssd_in_prompt(x2d, b, l, g, w_in, wdt, cw, cb, dtb, conv0, a_pad):
    tl = SSD_PIPE_TL
    tps = l // tl
    ncs = tl // CHUNK
    blk = lambda w: pl.BlockSpec((1, tl, w), lambda i, j: (i, j, 0))
    return pl.pallas_call(
        _ssd_in_prompt_kernel,
        grid=(b, l // tl),
        in_specs=[pl.BlockSpec((tl, D_MODEL), lambda i, j: (i * tps + j, 0)),
                  _resident((1, D_MODEL)), _resident(w_in.shape), _resident((D_MODEL, DT_PAD)),
                  _resident((CONV_W, CONV_DIM)), _resident((1, CONV_DIM)), _resident((1, DT_PAD)),
                  pl.BlockSpec((None, 1, CONV_W - 1, CONV_DIM), lambda i, j: (0, i, 0, 0)),
                  _resident((1, DT_PAD))],
        out_specs=[blk(D_INNER), blk(DT_PAD),
                   pl.BlockSpec((1, ncs, 3, DT_PAD, CHUNK), lambda i, j: (i, j, 0, 0, 0)),
                   blk(CONV_DIM),
                   pl.BlockSpec((None, 1, CONV_W - 1, CONV_DIM), lambda i, j: (0, i, 0, 0))],
        out_shape=[jax.ShapeDtypeStruct((b, l, D_INNER), BF16),
                   jax.ShapeDtypeStruct((b, l, DT_PAD), F32),
                   jax.ShapeDtypeStruct((b, l // CHUNK, 3, DT_PAD, CHUNK), F32),
                   jax.ShapeDtypeStruct((b, l, CONV_DIM), F32),
                   jax.ShapeDtypeStruct((1, b, CONV_W - 1, CONV_DIM), F32)],
        scratch_shapes=[pltpu.VMEM((CONV_HALO, CONV_DIM), F32), pltpu.VMEM((tl, D_MODEL), BF16)],
        compiler_params=_params(2),
        name="ssd_in_prompt",
    )(x2d, g.reshape(1, D_MODEL), w_in, wdt, cw, cb.reshape(1, CONV_DIM),
      dtb.reshape(1, DT_PAD), conv0, a_pad)


B_OFF = D_INNER
C_OFF = D_INNER + N_GROUPS * D_STATE


def _gate_norm(y, z, ng):
    yg = y * _silu(z)
    ms = jnp.mean(yg * yg, axis=-1, keepdims=True)
    return yg * lax.rsqrt(ms + EPS) * ng


def _head_rows(mat, g):
    return jnp.concatenate(
        [jnp.broadcast_to(mat[g * HEADS_PER_GROUP + r:g * HEADS_PER_GROUP + r + 1, :],
                          (HEAD_DIM, mat.shape[1])) for r in range(HEADS_PER_GROUP)], axis=0)


def _scan_prompt_kernel(xbc_ref, z_ref, acum_ref, tr_ref, de_ref, ng_ref, y_ref, hout_ref, *h_refs,
                        chunks):
    q = CHUNK
    step = pl.program_id(1)

    @pl.when(step == 0)
    def _():
        for h_ref in h_refs:
            h_ref[...] = jnp.zeros_like(h_ref)

    tril = (lax.broadcasted_iota(jnp.int32, (q, q), 0) >= lax.broadcasted_iota(jnp.int32, (q, q), 1))
    lane_g = lax.broadcasted_iota(jnp.int32, (q, GROUP_DIM), 1)
    lane_half = lax.broadcasted_iota(jnp.int32, (q, LANES), 1) < HEAD_DIM

    def chunk_body(groups, ci, carry):
        ts = pl.ds(pl.multiple_of(ci * q, q), q)
        acum = acum_ref[0, ts, :]
        acum_t, src_t, w_t = tr_ref[0, ci, 0], tr_ref[0, ci, 1], tr_ref[0, ci, 2]
        cd_b = jnp.broadcast_to(jnp.exp2(acum_t[:, q - 1:q]), (DT_PAD, D_STATE))
        for g in groups:
            gs = slice(g * GROUP_DIM, (g + 1) * GROUP_DIM)
            bg = xbc_ref[0, ts, B_OFF + g * D_STATE:B_OFF + (g + 1) * D_STATE].astype(BF16)
            cg = xbc_ref[0, ts, C_OFF + g * D_STATE:C_OFF + (g + 1) * D_STATE].astype(BF16)
            xg = xbc_ref[0, ts, gs]
            xgb = xg.astype(BF16)
            cb = _dot_nt(cg, bg)
            ms, cols = [], []
            for r in range(HEADS_PER_GROUP):
                hd = g * HEADS_PER_GROUP + r
                colf = jnp.broadcast_to(acum[:, hd:hd + 1], (q, q))
                rowf = jnp.broadcast_to(src_t[hd:hd + 1, :], (q, q))
                ms.append((cb * jnp.exp2(jnp.where(tril, colf - rowf, -jnp.inf))).astype(BF16))
                cols.append(colf)
            zero = jnp.zeros_like(xgb)
            y_diag = sum(
                _dot(ms[r], jnp.where((lane_g >= r * HEAD_DIM) & (lane_g < (r + 1) * HEAD_DIM),
                                      xgb, zero)) for r in range(HEADS_PER_GROUP))
            hg = h_refs[g][...]
            y_off = _dot_nt(cg, hg.astype(BF16))
            acum_e = jnp.concatenate([jnp.where(lane_half, cols[0], cols[1]),
                                      jnp.where(lane_half, cols[2], cols[3])], axis=1)
            y = y_diag + y_off * jnp.exp2(acum_e) + de_ref[:, gs] * xg
            xg_t = xg.T
            wt = jnp.concatenate(
                [xg_t[r * HEAD_DIM:(r + 1) * HEAD_DIM, :] * w_t[g * HEADS_PER_GROUP + r:
                                                                g * HEADS_PER_GROUP + r + 1, :]
                 for r in range(HEADS_PER_GROUP)], axis=0).astype(BF16)
            h_refs[g][...] = hg * _head_rows(cd_b, g) + _dot(wt, bg)
            zg = z_ref[0, ts, gs].astype(F32)
            y_ref[0, ts, gs] = _gate_norm(y, zg, ng_ref[:, gs]).astype(y_ref.dtype)
        return carry

    for g0 in range(0, N_GROUPS, SCAN_GROUPS_PER_PASS):
        groups = range(g0, g0 + SCAN_GROUPS_PER_PASS)
        lax.fori_loop(0, chunks, functools.partial(chunk_body, groups), 0)

    @pl.when(step == pl.num_programs(1) - 1)
    def _():
        for g, h_ref in enumerate(h_refs):
            hout_ref[0, g * GROUP_DIM:(g + 1) * GROUP_DIM, :] = h_ref[...]


SCAN_CHUNKS_PER_STEP = 8
SCAN_GROUPS_PER_PASS = 4


def _scan_prompt(xbc, z, acum, tr, d_e, ng):
    b, l, _ = xbc.shape
    chunks = SCAN_CHUNKS_PER_STEP
    tl = chunks * CHUNK
    blk = lambda w: pl.BlockSpec((1, tl, w), lambda i, j: (i, j, 0))
    return pl.pallas_call(
        functools.partial(_scan_prompt_kernel, chunks=chunks),
        grid=(b, l // tl),
        in_specs=[blk(CONV_DIM), blk(D_INNER), blk(DT_PAD),
                  pl.BlockSpec((1, chunks, 3, DT_PAD, CHUNK), lambda i, j: (i, j, 0, 0, 0)),
                  _resident((1, D_INNER)), _resident((1, D_INNER))],
        out_specs=[blk(D_INNER), pl.BlockSpec((1, D_INNER, D_STATE), lambda i, j: (i, 0, 0))],
        out_shape=[jax.ShapeDtypeStruct((b, l, D_INNER), BF16),
                   jax.ShapeDtypeStruct((b, D_INNER, D_STATE), F32)],
        scratch_shapes=[pltpu.VMEM((GROUP_DIM, D_STATE), F32)] * N_GROUPS,
        compiler_params=_params(2),
        name="ssd_scan_prompt",
    )(xbc, z, acum, tr, d_e, ng)


SAMPLE_LEN = 8
SCAN_NB = 16
SCAN_ROWS = SCAN_NB * SAMPLE_LEN
SCAN_STEP_NB = 8


def _scan_sample_kernel(xbc_ref, z_ref, dt_ref, h0_ref, apad_ref, ae_ref, de_ref, ng_ref,
                        gsum_ref, gexp_ref, y_ref, hout_ref,
                        wt_ref, ea_ref, ac_ref, xdt_ref, acp_ref, p_ref):
    rows = SCAN_ROWS
    srows = SCAN_STEP_NB * SAMPLE_LEN
    j = pl.program_id(1)

    def bcast_token(v, s):
        n, w = v.shape[0] // SAMPLE_LEN, v.shape[-1]
        v3 = v.reshape(n, SAMPLE_LEN, w)
        return jnp.broadcast_to(v3[:, s:s + 1, :], (n, SAMPLE_LEN, w)).reshape(n * SAMPLE_LEN, w)

    @pl.when(j == 0)
    def _():
        tok = lax.broadcasted_iota(jnp.int32, (rows, D_INNER), 0) & (SAMPLE_LEN - 1)
        tok_p = lax.broadcasted_iota(jnp.int32, (rows, DT_PAD), 0) & (SAMPLE_LEN - 1)

        def cumsum_tokens(v, t):
            for sh in (1, 2, 4):
                v = v + jnp.where(t >= sh, pltpu.roll(v, sh, 0), 0.0)
            return v

        xs = xbc_ref[:, :, 0:D_INNER].reshape(rows, D_INNER)
        dt_p = dt_ref[:, :, 0:DT_PAD].reshape(rows, DT_PAD)
        dt_e = dt_ref[:, :, DT_PAD:DT_PAD + D_INNER].reshape(rows, D_INNER)
        acp_ref[...] = cumsum_tokens(dt_p * apad_ref[...], tok_p)
        acum_e = cumsum_tokens(dt_e * ae_ref[...], tok)
        xdt = xs * dt_e
        w = xdt * jnp.exp(bcast_token(acum_e, SAMPLE_LEN - 1) - acum_e)
        wt_ref[...] = w.T.astype(BF16)
        ea_ref[...] = jnp.exp(acum_e)
        ac_ref[...] = acum_e
        xdt_ref[...] = xdt

    seq0 = j * SCAN_STEP_NB
    step_rows = pl.ds(pl.multiple_of(seq0 * SAMPLE_LEN, srows), srows)
    step_seqs = pl.ds(seq0, SCAN_STEP_NB)
    tok = lax.broadcasted_iota(jnp.int32, (srows, D_INNER), 0) & (SAMPLE_LEN - 1)
    xs = xbc_ref[step_seqs, :, 0:D_INNER].reshape(srows, D_INNER)
    bm = xbc_ref[step_seqs, :, B_OFF:C_OFF].reshape(srows, N_GROUPS * D_STATE)
    cm = xbc_ref[step_seqs, :, C_OFF:CONV_DIM].reshape(srows, N_GROUPS * D_STATE)
    acum_e = ac_ref[step_rows, :]
    xdt = xdt_ref[step_rows, :]
    for s in range(SAMPLE_LEN):
        p_ref[s * srows:(s + 1) * srows, :] = (cm * bcast_token(bm, s)).astype(BF16)
    cb_sum = _dot(p_ref[...], gsum_ref[...])
    cb_e = _dot(cb_sum.astype(BF16), gexp_ref[...])
    yd = de_ref[...] * xs
    for s in range(SAMPLE_LEN):
        diff = acum_e - bcast_token(acum_e, s)
        decay = jnp.exp(jnp.where(tok >= s, diff, -jnp.inf))
        yd = yd + cb_e[s * srows:(s + 1) * srows, :] * decay * bcast_token(xdt, s)

    eye = (lax.broadcasted_iota(jnp.int32, (N_HEADS, DT_PAD), 0)
           == lax.broadcasted_iota(jnp.int32, (N_HEADS, DT_PAD), 1))
    rowid = lax.broadcasted_iota(jnp.int32, (rows, D_STATE), 0)
    for t in range(SCAN_STEP_NB):
        seq = seq0 + t
        r0 = pl.multiple_of(seq * SAMPLE_LEN, SAMPLE_LEN)
        alast = acp_ref[pl.ds(r0 + SAMPLE_LEN - 1, 1), :]
        alast_col = jnp.sum(jnp.where(eye, jnp.broadcast_to(alast, (N_HEADS, DT_PAD)), 0.0),
                            axis=1, keepdims=True)
        cd_b = jnp.broadcast_to(jnp.exp(alast_col), (N_HEADS, D_STATE))
        mine = (rowid >= r0) & (rowid < r0 + SAMPLE_LEN)
        y_offs = []
        for g in range(N_GROUPS):
            gs = slice(g * GROUP_DIM, (g + 1) * GROUP_DIM)
            hg = h0_ref[t, gs, :]
            cg = xbc_ref[seq, :, C_OFF + g * D_STATE:C_OFF + (g + 1) * D_STATE].astype(BF16)
            y_offs.append(_dot_nt(cg, hg.astype(BF16)))
            b_all = xbc_ref[:, :, B_OFF + g * D_STATE:B_OFF + (g + 1) * D_STATE].reshape(
                rows, D_STATE)
            b_mine = jnp.where(mine, b_all, 0.0).astype(BF16)
            hout_ref[t, gs, :] = hg * _head_rows(cd_b, g) + _dot(wt_ref[gs, :], b_mine)
        y = yd[t * SAMPLE_LEN:(t + 1) * SAMPLE_LEN, :] + jnp.concatenate(y_offs, axis=1) * ea_ref[
            pl.ds(r0, SAMPLE_LEN), :]
        z = z_ref[seq]
        y_ref[seq] = jnp.concatenate(
            [_gate_norm(y[:, g * GROUP_DIM:(g + 1) * GROUP_DIM],
                        z[:, g * GROUP_DIM:(g + 1) * GROUP_DIM],
                        ng_ref[:, g * GROUP_DIM:(g + 1) * GROUP_DIM]) for g in range(N_GROUPS)],
            axis=1)


def _scan_sample(xbc, z, dt, h0, a_pad, a_e, d_e, ng, gsum, gexp):
    b = xbc.shape[0]
    dt_w = dt.shape[-1]
    blk = lambda w: pl.BlockSpec((SCAN_NB, SAMPLE_LEN, w), lambda i, j: (i, 0, 0))
    steps = SCAN_NB // SCAN_STEP_NB
    st = pl.BlockSpec((SCAN_STEP_NB, D_INNER, D_STATE), lambda i, j: (i * steps + j, 0, 0))
    return pl.pallas_call(
        _scan_sample_kernel,
        grid=(b // SCAN_NB, steps),
        in_specs=[blk(CONV_DIM), blk(D_INNER), blk(dt_w), st, _resident((1, DT_PAD)),
                  _resident((1, D_INNER)), _resident((1, D_INNER)), _resident((1, D_INNER)),
                  _resident(gsum.shape), _resident(gexp.shape)],
        out_specs=[blk(D_INNER), st],
        out_shape=[jax.ShapeDtypeStruct((b, SAMPLE_LEN, D_INNER), F32),
                   jax.ShapeDtypeStruct((b, D_INNER, D_STATE), F32)],
        scratch_shapes=[pltpu.VMEM((D_INNER, SCAN_ROWS), BF16),
                        pltpu.VMEM((SCAN_ROWS, D_INNER), F32),
                        pltpu.VMEM((SCAN_ROWS, D_INNER), F32),
                        pltpu.VMEM((SCAN_ROWS, D_INNER), F32),
                        pltpu.VMEM((SCAN_ROWS, DT_PAD), F32),
                        pltpu.VMEM((SAMPLE_LEN * SCAN_STEP_NB * SAMPLE_LEN, N_GROUPS * D_STATE),
                                   BF16)],
        compiler_params=_params(2),
        name="ssd_scan_sample",
    )(xbc, z, dt, h0, a_pad, a_e, d_e, ng, gsum, gexp)


POOL_HALO = 2 * SUBLANES


def _pool_kernel(x_ref, g_ref, win_ref, wgrp_ref, scale_ref, wout_ref, buf0_ref,
                 o_ref, bufnew_ref, ext_ref, *, nb, tl, pos0):
    rows = nb * tl
    jt = pl.program_id(1)

    @pl.when(jt == 0)
    def _():
        ext_ref[:, 0:POOL_HALO, :] = jnp.zeros((nb, POOL_HALO, D_MODEL), F32)
        ext_ref[:, POOL_HALO - (MAX_WIN - 1):POOL_HALO, :] = buf0_ref[...]

    x = x_ref[...]
    h = _rmsnorm(x, g_ref[...]).astype(BF16)
    ext_ref[:, POOL_HALO:POOL_HALO + tl, :] = _dot(h, win_ref[...]).reshape(nb, tl, D_MODEL)
    nblk = tl // SUBLANES
    halo_blocks = POOL_HALO // SUBLANES
    shape4 = (nb, nblk, SUBLANES, POOL_GROUP_DIM)
    pos = (pos0 + jt * tl + SUBLANES * lax.broadcasted_iota(jnp.int32, shape4, 1)
           + lax.broadcasted_iota(jnp.int32, shape4, 2)).astype(F32)
    out = x
    for k, w in enumerate(POOL_WINDOWS):
        sl = slice(k * POOL_GROUP_DIM, (k + 1) * POOL_GROUP_DIM)
        ext = ext_ref[:, :, sl].reshape(nb, nblk + halo_blocks, SUBLANES, POOL_GROUP_DIM)
        tot, shift = ext, 1
        while shift < w:
            tot = tot + _shift_rows(tot, shift)
            shift *= 2
        u = ext[:, halo_blocks:]
        mean = tot[:, halo_blocks:] / jnp.minimum(jnp.float32(w), pos + 1.0)
        m = (mean - u).reshape(rows, POOL_GROUP_DIM).astype(BF16)
        mixed = (_dot(m, wgrp_ref[k]) * scale_ref[:, sl]).astype(BF16)
        out = out + _dot(mixed, wout_ref[sl, :])
    o_ref[...] = out
    bufnew_ref[...] = ext_ref[:, tl + 1:tl + POOL_HALO, :]
    ext_ref[:, 0:POOL_HALO, :] = ext_ref[:, tl:tl + POOL_HALO, :]


def _pool(x2d, row0, b, l, g, w_in, w_grp, scale, w_out, buf0, *, nb, tl, pos0):
    assert row0 % (nb * tl) == 0 and (nb == 1 or l == tl)
    tile0, tps = row0 // (nb * tl), l // tl
    return pl.pallas_call(
        functools.partial(_pool_kernel, nb=nb, tl=tl, pos0=pos0),
        grid=(b // nb, tps),
        in_specs=[pl.BlockSpec((nb * tl, D_MODEL), lambda i, j: (tile0 + i * tps + j, 0)),
                  _resident((1, D_MODEL)), _resident((D_MODEL, D_MODEL)),
                  _resident(w_grp.shape), _resident((1, D_MODEL)), _resident((D_MODEL, D_MODEL)),
                  pl.BlockSpec((None, nb, MAX_WIN - 1, D_MODEL), lambda i, j: (0, i, 0, 0))],
        out_specs=[pl.BlockSpec((nb * tl, D_MODEL), lambda i, j: (i * tps + j, 0)),
                   pl.BlockSpec((None, nb, MAX_WIN - 1, D_MODEL), lambda i, j: (0, i, 0, 0))],
        out_shape=[jax.ShapeDtypeStruct((b * l, D_MODEL), F32),
                   jax.ShapeDtypeStruct((1, b, MAX_WIN - 1, D_MODEL), F32)],
        scratch_shapes=[pltpu.VMEM((nb, POOL_HALO + tl, D_MODEL), F32)],
        compiler_params=_params(2),
        name="pool_mixer",
    )(x2d, g.reshape(1, D_MODEL), w_in, w_grp, scale.reshape(1, D_MODEL), w_out, buf0)


def _expand_heads(v):
    return jnp.repeat(v.astype(F32), HEAD_DIM).reshape(1, D_INNER)


def _pad_heads(v):
    return jnp.pad(v.astype(F32), (0, DT_PAD - N_HEADS)).reshape(1, DT_PAD)


def _group_sum_matrix():
    m = np.zeros((N_GROUPS * D_STATE, LANES), np.float32)
    for g in range(N_GROUPS):
        m[g * D_STATE:(g + 1) * D_STATE, g] = 1.0
    return jnp.asarray(m, BF16)


def _group_expand_matrix():
    m = np.zeros((LANES, D_INNER), np.float32)
    for g in range(N_GROUPS):
        m[g, g * GROUP_DIM:(g + 1) * GROUP_DIM] = 1.0
    return jnp.asarray(m, BF16)


def _trunks(x_prompt, x_sample, ssm0_s, conv0_s, pool0_s, p):
    bp, lp, _ = x_prompt.shape
    bs, ls, _ = x_sample.shape
    rows = (bp * lp, bs * ls)
    bf = lambda a: a.astype(BF16)
    w_gate, w_up, w_down = p["ffn_w_gate"], p["ffn_w_up"], p["ffn_w_down"]
    ffn = lambda xs, i, k, **kw: _ffn(xs, rows, p["ffn_norm"][i, k], w_gate, w_up, w_down, (i, k), **kw)
    x2d = ffn([x_prompt.reshape(rows[0], D_MODEL), x_sample.reshape(rows[1], D_MODEL)], 0, 0)

    w_in = p["ssd_w_in"][0]
    w_zxbc = bf(w_in[:, :D_INNER + CONV_DIM])
    w_dt = w_in[:, D_INNER + CONV_DIM:]
    dt_bias = p["ssd_dt_bias"][0]
    wdt = jnp.pad(w_dt, ((0, 0), (0, DT_PAD - N_HEADS)))
    dtb = jnp.pad(dt_bias, (0, DT_PAD - N_HEADS))
    wdt_s = jnp.concatenate([wdt, jnp.repeat(w_dt, HEAD_DIM, axis=1)], axis=1)
    dtb_s = jnp.concatenate([dtb, jnp.repeat(dt_bias, HEAD_DIM)])
    a_neg = -jnp.exp(p["ssd_a_log"][0].astype(F32))
    a_pad, d_e = _pad_heads(a_neg), _expand_heads(p["ssd_d"][0])
    ng = p["ssd_norm"][0].reshape(1, D_INNER)
    common = (p["mix_norm"][0], w_zxbc)
    conv_w, conv_b = p["ssd_conv_w"][0], p["ssd_conv_b"][0]
    z, acum, tr, xbc, conv_p = _ssd_in_prompt(x2d, bp, lp, *common, bf(wdt), conv_w, conv_b, dtb,
                                              jnp.zeros((1, bp, CONV_W - 1, CONV_DIM), F32), a_pad)
    y_p, ssm_p = _scan_prompt(xbc, z, acum, tr, d_e, ng)
    z, xbc, dt, conv_s = _ssd_in(x2d, rows[0], bs, ls, *common, bf(wdt_s), conv_w, conv_b, dtb_s,
                                 conv0_s, nb=32, tl=SAMPLE_LEN)
    y_s, ssm_s = _scan_sample(xbc, z, dt, ssm0_s.reshape(bs, D_INNER, D_STATE), a_pad,
                              _expand_heads(a_neg), d_e, ng, _group_sum_matrix(),
                              _group_expand_matrix())
    x2d = ffn([x2d], 0, 1, pre=([y_p.reshape(rows[0], D_INNER), bf(y_s).reshape(rows[1], D_INNER)],
                                bf(p["ssd_w_out"][0])))

    x2d = ffn([x2d], 1, 0)
    pool_w = (p["mix_norm"][1], bf(p["pool_w_in"][0]), bf(p["pool_w_group"][0]), p["pool_scale"][0],
              bf(p["pool_w_out"][0]))
    xm_p, pool_p = _pool(x2d, 0, bp, lp, *pool_w, jnp.zeros((1, bp, MAX_WIN - 1, D_MODEL), F32),
                         nb=1, tl=1024, pos0=0)
    xm_s, pool_s = _pool(x2d, rows[0], bs, ls, *pool_w, pool0_s, nb=64, tl=SAMPLE_LEN, pos0=PAST_LEN)
    out_p, out_s = ffn([xm_p, xm_s], 1, 1, final_g=p["final_norm"], split_out=True)
    state = lambda a, b: a.reshape(b, N_HEADS, HEAD_DIM, D_STATE)[None]
    return (out_p.reshape(bp, lp, D_MODEL), out_s.reshape(bs, ls, D_MODEL),
            state(ssm_p, bp), conv_p, pool_p, state(ssm_s, bs), conv_s, pool_s)


def kernel(x_prompt, x_sample, state_ssm, state_conv, state_pool, ffn_norm, ffn_w_gate, ffn_w_up,
           ffn_w_down, mix_norm, ssd_w_in, ssd_conv_w, ssd_conv_b, ssd_dt_bias, ssd_a_log, ssd_d,
           ssd_norm, ssd_w_out, pool_w_in, pool_w_group, pool_scale, pool_w_out, final_norm):
    p = dict(ffn_norm=ffn_norm, ffn_w_gate=ffn_w_gate, ffn_w_up=ffn_w_up, ffn_w_down=ffn_w_down,
             mix_norm=mix_norm, ssd_w_in=ssd_w_in, ssd_conv_w=ssd_conv_w, ssd_conv_b=ssd_conv_b,
             ssd_dt_bias=ssd_dt_bias, ssd_a_log=ssd_a_log, ssd_d=ssd_d, ssd_norm=ssd_norm,
             ssd_w_out=ssd_w_out, pool_w_in=pool_w_in, pool_w_group=pool_w_group,
             pool_scale=pool_scale, pool_w_out=pool_w_out, final_norm=final_norm)
    return _trunks(x_prompt, x_sample, state_ssm[0], state_conv, state_pool, p)
```

```python
import functools

import numpy as np
import jax
import jax.numpy as jnp
from jax import lax
from jax.experimental import pallas as pl
from jax.experimental.pallas import tpu as pltpu

F32 = jnp.float32
BF16 = jnp.bfloat16

EPS = 1e-6
D_MODEL = 1024
D_FF = 2816
D_INNER = 2048
HEAD_DIM = 64
N_HEADS = 32
N_GROUPS = 8
HEADS_PER_GROUP = 4
GROUP_DIM = HEADS_PER_GROUP * HEAD_DIM
D_STATE = 128
CONV_W = 4
CONV_DIM = D_INNER + 2 * N_GROUPS * D_STATE
CHUNK = 128
POOL_WINDOWS = (2, 4, 8, 16)
POOL_GROUP_DIM = 256
MAX_WIN = 16
PAST_LEN = 16384
LANES = 128
SUBLANES = 8
DT_PAD = LANES
VMEM_LIMIT = 60 * 1024 * 1024

NT_DIMS = (((1,), (1,)), ((), ()))
LOG2E = 1.4426950408889634


def _resident(shape, lead=()):
    nd = len(shape)
    return pl.BlockSpec((None,) * len(lead) + tuple(shape), lambda *_: tuple(lead) + (0,) * nd,
                        pipeline_mode=pl.Buffered(1))


def _params(n_axes):
    return pltpu.CompilerParams(dimension_semantics=("arbitrary",) * n_axes,
                                vmem_limit_bytes=VMEM_LIMIT)


def _rmsnorm(x, g):
    ms = jnp.mean(x * x, axis=-1, keepdims=True)
    return x * lax.rsqrt(ms + EPS) * g


def _silu(x):
    return x * jax.nn.sigmoid(x)


def _softplus(x):
    return jnp.maximum(x, 0.0) + jnp.log(1.0 + jnp.exp(-jnp.abs(x)))


def _dot(a, b):
    return jnp.dot(a, b, preferred_element_type=F32)


def _dot_nt(a, b):
    return lax.dot_general(a, b, NT_DIMS, preferred_element_type=F32)


def _dot_exact01(t, x):
    hi = x.astype(BF16)
    r1 = x - hi.astype(F32)
    mid = r1.astype(BF16)
    lo = (r1 - mid.astype(F32)).astype(BF16)
    return _dot(t, hi) + _dot(t, mid) + _dot(t, lo)


FFN_TM = 512
FFN_TF = 256


def _ffn_kernel(*refs, n_x, n_y, final, n_out, tiles_a):
    refs = list(refs)
    x_refs = [refs.pop(0) for _ in range(n_x)]
    y_refs = [refs.pop(0) for _ in range(n_y)]
    wpre_ref = refs.pop(0) if n_y else None
    g_ref, wg_ref, wu_ref, wd_ref = refs[:4]
    del refs[:4]
    fg_ref = refs.pop(0) if final else None
    o_refs = [refs.pop(0) for _ in range(n_out)]
    h_ref, a_ref = refs
    second = pl.program_id(0) >= tiles_a

    def pick(rs, dtype):
        if len(rs) == 1:
            return rs[0][...].astype(dtype)
        return jnp.where(second, rs[1][...].astype(dtype), rs[0][...].astype(dtype))

    x = pick(x_refs, F32)
    if n_y:
        x = x + _dot(pick(y_refs, BF16), wpre_ref[...])
    h_ref[...] = _rmsnorm(x, g_ref[...]).astype(BF16)
    for f in range(0, D_FF, FFN_TF):
        h = h_ref[...]
        gate = _dot(h, wg_ref[:, f:f + FFN_TF].astype(BF16))
        up = _dot(h, wu_ref[:, f:f + FFN_TF].astype(BF16))
        a_ref[:, f:f + FFN_TF] = (_silu(gate) * up).astype(BF16)
    y = x + 0.5 * _dot(a_ref[...], wd_ref[...].astype(BF16))
    if final:
        y = _rmsnorm(y, fg_ref[...])
    if n_out == 1:
        o_refs[0][...] = y
    else:
        @pl.when(jnp.logical_not(second))
        def _():
            o_refs[0][...] = y

        @pl.when(second)
        def _():
            o_refs[1][...] = y


def _ffn(xs, rows, g, w_gate, w_up, w_down, idx, final_g=None, pre=None, split_out=False):
    tm = FFN_TM
    tiles_a, tiles_b = rows[0] // tm, rows[1] // tm
    assert rows[0] % tm == 0 and rows[1] % tm == 0
    final = final_g is not None

    def row_specs(arrs):
        if len(arrs) == 1:
            return [pl.BlockSpec((tm, arrs[0].shape[1]), lambda i: (i, 0))]
        return [pl.BlockSpec((tm, arrs[0].shape[1]), lambda i: (jnp.minimum(i, tiles_a - 1), 0)),
                pl.BlockSpec((tm, arrs[1].shape[1]), lambda i: (jnp.maximum(i - tiles_a, 0), 0))]

    in_specs, args = row_specs(xs), list(xs)
    ys = []
    if pre is not None:
        ys, wpre = pre
        in_specs += row_specs(ys) + [_resident(wpre.shape)]
        args += list(ys) + [wpre]
    in_specs += [_resident((1, D_MODEL)), _resident((D_MODEL, D_FF), idx),
                 _resident((D_MODEL, D_FF), idx), _resident((D_FF, D_MODEL), idx)]
    args += [g.reshape(1, D_MODEL), w_gate, w_up, w_down]
    if final:
        in_specs.append(_resident((1, D_MODEL)))
        args.append(final_g.reshape(1, D_MODEL))
    if split_out:
        outs = [jax.ShapeDtypeStruct((r, D_MODEL), F32) for r in rows]
    else:
        outs = [jax.ShapeDtypeStruct((rows[0] + rows[1], D_MODEL), F32)]
    res = pl.pallas_call(
        functools.partial(_ffn_kernel, n_x=len(xs), n_y=len(ys), final=final, n_out=len(outs),
                          tiles_a=tiles_a),
        grid=(tiles_a + tiles_b,),
        in_specs=in_specs,
        out_specs=row_specs(outs),
        out_shape=outs,
        scratch_shapes=[pltpu.VMEM((tm, D_MODEL), BF16), pltpu.VMEM((tm, D_FF), BF16)],
        compiler_params=_params(1),
        name="ffn" + ("_pre" if pre is not None else "") + ("_final" if final else ""),
    )(*args)
    return res if split_out else res[0]


CONV_HALO = SUBLANES
SSD_IN_LANE_CHUNK = 512


def _shift_rows(v, k):
    if k == SUBLANES:
        return jnp.concatenate([v[:, :1], v[:, :-1]], axis=1)
    r = pltpu.roll(v, k, 2)
    prev = jnp.concatenate([r[:, :1], r[:, :-1]], axis=1)
    sub = lax.broadcasted_iota(jnp.int32, v.shape, 2)
    return jnp.where(sub < k, prev, r)


def _chunk_decays(dt, a_pad):
    q = dt.shape[0]
    tril = (lax.broadcasted_iota(jnp.int32, (q, q), 0) >= lax.broadcasted_iota(jnp.int32, (q, q), 1))
    acum = _dot_exact01(tril.astype(BF16), dt * a_pad) * LOG2E
    acum_t = acum.T
    dt_t = dt.T
    src_t = acum_t - jnp.log2(dt_t)
    w_t = jnp.exp2(acum_t[:, q - 1:q] - acum_t) * dt_t
    return acum, acum_t, src_t, w_t


def _conv_silu(ext, cw_ref, cb_ref, sl):
    w = ext.shape[-1]
    tap = lambda k: cw_ref[k:k + 1, sl].reshape(1, 1, 1, w)
    ext1 = _shift_rows(ext, 1)
    p = ext * tap(3) + ext1 * tap(2)
    q = ext * tap(1) + ext1 * tap(0)
    return _silu((cb_ref[:, sl].reshape(1, 1, 1, w) + p + _shift_rows(q, 2))[:, 1:])


def _ssd_in_kernel(x_ref, g_ref, win_ref, wdt_ref, cw_ref, cb_ref, dtb_ref, conv0_ref,
                   z_ref, xbc_ref, dt_ref, convnew_ref, halo_ref, h_ref, *, nb, tl):
    rows = nb * tl
    wc = SSD_IN_LANE_CHUNK
    nblk = tl // SUBLANES

    @pl.when(pl.program_id(1) == 0)
    def _():
        halo_ref[...] = jnp.zeros_like(halo_ref)
        halo_ref[:, CONV_HALO - (CONV_W - 1):, :] = conv0_ref[...]

    h_ref[...] = _rmsnorm(x_ref[...], g_ref[...]).astype(BF16)
    dt = _softplus(_dot(h_ref[...], wdt_ref[...]) + dtb_ref[...])
    dt_ref[...] = dt.reshape(nb, tl, dt.shape[-1])
    for c in range(0, CONV_DIM, wc):
        sl = slice(c, c + wc)
        zs = slice(c // 2, c // 2 + wc // 2)
        z_ref[:, :, zs] = _dot(h_ref[...], win_ref[:, zs]).reshape(nb, tl, wc // 2)
        cur = _dot(h_ref[...], win_ref[:, D_INNER + c:D_INNER + c + wc]).reshape(nb, tl, wc)
        ext = jnp.concatenate([halo_ref[:, :, sl], cur], axis=1).reshape(nb, nblk + 1, SUBLANES, wc)
        xbc_ref[:, :, sl] = _conv_silu(ext, cw_ref, cb_ref, sl).reshape(nb, tl, wc)
        halo_ref[:, :, sl] = cur[:, tl - CONV_HALO:, :]
    convnew_ref[...] = halo_ref[:, CONV_HALO - (CONV_W - 1):, :]


def _ssd_in(x2d, row0, b, l, g, w_in, wdt, cw, cb, dtb, conv0, *, nb, tl):
    assert l == tl and row0 % (nb * tl) == 0
    tile0 = row0 // (nb * tl)
    dt_w = wdt.shape[1]
    blk = lambda w: pl.BlockSpec((nb, tl, w), lambda i, j: (i, j, 0))
    return pl.pallas_call(
        functools.partial(_ssd_in_kernel, nb=nb, tl=tl),
        grid=(b // nb, l // tl),
        in_specs=[pl.BlockSpec((nb * tl, D_MODEL), lambda i, j: (tile0 + i, 0)),
                  _resident((1, D_MODEL)), _resident(w_in.shape), _resident((D_MODEL, dt_w)),
                  _resident((CONV_W, CONV_DIM)), _resident((1, CONV_DIM)), _resident((1, dt_w)),
                  pl.BlockSpec((nb, CONV_W - 1, CONV_DIM), lambda i, j: (i, 0, 0))],
        out_specs=[blk(D_INNER), blk(CONV_DIM), blk(dt_w),
                   pl.BlockSpec((nb, CONV_W - 1, CONV_DIM), lambda i, j: (i, 0, 0))],
        out_shape=[jax.ShapeDtypeStruct((b, l, D_INNER), F32),
                   jax.ShapeDtypeStruct((b, l, CONV_DIM), F32),
                   jax.ShapeDtypeStruct((b, l, dt_w), F32),
                   jax.ShapeDtypeStruct((b, CONV_W - 1, CONV_DIM), F32)],
        scratch_shapes=[pltpu.VMEM((nb, CONV_HALO, CONV_DIM), F32),
                        pltpu.VMEM((nb * tl, D_MODEL), BF16)],
        compiler_params=_params(2),
        name="ssd_in",
    )(x2d, g.reshape(1, D_MODEL), w_in, wdt, cw, cb.reshape(1, CONV_DIM),
      dtb.reshape(1, dt_w), conv0)


SSD_PIPE_TL = 512


def _ssd_in_prompt_kernel(x_ref, g_ref, win_ref, wdt_ref, cw_ref, cb_ref, dtb_ref, conv0_ref,
                          apad_ref, z_ref, acum_ref, tr_ref, xbc_ref, convnew_ref, halo_ref, h_ref):
    tl = SSD_PIPE_TL
    wc = SSD_IN_LANE_CHUNK
    nblk = tl // SUBLANES

    @pl.when(pl.program_id(1) == 0)
    def _():
        halo_ref[...] = jnp.zeros_like(halo_ref)
        halo_ref[CONV_HALO - (CONV_W - 1):, :] = conv0_ref[0]

    h_ref[...] = _rmsnorm(x_ref[...], g_ref[...]).astype(BF16)
    dt = _softplus(_dot(h_ref[...], wdt_ref[...]) + dtb_ref[...])
    for j in range(tl // CHUNK):
        cs = slice(j * CHUNK, (j + 1) * CHUNK)
        acum, acum_t, src_t, w_t = _chunk_decays(dt[cs, :], apad_ref[...])
        acum_ref[0, cs, :] = acum
        tr_ref[0, j, 0] = acum_t
        tr_ref[0, j, 1] = src_t
        tr_ref[0, j, 2] = w_t
    for c in range(0, CONV_DIM, wc):
        sl = slice(c, c + wc)
        zs = slice(c // 2, c // 2 + wc // 2)
        z_ref[0, :, zs] = _dot(h_ref[...], win_ref[:, zs]).astype(z_ref.dtype)
        cur = _dot(h_ref[...], win_ref[:, D_INNER + c:D_INNER + c + wc])
        ext = jnp.concatenate([halo_ref[:, sl], cur], axis=0).reshape(1, nblk + 1, SUBLANES, wc)
        xbc_ref[0, :, sl] = _conv_silu(ext, cw_ref, cb_ref, sl).reshape(tl, wc)
        halo_ref[:, sl] = cur[tl - CONV_HALO:, :]
    convnew_ref[0] = halo_ref[CONV_HALO - (CONV_W - 1):, :]


def _ssd_in_prompt(x2d, b, l, g, w_in, wdt, cw, cb, dtb, conv0, a_pad):
    tl = SSD_PIPE_TL
    tps = l // tl
    ncs = tl // CHUNK
    blk = lambda w: pl.BlockSpec((1, tl, w), lambda i, j: (i, j, 0))
    return pl.pallas_call(
        _ssd_in_prompt_kernel,
        grid=(b, l // tl),
        in_specs=[pl.BlockSpec((tl, D_MODEL), lambda i, j: (i * tps + j, 0)),
                  _resident((1, D_MODEL)), _resident(w_in.shape), _resident((D_MODEL, DT_PAD)),
                  _resident((CONV_W, CONV_DIM)), _resident((1, CONV_DIM)), _resident((1, DT_PAD)),
                  pl.BlockSpec((1, CONV_W - 1, CONV_DIM), lambda i, j: (i, 0, 0)),
                  _resident((1, DT_PAD))],
        out_specs=[blk(D_INNER), blk(DT_PAD),
                   pl.BlockSpec((1, ncs, 3, DT_PAD, CHUNK), lambda i, j: (i, j, 0, 0, 0)),
                   blk(CONV_DIM),
                   pl.BlockSpec((1, CONV_W - 1, CONV_DIM), lambda i, j: (i, 0, 0))],
        out_shape=[jax.ShapeDtypeStruct((b, l, D_INNER), BF16),
                   jax.ShapeDtypeStruct((b, l, DT_PAD), F32),
                   jax.ShapeDtypeStruct((b, l // CHUNK, 3, DT_PAD, CHUNK), F32),
                   jax.ShapeDtypeStruct((b, l, CONV_DIM), F32),
                   jax.ShapeDtypeStruct((b, CONV_W - 1, CONV_DIM), F32)],
        scratch_shapes=[pltpu.VMEM((CONV_HALO, CONV_DIM), F32), pltpu.VMEM((tl, D_MODEL), BF16)],
        compiler_params=_params(2),
        name="ssd_in_prompt",
    )(x2d, g.reshape(1, D_MODEL), w_in, wdt, cw, cb.reshape(1, CONV_DIM),
      dtb.reshape(1, DT_PAD), conv0, a_pad)


B_OFF = D_INNER
C_OFF = D_INNER + N_GROUPS * D_STATE


def _gate_norm(y, z, ng):
    yg = y * _silu(z)
    ms = jnp.mean(yg * yg, axis=-1, keepdims=True)
    return yg * lax.rsqrt(ms + EPS) * ng


def _head_rows(mat, g):
    return jnp.concatenate(
        [jnp.broadcast_to(mat[g * HEADS_PER_GROUP + r:g * HEADS_PER_GROUP + r + 1, :],
                          (HEAD_DIM, mat.shape[1])) for r in range(HEADS_PER_GROUP)], axis=0)


def _scan_prompt_kernel(xbc_ref, z_ref, acum_ref, tr_ref, de_ref, ng_ref, y_ref, hout_ref, *h_refs,
                        chunks):
    q = CHUNK
    step = pl.program_id(1)

    @pl.when(step == 0)
    def _():
        for h_ref in h_refs:
            h_ref[...] = jnp.zeros_like(h_ref)

    tril = (lax.broadcasted_iota(jnp.int32, (q, q), 0) >= lax.broadcasted_iota(jnp.int32, (q, q), 1))
    lane_g = lax.broadcasted_iota(jnp.int32, (q, GROUP_DIM), 1)
    lane_half = lax.broadcasted_iota(jnp.int32, (q, LANES), 1) < HEAD_DIM

    def chunk_body(groups, ci, carry):
        ts = pl.ds(pl.multiple_of(ci * q, q), q)
        acum = acum_ref[0, ts, :]
        acum_t, src_t, w_t = tr_ref[0, ci, 0], tr_ref[0, ci, 1], tr_ref[0, ci, 2]
        cd_b = jnp.broadcast_to(jnp.exp2(acum_t[:, q - 1:q]), (DT_PAD, D_STATE))
        for g in groups:
            gs = slice(g * GROUP_DIM, (g + 1) * GROUP_DIM)
            bg = xbc_ref[0, ts, B_OFF + g * D_STATE:B_OFF + (g + 1) * D_STATE].astype(BF16)
            cg = xbc_ref[0, ts, C_OFF + g * D_STATE:C_OFF + (g + 1) * D_STATE].astype(BF16)
            xg = xbc_ref[0, ts, gs]
            xgb = xg.astype(BF16)
            cb = _dot_nt(cg, bg)
            ms, cols = [], []
            for r in range(HEADS_PER_GROUP):
                hd = g * HEADS_PER_GROUP + r
                colf = jnp.broadcast_to(acum[:, hd:hd + 1], (q, q))
                rowf = jnp.broadcast_to(src_t[hd:hd + 1, :], (q, q))
                ms.append((cb * jnp.exp2(jnp.where(tril, colf - rowf, -jnp.inf))).astype(BF16))
                cols.append(colf)
            zero = jnp.zeros_like(xgb)
            y_diag = sum(
                _dot(ms[r], jnp.where((lane_g >= r * HEAD_DIM) & (lane_g < (r + 1) * HEAD_DIM),
                                      xgb, zero)) for r in range(HEADS_PER_GROUP))
            hg = h_refs[g][...]
            y_off = _dot_nt(cg, hg.astype(BF16))
            acum_e = jnp.concatenate([jnp.where(lane_half, cols[0], cols[1]),
                                      jnp.where(lane_half, cols[2], cols[3])], axis=1)
            y = y_diag + y_off * jnp.exp2(acum_e) + de_ref[:, gs] * xg
            xg_t = xg.T
            wt = jnp.concatenate(
                [xg_t[r * HEAD_DIM:(r + 1) * HEAD_DIM, :] * w_t[g * HEADS_PER_GROUP + r:
                                                                g * HEADS_PER_GROUP + r + 1, :]
                 for r in range(HEADS_PER_GROUP)], axis=0).astype(BF16)
            h_refs[g][...] = hg * _head_rows(cd_b, g) + _dot(wt, bg)
            zg = z_ref[0, ts, gs].astype(F32)
            y_ref[0, ts, gs] = _gate_norm(y, zg, ng_ref[:, gs]).astype(y_ref.dtype)
        return carry

    for g0 in range(0, N_GROUPS, SCAN_GROUPS_PER_PASS):
        groups = range(g0, g0 + SCAN_GROUPS_PER_PASS)
        lax.fori_loop(0, chunks, functools.partial(chunk_body, groups), 0)

    @pl.when(step == pl.num_programs(1) - 1)
    def _():
        for g, h_ref in enumerate(h_refs):
            hout_ref[0, g * GROUP_DIM:(g + 1) * GROUP_DIM, :] = h_ref[...]


SCAN_CHUNKS_PER_STEP = 4
SCAN_GROUPS_PER_PASS = 4


def _scan_prompt(xbc, z, acum, tr, d_e, ng):
    b, l, _ = xbc.shape
    chunks = SCAN_CHUNKS_PER_STEP
    tl = chunks * CHUNK
    blk = lambda w: pl.BlockSpec((1, tl, w), lambda i, j: (i, j, 0))
    return pl.pallas_call(
        functools.partial(_scan_prompt_kernel, chunks=chunks),
        grid=(b, l // tl),
        in_specs=[blk(CONV_DIM), blk(D_INNER), blk(DT_PAD),
                  pl.BlockSpec((1, chunks, 3, DT_PAD, CHUNK), lambda i, j: (i, j, 0, 0, 0)),
                  _resident((1, D_INNER)), _resident((1, D_INNER))],
        out_specs=[blk(D_INNER), pl.BlockSpec((1, D_INNER, D_STATE), lambda i, j: (i, 0, 0))],
        out_shape=[jax.ShapeDtypeStruct((b, l, D_INNER), BF16),
                   jax.ShapeDtypeStruct((b, D_INNER, D_STATE), F32)],
        scratch_shapes=[pltpu.VMEM((GROUP_DIM, D_STATE), F32)] * N_GROUPS,
        compiler_params=_params(2),
        name="ssd_scan_prompt",
    )(xbc, z, acum, tr, d_e, ng)


SAMPLE_LEN = 8
SCAN_NB = 16
SCAN_ROWS = SCAN_NB * SAMPLE_LEN
SCAN_STEP_NB = 8


def _scan_sample_kernel(xbc_ref, z_ref, dt_ref, h0_ref, apad_ref, ae_ref, de_ref, ng_ref,
                        gsum_ref, gexp_ref, y_ref, hout_ref,
                        wt_ref, ea_ref, ac_ref, xdt_ref, acp_ref, p_ref):
    rows = SCAN_ROWS
    srows = SCAN_STEP_NB * SAMPLE_LEN
    j = pl.program_id(1)

    def bcast_token(v, s):
        n, w = v.shape[0] // SAMPLE_LEN, v.shape[-1]
        v3 = v.reshape(n, SAMPLE_LEN, w)
        return jnp.broadcast_to(v3[:, s:s + 1, :], (n, SAMPLE_LEN, w)).reshape(n * SAMPLE_LEN, w)

    @pl.when(j == 0)
    def _():
        tok = lax.broadcasted_iota(jnp.int32, (rows, D_INNER), 0) & (SAMPLE_LEN - 1)
        tok_p = lax.broadcasted_iota(jnp.int32, (rows, DT_PAD), 0) & (SAMPLE_LEN - 1)

        def cumsum_tokens(v, t):
            for sh in (1, 2, 4):
                v = v + jnp.where(t >= sh, pltpu.roll(v, sh, 0), 0.0)
            return v

        xs = xbc_ref[:, :, 0:D_INNER].reshape(rows, D_INNER)
        dt_p = dt_ref[:, :, 0:DT_PAD].reshape(rows, DT_PAD)
        dt_e = dt_ref[:, :, DT_PAD:DT_PAD + D_INNER].reshape(rows, D_INNER)
        acp_ref[...] = cumsum_tokens(dt_p * apad_ref[...], tok_p)
        acum_e = cumsum_tokens(dt_e * ae_ref[...], tok)
        xdt = xs * dt_e
        w = xdt * jnp.exp(bcast_token(acum_e, SAMPLE_LEN - 1) - acum_e)
        wt_ref[...] = w.T.astype(BF16)
        ea_ref[...] = jnp.exp(acum_e)
        ac_ref[...] = acum_e
        xdt_ref[...] = xdt

    seq0 = j * SCAN_STEP_NB
    step_rows = pl.ds(pl.multiple_of(seq0 * SAMPLE_LEN, srows), srows)
    step_seqs = pl.ds(seq0, SCAN_STEP_NB)
    tok = lax.broadcasted_iota(jnp.int32, (srows, D_INNER), 0) & (SAMPLE_LEN - 1)
    xs = xbc_ref[step_seqs, :, 0:D_INNER].reshape(srows, D_INNER)
    bm = xbc_ref[step_seqs, :, B_OFF:C_OFF].reshape(srows, N_GROUPS * D_STATE)
    cm = xbc_ref[step_seqs, :, C_OFF:CONV_DIM].reshape(srows, N_GROUPS * D_STATE)
    acum_e = ac_ref[step_rows, :]
    xdt = xdt_ref[step_rows, :]
    for s in range(SAMPLE_LEN):
        p_ref[s * srows:(s + 1) * srows, :] = (cm * bcast_token(bm, s)).astype(BF16)
    cb_sum = _dot(p_ref[...], gsum_ref[...])
    cb_e = _dot(cb_sum.astype(BF16), gexp_ref[...])
    yd = de_ref[...] * xs
    for s in range(SAMPLE_LEN):
        diff = acum_e - bcast_token(acum_e, s)
        decay = jnp.exp(jnp.where(tok >= s, diff, -jnp.inf))
        yd = yd + cb_e[s * srows:(s + 1) * srows, :] * decay * bcast_token(xdt, s)

    eye = (lax.broadcasted_iota(jnp.int32, (N_HEADS, DT_PAD), 0)
           == lax.broadcasted_iota(jnp.int32, (N_HEADS, DT_PAD), 1))
    rowid = lax.broadcasted_iota(jnp.int32, (rows, D_STATE), 0)
    for t in range(SCAN_STEP_NB):
        seq = seq0 + t
        r0 = pl.multiple_of(seq * SAMPLE_LEN, SAMPLE_LEN)
        alast = acp_ref[pl.ds(r0 + SAMPLE_LEN - 1, 1), :]
        alast_col = jnp.sum(jnp.where(eye, jnp.broadcast_to(alast, (N_HEADS, DT_PAD)), 0.0),
                            axis=1, keepdims=True)
        cd_b = jnp.broadcast_to(jnp.exp(alast_col), (N_HEADS, D_STATE))
        mine = (rowid >= r0) & (rowid < r0 + SAMPLE_LEN)
        y_offs = []
        for g in range(N_GROUPS):
            gs = slice(g * GROUP_DIM, (g + 1) * GROUP_DIM)
            hg = h0_ref[t, gs, :]
            cg = xbc_ref[seq, :, C_OFF + g * D_STATE:C_OFF + (g + 1) * D_STATE].astype(BF16)
            y_offs.append(_dot_nt(cg, hg.astype(BF16)))
            b_all = xbc_ref[:, :, B_OFF + g * D_STATE:B_OFF + (g + 1) * D_STATE].reshape(
                rows, D_STATE)
            b_mine = jnp.where(mine, b_all, 0.0).astype(BF16)
            hout_ref[t, gs, :] = hg * _head_rows(cd_b, g) + _dot(wt_ref[gs, :], b_mine)
        y = yd[t * SAMPLE_LEN:(t + 1) * SAMPLE_LEN, :] + jnp.concatenate(y_offs, axis=1) * ea_ref[
            pl.ds(r0, SAMPLE_LEN), :]
        z = z_ref[seq]
        y_ref[seq] = jnp.concatenate(
            [_gate_norm(y[:, g * GROUP_DIM:(g + 1) * GROUP_DIM],
                        z[:, g * GROUP_DIM:(g + 1) * GROUP_DIM],
                        ng_ref[:, g * GROUP_DIM:(g + 1) * GROUP_DIM]) for g in range(N_GROUPS)],
            axis=1)


def _scan_sample(xbc, z, dt, h0, a_pad, a_e, d_e, ng, gsum, gexp):
    b = xbc.shape[0]
    dt_w = dt.shape[-1]
    blk = lambda w: pl.BlockSpec((SCAN_NB, SAMPLE_LEN, w), lambda i, j: (i, 0, 0))
    steps = SCAN_NB // SCAN_STEP_NB
    st = pl.BlockSpec((SCAN_STEP_NB, D_INNER, D_STATE), lambda i, j: (i * steps + j, 0, 0))
    return pl.pallas_call(
        _scan_sample_kernel,
        grid=(b // SCAN_NB, steps),
        in_specs=[blk(CONV_DIM), blk(D_INNER), blk(dt_w), st, _resident((1, DT_PAD)),
                  _resident((1, D_INNER)), _resident((1, D_INNER)), _resident((1, D_INNER)),
                  _resident(gsum.shape), _resident(gexp.shape)],
        out_specs=[blk(D_INNER), st],
        out_shape=[jax.ShapeDtypeStruct((b, SAMPLE_LEN, D_INNER), F32),
                   jax.ShapeDtypeStruct((b, D_INNER, D_STATE), F32)],
        scratch_shapes=[pltpu.VMEM((D_INNER, SCAN_ROWS), BF16),
                        pltpu.VMEM((SCAN_ROWS, D_INNER), F32),
                        pltpu.VMEM((SCAN_ROWS, D_INNER), F32),
                        pltpu.VMEM((SCAN_ROWS, D_INNER), F32),
                        pltpu.VMEM((SCAN_ROWS, DT_PAD), F32),
                        pltpu.VMEM((SAMPLE_LEN * SCAN_STEP_NB * SAMPLE_LEN, N_GROUPS * D_STATE),
                                   BF16)],
        compiler_params=_params(2),
        name="ssd_scan_sample",
    )(xbc, z, dt, h0, a_pad, a_e, d_e, ng, gsum, gexp)


POOL_HALO = 2 * SUBLANES


def _pool_kernel(x_ref, g_ref, win_ref, wgrp_ref, scale_ref, wout_ref, buf0_ref,
                 o_ref, bufnew_ref, ext_ref, *, nb, tl, pos0):
    rows = nb * tl
    jt = pl.program_id(1)

    @pl.when(jt == 0)
    def _():
        ext_ref[:, 0:POOL_HALO, :] = jnp.zeros((nb, POOL_HALO, D_MODEL), F32)
        ext_ref[:, POOL_HALO - (MAX_WIN - 1):POOL_HALO, :] = buf0_ref[...]

    x = x_ref[...]
    h = _rmsnorm(x, g_ref[...]).astype(BF16)
    ext_ref[:, POOL_HALO:POOL_HALO + tl, :] = _dot(h, win_ref[...]).reshape(nb, tl, D_MODEL)
    nblk = tl // SUBLANES
    halo_blocks = POOL_HALO // SUBLANES
    shape4 = (nb, nblk, SUBLANES, POOL_GROUP_DIM)
    pos = (pos0 + jt * tl + SUBLANES * lax.broadcasted_iota(jnp.int32, shape4, 1)
           + lax.broadcasted_iota(jnp.int32, shape4, 2)).astype(F32)
    out = x
    for k, w in enumerate(POOL_WINDOWS):
        sl = slice(k * POOL_GROUP_DIM, (k + 1) * POOL_GROUP_DIM)
        ext = ext_ref[:, :, sl].reshape(nb, nblk + halo_blocks, SUBLANES, POOL_GROUP_DIM)
        tot, shift = ext, 1
        while shift < w:
            tot = tot + _shift_rows(tot, shift)
            shift *= 2
        u = ext[:, halo_blocks:]
        mean = tot[:, halo_blocks:] / jnp.minimum(jnp.float32(w), pos + 1.0)
        m = (mean - u).reshape(rows, POOL_GROUP_DIM).astype(BF16)
        mixed = (_dot(m, wgrp_ref[k]) * scale_ref[:, sl]).astype(BF16)
        out = out + _dot(mixed, wout_ref[sl, :])
    o_ref[...] = out
    bufnew_ref[...] = ext_ref[:, tl + 1:tl + POOL_HALO, :]
    ext_ref[:, 0:POOL_HALO, :] = ext_ref[:, tl:tl + POOL_HALO, :]


def _pool(x2d, row0, b, l, g, w_in, w_grp, scale, w_out, buf0, *, nb, tl, pos0):
    assert row0 % (nb * tl) == 0 and (nb == 1 or l == tl)
    tile0, tps = row0 // (nb * tl), l // tl
    return pl.pallas_call(
        functools.partial(_pool_kernel, nb=nb, tl=tl, pos0=pos0),
        grid=(b // nb, tps),
        in_specs=[pl.BlockSpec((nb * tl, D_MODEL), lambda i, j: (tile0 + i * tps + j, 0)),
                  _resident((1, D_MODEL)), _resident((D_MODEL, D_MODEL)),
                  _resident(w_grp.shape), _resident((1, D_MODEL)), _resident((D_MODEL, D_MODEL)),
                  pl.BlockSpec((nb, MAX_WIN - 1, D_MODEL), lambda i, j: (i, 0, 0))],
        out_specs=[pl.BlockSpec((nb * tl, D_MODEL), lambda i, j: (i * tps + j, 0)),
                   pl.BlockSpec((nb, MAX_WIN - 1, D_MODEL), lambda i, j: (i, 0, 0))],
        out_shape=[jax.ShapeDtypeStruct((b * l, D_MODEL), F32),
                   jax.ShapeDtypeStruct((b, MAX_WIN - 1, D_MODEL), F32)],
        scratch_shapes=[pltpu.VMEM((nb, POOL_HALO + tl, D_MODEL), F32)],
        compiler_params=_params(2),
        name="pool_mixer",
    )(x2d, g.reshape(1, D_MODEL), w_in, w_grp, scale.reshape(1, D_MODEL), w_out, buf0)


def _expand_heads(v):
    return jnp.repeat(v.astype(F32), HEAD_DIM).reshape(1, D_INNER)


def _pad_heads(v):
    return jnp.pad(v.astype(F32), (0, DT_PAD - N_HEADS)).reshape(1, DT_PAD)


def _group_sum_matrix():
    m = np.zeros((N_GROUPS * D_STATE, LANES), np.float32)
    for g in range(N_GROUPS):
        m[g * D_STATE:(g + 1) * D_STATE, g] = 1.0
    return jnp.asarray(m, BF16)


def _group_expand_matrix():
    m = np.zeros((LANES, D_INNER), np.float32)
    for g in range(N_GROUPS):
        m[g, g * GROUP_DIM:(g + 1) * GROUP_DIM] = 1.0
    return jnp.asarray(m, BF16)


def _trunks(x_prompt, x_sample, ssm0_s, conv0_s, pool0_s, p):
    bp, lp, _ = x_prompt.shape
    bs, ls, _ = x_sample.shape
    rows = (bp * lp, bs * ls)
    bf = lambda a: a.astype(BF16)
    w_gate, w_up, w_down = p["ffn_w_gate"], p["ffn_w_up"], p["ffn_w_down"]
    ffn = lambda xs, i, k, **kw: _ffn(xs, rows, p["ffn_norm"][i, k], w_gate, w_up, w_down, (i, k), **kw)
    x2d = ffn([x_prompt.reshape(rows[0], D_MODEL), x_sample.reshape(rows[1], D_MODEL)], 0, 0)

    w_in = p["ssd_w_in"][0]
    w_dt = w_in[:, D_INNER + CONV_DIM:]
    dt_bias = p["ssd_dt_bias"][0]
    wdt = jnp.pad(w_dt, ((0, 0), (0, DT_PAD - N_HEADS)))
    dtb = jnp.pad(dt_bias, (0, DT_PAD - N_HEADS))
    wdt_s = jnp.concatenate([wdt, jnp.repeat(w_dt, HEAD_DIM, axis=1)], axis=1)
    dtb_s = jnp.concatenate([dtb, jnp.repeat(dt_bias, HEAD_DIM)])
    a_neg = -jnp.exp(p["ssd_a_log"][0].astype(F32))
    a_pad, d_e = _pad_heads(a_neg), _expand_heads(p["ssd_d"][0])
    ng = p["ssd_norm"][0].reshape(1, D_INNER)
    common = (p["mix_norm"][0], bf(w_in))
    conv_w, conv_b = p["ssd_conv_w"][0], p["ssd_conv_b"][0]
    z, acum, tr, xbc, conv_p = _ssd_in_prompt(x2d, bp, lp, *common, bf(wdt), conv_w, conv_b, dtb,
                                              jnp.zeros((bp, CONV_W - 1, CONV_DIM), F32), a_pad)
    y_p, ssm_p = _scan_prompt(xbc, z, acum, tr, d_e, ng)
    z, xbc, dt, conv_s = _ssd_in(x2d, rows[0], bs, ls, *common, bf(wdt_s), conv_w, conv_b, dtb_s,
                                 conv0_s, nb=32, tl=SAMPLE_LEN)
    y_s, ssm_s = _scan_sample(xbc, z, dt, ssm0_s.reshape(bs, D_INNER, D_STATE), a_pad,
                              _expand_heads(a_neg), d_e, ng, _group_sum_matrix(),
                              _group_expand_matrix())
    x2d = ffn([x2d], 0, 1, pre=([y_p.reshape(rows[0], D_INNER), bf(y_s).reshape(rows[1], D_INNER)],
                                bf(p["ssd_w_out"][0])))

    x2d = ffn([x2d], 1, 0)
    pool_w = (p["mix_norm"][1], bf(p["pool_w_in"][0]), bf(p["pool_w_group"][0]), p["pool_scale"][0],
              bf(p["pool_w_out"][0]))
    xm_p, pool_p = _pool(x2d, 0, bp, lp, *pool_w, jnp.zeros((bp, MAX_WIN - 1, D_MODEL), F32),
                         nb=1, tl=1024, pos0=0)
    xm_s, pool_s = _pool(x2d, rows[0], bs, ls, *pool_w, pool0_s, nb=64, tl=SAMPLE_LEN, pos0=PAST_LEN)
    out_p, out_s = ffn([xm_p, xm_s], 1, 1, final_g=p["final_norm"], split_out=True)
    state = lambda a, b: a.reshape(b, N_HEADS, HEAD_DIM, D_STATE)[None]
    return (out_p.reshape(bp, lp, D_MODEL), out_s.reshape(bs, ls, D_MODEL),
            state(ssm_p, bp), conv_p[None], pool_p[None], state(ssm_s, bs), conv_s[None], pool_s[None])


def kernel(x_prompt, x_sample, state_ssm, state_conv, state_pool, ffn_norm, ffn_w_gate, ffn_w_up,
           ffn_w_down, mix_norm, ssd_w_in, ssd_conv_w, ssd_conv_b, ssd_dt_bias, ssd_a_log, ssd_d,
           ssd_norm, ssd_w_out, pool_w_in, pool_w_group, pool_scale, pool_w_out, final_norm):
    p = dict(ffn_norm=ffn_norm, ffn_w_gate=ffn_w_gate, ffn_w_up=ffn_w_up, ffn_w_down=ffn_w_down,
             mix_norm=mix_norm, ssd_w_in=ssd_w_in, ssd_conv_w=ssd_conv_w, ssd_conv_b=ssd_conv_b,
             ssd_dt_bias=ssd_dt_bias, ssd_a_log=ssd_a_log, ssd_d=ssd_d, ssd_norm=ssd_norm,
             ssd_w_out=ssd_w_out, pool_w_in=pool_w_in, pool_w_group=pool_w_group,
             pool_scale=pool_scale, pool_w_out=pool_w_out, final_norm=final_norm)
    return _trunks(x_prompt, x_sample, state_ssm[0], state_conv[0], state_pool[0], p)
```

```python
import functools

import numpy as np
import jax
import jax.numpy as jnp
from jax import lax
from jax.experimental import pallas as pl
from jax.experimental.pallas import tpu as pltpu

F32 = jnp.float32
BF16 = jnp.bfloat16

EPS = 1e-6
D_MODEL = 1024
D_FF = 2816
D_INNER = 2048
HEAD_DIM = 64
N_HEADS = 32
N_GROUPS = 8
HEADS_PER_GROUP = 4
GROUP_DIM = HEADS_PER_GROUP * HEAD_DIM
D_STATE = 128
CONV_W = 4
CONV_DIM = D_INNER + 2 * N_GROUPS * D_STATE
CHUNK = 128
POOL_WINDOWS = (2, 4, 8, 16)
POOL_GROUP_DIM = 256
MAX_WIN = 16
PAST_LEN = 16384
LANES = 128
SUBLANES = 8
DT_PAD = LANES
VMEM_LIMIT = 60 * 1024 * 1024

NT_DIMS = (((1,), (1,)), ((), ()))
LOG2E = 1.4426950408889634


def _resident(shape, lead=()):
    nd = len(shape)
    return pl.BlockSpec((None,) * len(lead) + tuple(shape), lambda *_: tuple(lead) + (0,) * nd,
                        pipeline_mode=pl.Buffered(1))


def _params(n_axes):
    return pltpu.CompilerParams(dimension_semantics=("arbitrary",) * n_axes,
                                vmem_limit_bytes=VMEM_LIMIT)


def _rmsnorm(x, g):
    ms = jnp.mean(x * x, axis=-1, keepdims=True)
    return x * lax.rsqrt(ms + EPS) * g


def _silu(x):
    return x * jax.nn.sigmoid(x)


def _softplus(x):
    return jnp.maximum(x, 0.0) + jnp.log(1.0 + jnp.exp(-jnp.abs(x)))


def _dot(a, b):
    return jnp.dot(a, b, preferred_element_type=F32)


def _dot_nt(a, b):
    return lax.dot_general(a, b, NT_DIMS, preferred_element_type=F32)


def _dot_exact01(t, x):
    hi = x.astype(BF16)
    r1 = x - hi.astype(F32)
    mid = r1.astype(BF16)
    lo = (r1 - mid.astype(F32)).astype(BF16)
    return _dot(t, hi) + _dot(t, mid) + _dot(t, lo)


FFN_TM = 512
FFN_TF = 256


def _ffn_kernel(*refs, n_x, n_y, final, n_out, tiles_a):
    refs = list(refs)
    x_refs = [refs.pop(0) for _ in range(n_x)]
    y_refs = [refs.pop(0) for _ in range(n_y)]
    wpre_ref = refs.pop(0) if n_y else None
    g_ref, wg_ref, wu_ref, wd_ref = refs[:4]
    del refs[:4]
    fg_ref = refs.pop(0) if final else None
    o_refs = [refs.pop(0) for _ in range(n_out)]
    h_ref, a_ref = refs
    second = pl.program_id(0) >= tiles_a

    def pick(rs, dtype):
        if len(rs) == 1:
            return rs[0][...].astype(dtype)
        return jnp.where(second, rs[1][...].astype(dtype), rs[0][...].astype(dtype))

    x = pick(x_refs, F32)
    if n_y:
        x = x + _dot(pick(y_refs, BF16), wpre_ref[...])
    h_ref[...] = _rmsnorm(x, g_ref[...]).astype(BF16)
    for f in range(0, D_FF, FFN_TF):
        h = h_ref[...]
        gate = _dot(h, wg_ref[:, f:f + FFN_TF].astype(BF16))
        up = _dot(h, wu_ref[:, f:f + FFN_TF].astype(BF16))
        a_ref[:, f:f + FFN_TF] = (_silu(gate) * up).astype(BF16)
    y = x + 0.5 * _dot(a_ref[...], wd_ref[...].astype(BF16))
    if final:
        y = _rmsnorm(y, fg_ref[...])
    if n_out == 1:
        o_refs[0][...] = y
    else:
        @pl.when(jnp.logical_not(second))
        def _():
            o_refs[0][...] = y

        @pl.when(second)
        def _():
            o_refs[1][...] = y


def _ffn(xs, rows, g, w_gate, w_up, w_down, idx, final_g=None, pre=None, split_out=False):
    tm = FFN_TM
    tiles_a, tiles_b = rows[0] // tm, rows[1] // tm
    assert rows[0] % tm == 0 and rows[1] % tm == 0
    final = final_g is not None

    def row_specs(arrs):
        if len(arrs) == 1:
            return [pl.BlockSpec((tm, arrs[0].shape[1]), lambda i: (i, 0))]
        return [pl.BlockSpec((tm, arrs[0].shape[1]), lambda i: (jnp.minimum(i, tiles_a - 1), 0)),
                pl.BlockSpec((tm, arrs[1].shape[1]), lambda i: (jnp.maximum(i - tiles_a, 0), 0))]

    in_specs, args = row_specs(xs), list(xs)
    ys = []
    if pre is not None:
        ys, wpre = pre
        in_specs += row_specs(ys) + [_resident(wpre.shape)]
        args += list(ys) + [wpre]
    in_specs += [_resident((1, D_MODEL)), _resident((D_MODEL, D_FF), idx),
                 _resident((D_MODEL, D_FF), idx), _resident((D_FF, D_MODEL), idx)]
    args += [g.reshape(1, D_MODEL), w_gate, w_up, w_down]
    if final:
        in_specs.append(_resident((1, D_MODEL)))
        args.append(final_g.reshape(1, D_MODEL))
    if split_out:
        outs = [jax.ShapeDtypeStruct((r, D_MODEL), F32) for r in rows]
    else:
        outs = [jax.ShapeDtypeStruct((rows[0] + rows[1], D_MODEL), F32)]
    res = pl.pallas_call(
        functools.partial(_ffn_kernel, n_x=len(xs), n_y=len(ys), final=final, n_out=len(outs),
                          tiles_a=tiles_a),
        grid=(tiles_a + tiles_b,),
        in_specs=in_specs,
        out_specs=row_specs(outs),
        out_shape=outs,
        scratch_shapes=[pltpu.VMEM((tm, D_MODEL), BF16), pltpu.VMEM((tm, D_FF), BF16)],
        compiler_params=_params(1),
        name="ffn" + ("_pre" if pre is not None else "") + ("_final" if final else ""),
    )(*args)
    return res if split_out else res[0]


CONV_HALO = SUBLANES
SSD_IN_LANE_CHUNK = 512


def _shift_rows(v, k):
    if k == SUBLANES:
        return jnp.concatenate([v[:, :1], v[:, :-1]], axis=1)
    r = pltpu.roll(v, k, 2)
    prev = jnp.concatenate([r[:, :1], r[:, :-1]], axis=1)
    sub = lax.broadcasted_iota(jnp.int32, v.shape, 2)
    return jnp.where(sub < k, prev, r)


def _chunk_decays(dt, a_pad):
    q = dt.shape[0]
    tril = (lax.broadcasted_iota(jnp.int32, (q, q), 0) >= lax.broadcasted_iota(jnp.int32, (q, q), 1))
    acum = _dot_exact01(tril.astype(BF16), dt * a_pad) * LOG2E
    acum_t = acum.T
    dt_t = dt.T
    src_t = acum_t - jnp.log2(dt_t)
    w_t = jnp.exp2(acum_t[:, q - 1:q] - acum_t) * dt_t
    return acum, acum_t, src_t, w_t


def _conv_silu(ext, cw_ref, cb_ref, sl):
    w = ext.shape[-1]
    tap = lambda k: cw_ref[k:k + 1, sl].reshape(1, 1, 1, w)
    ext1 = _shift_rows(ext, 1)
    p = ext * tap(3) + ext1 * tap(2)
    q = ext * tap(1) + ext1 * tap(0)
    return _silu((cb_ref[:, sl].reshape(1, 1, 1, w) + p + _shift_rows(q, 2))[:, 1:])


def _ssd_in_kernel(x_ref, g_ref, win_ref, wdt_ref, cw_ref, cb_ref, dtb_ref, conv0_ref,
                   z_ref, xbc_ref, dt_ref, convnew_ref, halo_ref, h_ref, *, nb, tl):
    rows = nb * tl
    wc = SSD_IN_LANE_CHUNK
    nblk = tl // SUBLANES

    @pl.when(pl.program_id(1) == 0)
    def _():
        halo_ref[...] = jnp.zeros_like(halo_ref)
        halo_ref[:, CONV_HALO - (CONV_W - 1):, :] = conv0_ref[...]

    h_ref[...] = _rmsnorm(x_ref[...], g_ref[...]).astype(BF16)
    dt = _softplus(_dot(h_ref[...], wdt_ref[...]) + dtb_ref[...])
    dt_ref[...] = dt.reshape(nb, tl, dt.shape[-1])
    for c in range(0, CONV_DIM, wc):
        sl = slice(c, c + wc)
        zs = slice(c // 2, c // 2 + wc // 2)
        z_ref[:, :, zs] = _dot(h_ref[...], win_ref[:, zs]).reshape(nb, tl, wc // 2)
        cur = _dot(h_ref[...], win_ref[:, D_INNER + c:D_INNER + c + wc]).reshape(nb, tl, wc)
        ext = jnp.concatenate([halo_ref[:, :, sl], cur], axis=1).reshape(nb, nblk + 1, SUBLANES, wc)
        xbc_ref[:, :, sl] = _conv_silu(ext, cw_ref, cb_ref, sl).reshape(nb, tl, wc)
        halo_ref[:, :, sl] = cur[:, tl - CONV_HALO:, :]
    convnew_ref[...] = halo_ref[:, CONV_HALO - (CONV_W - 1):, :]


def _ssd_in(x2d, row0, b, l, g, w_in, wdt, cw, cb, dtb, conv0, *, nb, tl):
    assert l == tl and row0 % (nb * tl) == 0
    tile0 = row0 // (nb * tl)
    dt_w = wdt.shape[1]
    blk = lambda w: pl.BlockSpec((nb, tl, w), lambda i, j: (i, j, 0))
    return pl.pallas_call(
        functools.partial(_ssd_in_kernel, nb=nb, tl=tl),
        grid=(b // nb, l // tl),
        in_specs=[pl.BlockSpec((nb * tl, D_MODEL), lambda i, j: (tile0 + i, 0)),
                  _resident((1, D_MODEL)), _resident(w_in.shape), _resident((D_MODEL, dt_w)),
                  _resident((CONV_W, CONV_DIM)), _resident((1, CONV_DIM)), _resident((1, dt_w)),
                  pl.BlockSpec((nb, CONV_W - 1, CONV_DIM), lambda i, j: (i, 0, 0))],
        out_specs=[blk(D_INNER), blk(CONV_DIM), blk(dt_w),
                   pl.BlockSpec((nb, CONV_W - 1, CONV_DIM), lambda i, j: (i, 0, 0))],
        out_shape=[jax.ShapeDtypeStruct((b, l, D_INNER), F32),
                   jax.ShapeDtypeStruct((b, l, CONV_DIM), F32),
                   jax.ShapeDtypeStruct((b, l, dt_w), F32),
                   jax.ShapeDtypeStruct((b, CONV_W - 1, CONV_DIM), F32)],
        scratch_shapes=[pltpu.VMEM((nb, CONV_HALO, CONV_DIM), F32),
                        pltpu.VMEM((nb * tl, D_MODEL), BF16)],
        compiler_params=_params(2),
        name="ssd_in",
    )(x2d, g.reshape(1, D_MODEL), w_in, wdt, cw, cb.reshape(1, CONV_DIM),
      dtb.reshape(1, dt_w), conv0)


SSD_PIPE_TL = 512


def _ssd_in_prompt_kernel(x_ref, g_ref, win_ref, wdt_ref, cw_ref, cb_ref, dtb_ref, conv0_ref,
                          apad_ref, z_ref, acum_ref, tr_ref, xbc_ref, convnew_ref, halo_ref, h_ref):
    tl = SSD_PIPE_TL
    wc = SSD_IN_LANE_CHUNK
    nblk = tl // SUBLANES

    @pl.when(pl.program_id(1) == 0)
    def _():
        halo_ref[...] = jnp.zeros_like(halo_ref)
        halo_ref[CONV_HALO - (CONV_W - 1):, :] = conv0_ref[0]

    h_ref[...] = _rmsnorm(x_ref[...], g_ref[...]).astype(BF16)
    dt = _softplus(_dot(h_ref[...], wdt_ref[...]) + dtb_ref[...])
    for j in range(tl // CHUNK):
        cs = slice(j * CHUNK, (j + 1) * CHUNK)
        acum, acum_t, src_t, w_t = _chunk_decays(dt[cs, :], apad_ref[...])
        acum_ref[0, cs, :] = acum
        tr_ref[0, j, 0] = acum_t
        tr_ref[0, j, 1] = src_t
        tr_ref[0, j, 2] = w_t
    for c in range(0, CONV_DIM, wc):
        sl = slice(c, c + wc)
        zs = slice(c // 2, c // 2 + wc // 2)
        z_ref[0, :, zs] = _dot(h_ref[...], win_ref[:, zs]).astype(z_ref.dtype)
        cur = _dot(h_ref[...], win_ref[:, D_INNER + c:D_INNER + c + wc])
        ext = jnp.concatenate([halo_ref[:, sl], cur], axis=0).reshape(1, nblk + 1, SUBLANES, wc)
        xbc_ref[0, :, sl] = _conv_silu(ext, cw_ref, cb_ref, sl).reshape(tl, wc)
        halo_ref[:, sl] = cur[tl - CONV_HALO:, :]
    convnew_ref[0] = halo_ref[CONV_HALO - (CONV_W - 1):, :]


def _ssd_in_prompt(x2d, b, l, g, w_in, wdt, cw, cb, dtb, conv0, a_pad):
    tl = SSD_PIPE_TL
    tps = l // tl
    ncs = tl // CHUNK
    blk = lambda w: pl.BlockSpec((1, tl, w), lambda i, j: (i, j, 0))
    return pl.pallas_call(
        _ssd_in_prompt_kernel,
        grid=(b, l // tl),
        in_specs=[pl.BlockSpec((tl, D_MODEL), lambda i, j: (i * tps + j, 0)),
                  _resident((1, D_MODEL)), _resident(w_in.shape), _resident((D_MODEL, DT_PAD)),
                  _resident((CONV_W, CONV_DIM)), _resident((1, CONV_DIM)), _resident((1, DT_PAD)),
                  pl.BlockSpec((1, CONV_W - 1, CONV_DIM), lambda i, j: (i, 0, 0)),
                  _resident((1, DT_PAD))],
        out_specs=[blk(D_INNER), blk(DT_PAD),
                   pl.BlockSpec((1, ncs, 3, DT_PAD, CHUNK), lambda i, j: (i, j, 0, 0, 0)),
                   blk(CONV_DIM),
                   pl.BlockSpec((1, CONV_W - 1, CONV_DIM), lambda i, j: (i, 0, 0))],
        out_shape=[jax.ShapeDtypeStruct((b, l, D_INNER), BF16),
                   jax.ShapeDtypeStruct((b, l, DT_PAD), F32),
                   jax.ShapeDtypeStruct((b, l // CHUNK, 3, DT_PAD, CHUNK), F32),
                   jax.ShapeDtypeStruct((b, l, CONV_DIM), F32),
                   jax.ShapeDtypeStruct((b, CONV_W - 1, CONV_DIM), F32)],
        scratch_shapes=[pltpu.VMEM((CONV_HALO, CONV_DIM), F32), pltpu.VMEM((tl, D_MODEL), BF16)],
        compiler_params=_params(2),
        name="ssd_in_prompt",
    )(x2d, g.reshape(1, D_MODEL), w_in, wdt, cw, cb.reshape(1, CONV_DIM),
      dtb.reshape(1, DT_PAD), conv0, a_pad)


B_OFF = D_INNER
C_OFF = D_INNER + N_GROUPS * D_STATE


def _gate_norm(y, z, ng):
    yg = y * _silu(z)
    ms = jnp.mean(yg * yg, axis=-1, keepdims=True)
    return yg * lax.rsqrt(ms + EPS) * ng


def _head_rows(mat, g):
    return jnp.concatenate(
        [jnp.broadcast_to(mat[g * HEADS_PER_GROUP + r:g * HEADS_PER_GROUP + r + 1, :],
                          (HEAD_DIM, mat.shape[1])) for r in range(HEADS_PER_GROUP)], axis=0)


def _scan_prompt_kernel(xbc_ref, z_ref, acum_ref, tr_ref, de_ref, ng_ref, y_ref, hout_ref, *h_refs,
                        chunks):
    q = CHUNK
    step = pl.program_id(1)

    @pl.when(step == 0)
    def _():
        for h_ref in h_refs:
            h_ref[...] = jnp.zeros_like(h_ref)

    tril = (lax.broadcasted_iota(jnp.int32, (q, q), 0) >= lax.broadcasted_iota(jnp.int32, (q, q), 1))
    lane_g = lax.broadcasted_iota(jnp.int32, (q, GROUP_DIM), 1)
    lane_half = lax.broadcasted_iota(jnp.int32, (q, LANES), 1) < HEAD_DIM

    def chunk_body(groups, ci, carry):
        ts = pl.ds(pl.multiple_of(ci * q, q), q)
        acum = acum_ref[0, ts, :]
        acum_t, src_t, w_t = tr_ref[0, ci, 0], tr_ref[0, ci, 1], tr_ref[0, ci, 2]
        cd_b = jnp.broadcast_to(jnp.exp2(acum_t[:, q - 1:q]), (DT_PAD, D_STATE))
        for g in groups:
            gs = slice(g * GROUP_DIM, (g + 1) * GROUP_DIM)
            bg = xbc_ref[0, ts, B_OFF + g * D_STATE:B_OFF + (g + 1) * D_STATE].astype(BF16)
            cg = xbc_ref[0, ts, C_OFF + g * D_STATE:C_OFF + (g + 1) * D_STATE].astype(BF16)
            xg = xbc_ref[0, ts, gs]
            xgb = xg.astype(BF16)
            cb = _dot_nt(cg, bg)
            ms, cols = [], []
            for r in range(HEADS_PER_GROUP):
                hd = g * HEADS_PER_GROUP + r
                colf = jnp.broadcast_to(acum[:, hd:hd + 1], (q, q))
                rowf = jnp.broadcast_to(src_t[hd:hd + 1, :], (q, q))
                ms.append((cb * jnp.exp2(jnp.where(tril, colf - rowf, -jnp.inf))).astype(BF16))
                cols.append(colf)
            zero = jnp.zeros_like(xgb)
            y_diag = sum(
                _dot(ms[r], jnp.where((lane_g >= r * HEAD_DIM) & (lane_g < (r + 1) * HEAD_DIM),
                                      xgb, zero)) for r in range(HEADS_PER_GROUP))
            hg = h_refs[g][...]
            y_off = _dot_nt(cg, hg.astype(BF16))
            acum_e = jnp.concatenate([jnp.where(lane_half, cols[0], cols[1]),
                                      jnp.where(lane_half, cols[2], cols[3])], axis=1)
            y = y_diag + y_off * jnp.exp2(acum_e) + de_ref[:, gs] * xg
            xg_t = xg.T
            wt = jnp.concatenate(
                [xg_t[r * HEAD_DIM:(r + 1) * HEAD_DIM, :] * w_t[g * HEADS_PER_GROUP + r:
                                                                g * HEADS_PER_GROUP + r + 1, :]
                 for r in range(HEADS_PER_GROUP)], axis=0).astype(BF16)
            h_refs[g][...] = hg * _head_rows(cd_b, g) + _dot(wt, bg)
            zg = z_ref[0, ts, gs].astype(F32)
            y_ref[0, ts, gs] = _gate_norm(y, zg, ng_ref[:, gs]).astype(y_ref.dtype)
        return carry

    for g0 in range(0, N_GROUPS, SCAN_GROUPS_PER_PASS):
        groups = range(g0, g0 + SCAN_GROUPS_PER_PASS)
        lax.fori_loop(0, chunks, functools.partial(chunk_body, groups), 0)

    @pl.when(step == pl.num_programs(1) - 1)
    def _():
        for g, h_ref in enumerate(h_refs):
            hout_ref[0, g * GROUP_DIM:(g + 1) * GROUP_DIM, :] = h_ref[...]


SCAN_CHUNKS_PER_STEP = 4
SCAN_GROUPS_PER_PASS = 4


def _scan_prompt(xbc, z, acum, tr, d_e, ng):
    b, l, _ = xbc.shape
    chunks = SCAN_CHUNKS_PER_STEP
    tl = chunks * CHUNK
    blk = lambda w: pl.BlockSpec((1, tl, w), lambda i, j: (i, j, 0))
    return pl.pallas_call(
        functools.partial(_scan_prompt_kernel, chunks=chunks),
        grid=(b, l // tl),
        in_specs=[blk(CONV_DIM), blk(D_INNER), blk(DT_PAD),
                  pl.BlockSpec((1, chunks, 3, DT_PAD, CHUNK), lambda i, j: (i, j, 0, 0, 0)),
                  _resident((1, D_INNER)), _resident((1, D_INNER))],
        out_specs=[blk(D_INNER), pl.BlockSpec((1, D_INNER, D_STATE), lambda i, j: (i, 0, 0))],
        out_shape=[jax.ShapeDtypeStruct((b, l, D_INNER), BF16),
                   jax.ShapeDtypeStruct((b, D_INNER, D_STATE), F32)],
        scratch_shapes=[pltpu.VMEM((GROUP_DIM, D_STATE), F32)] * N_GROUPS,
        compiler_params=_params(2),
        name="ssd_scan_prompt",
    )(xbc, z, acum, tr, d_e, ng)


SAMPLE_LEN = 8
SCAN_NB = 16
SCAN_ROWS = SCAN_NB * SAMPLE_LEN
SCAN_STEP_NB = 4


def _scan_sample_kernel(xbc_ref, z_ref, dt_ref, h0_ref, apad_ref, ae_ref, de_ref, ng_ref,
                        gsum_ref, gexp_ref, y_ref, hout_ref,
                        wt_ref, ea_ref, ac_ref, xdt_ref, acp_ref, p_ref):
    rows = SCAN_ROWS
    srows = SCAN_STEP_NB * SAMPLE_LEN
    j = pl.program_id(1)

    def bcast_token(v, s):
        n, w = v.shape[0] // SAMPLE_LEN, v.shape[-1]
        v3 = v.reshape(n, SAMPLE_LEN, w)
        return jnp.broadcast_to(v3[:, s:s + 1, :], (n, SAMPLE_LEN, w)).reshape(n * SAMPLE_LEN, w)

    @pl.when(j == 0)
    def _():
        tok = lax.broadcasted_iota(jnp.int32, (rows, D_INNER), 0) & (SAMPLE_LEN - 1)
        tok_p = lax.broadcasted_iota(jnp.int32, (rows, DT_PAD), 0) & (SAMPLE_LEN - 1)

        def cumsum_tokens(v, t):
            for sh in (1, 2, 4):
                v = v + jnp.where(t >= sh, pltpu.roll(v, sh, 0), 0.0)
            return v

        xs = xbc_ref[:, :, 0:D_INNER].reshape(rows, D_INNER)
        dt_p = dt_ref[:, :, 0:DT_PAD].reshape(rows, DT_PAD)
        dt_e = dt_ref[:, :, DT_PAD:DT_PAD + D_INNER].reshape(rows, D_INNER)
        acp_ref[...] = cumsum_tokens(dt_p * apad_ref[...], tok_p)
        acum_e = cumsum_tokens(dt_e * ae_ref[...], tok)
        xdt = xs * dt_e
        w = xdt * jnp.exp(bcast_token(acum_e, SAMPLE_LEN - 1) - acum_e)
        wt_ref[...] = w.T.astype(BF16)
        ea_ref[...] = jnp.exp(acum_e)
        ac_ref[...] = acum_e
        xdt_ref[...] = xdt

    seq0 = j * SCAN_STEP_NB
    step_rows = pl.ds(pl.multiple_of(seq0 * SAMPLE_LEN, srows), srows)
    step_seqs = pl.ds(seq0, SCAN_STEP_NB)
    tok = lax.broadcasted_iota(jnp.int32, (srows, D_INNER), 0) & (SAMPLE_LEN - 1)
    xs = xbc_ref[step_seqs, :, 0:D_INNER].reshape(srows, D_INNER)
    bm = xbc_ref[step_seqs, :, B_OFF:C_OFF].reshape(srows, N_GROUPS * D_STATE)
    cm = xbc_ref[step_seqs, :, C_OFF:CONV_DIM].reshape(srows, N_GROUPS * D_STATE)
    acum_e = ac_ref[step_rows, :]
    xdt = xdt_ref[step_rows, :]
    for s in range(SAMPLE_LEN):
        p_ref[s * srows:(s + 1) * srows, :] = (cm * bcast_token(bm, s)).astype(BF16)
    cb_sum = _dot(p_ref[...], gsum_ref[...])
    cb_e = _dot(cb_sum.astype(BF16), gexp_ref[...])
    yd = de_ref[...] * xs
    for s in range(SAMPLE_LEN):
        diff = acum_e - bcast_token(acum_e, s)
        decay = jnp.exp(jnp.where(tok >= s, diff, -jnp.inf))
        yd = yd + cb_e[s * srows:(s + 1) * srows, :] * decay * bcast_token(xdt, s)

    eye = (lax.broadcasted_iota(jnp.int32, (N_HEADS, DT_PAD), 0)
           == lax.broadcasted_iota(jnp.int32, (N_HEADS, DT_PAD), 1))
    rowid = lax.broadcasted_iota(jnp.int32, (rows, D_STATE), 0)
    for t in range(SCAN_STEP_NB):
        seq = seq0 + t
        r0 = pl.multiple_of(seq * SAMPLE_LEN, SAMPLE_LEN)
        alast = acp_ref[pl.ds(r0 + SAMPLE_LEN - 1, 1), :]
        alast_col = jnp.sum(jnp.where(eye, jnp.broadcast_to(alast, (N_HEADS, DT_PAD)), 0.0),
                            axis=1, keepdims=True)
        cd_b = jnp.broadcast_to(jnp.exp(alast_col), (N_HEADS, D_STATE))
        mine = (rowid >= r0) & (rowid < r0 + SAMPLE_LEN)
        y_offs = []
        for g in range(N_GROUPS):
            gs = slice(g * GROUP_DIM, (g + 1) * GROUP_DIM)
            hg = h0_ref[t, gs, :]
            cg = xbc_ref[seq, :, C_OFF + g * D_STATE:C_OFF + (g + 1) * D_STATE].astype(BF16)
            y_offs.append(_dot_nt(cg, hg.astype(BF16)))
            b_all = xbc_ref[:, :, B_OFF + g * D_STATE:B_OFF + (g + 1) * D_STATE].reshape(
                rows, D_STATE)
            b_mine = jnp.where(mine, b_all, 0.0).astype(BF16)
            hout_ref[t, gs, :] = hg * _head_rows(cd_b, g) + _dot(wt_ref[gs, :], b_mine)
        y = yd[t * SAMPLE_LEN:(t + 1) * SAMPLE_LEN, :] + jnp.concatenate(y_offs, axis=1) * ea_ref[
            pl.ds(r0, SAMPLE_LEN), :]
        z = z_ref[seq]
        y_ref[seq] = jnp.concatenate(
            [_gate_norm(y[:, g * GROUP_DIM:(g + 1) * GROUP_DIM],
                        z[:, g * GROUP_DIM:(g + 1) * GROUP_DIM],
                        ng_ref[:, g * GROUP_DIM:(g + 1) * GROUP_DIM]) for g in range(N_GROUPS)],
            axis=1)


def _scan_sample(xbc, z, dt, h0, a_pad, a_e, d_e, ng, gsum, gexp):
    b = xbc.shape[0]
    dt_w = dt.shape[-1]
    blk = lambda w: pl.BlockSpec((SCAN_NB, SAMPLE_LEN, w), lambda i, j: (i, 0, 0))
    steps = SCAN_NB // SCAN_STEP_NB
    st = pl.BlockSpec((SCAN_STEP_NB, D_INNER, D_STATE), lambda i, j: (i * steps + j, 0, 0))
    return pl.pallas_call(
        _scan_sample_kernel,
        grid=(b // SCAN_NB, steps),
        in_specs=[blk(CONV_DIM), blk(D_INNER), blk(dt_w), st, _resident((1, DT_PAD)),
                  _resident((1, D_INNER)), _resident((1, D_INNER)), _resident((1, D_INNER)),
                  _resident(gsum.shape), _resident(gexp.shape)],
        out_specs=[blk(D_INNER), st],
        out_shape=[jax.ShapeDtypeStruct((b, SAMPLE_LEN, D_INNER), F32),
                   jax.ShapeDtypeStruct((b, D_INNER, D_STATE), F32)],
        scratch_shapes=[pltpu.VMEM((D_INNER, SCAN_ROWS), BF16),
                        pltpu.VMEM((SCAN_ROWS, D_INNER), F32),
                        pltpu.VMEM((SCAN_ROWS, D_INNER), F32),
                        pltpu.VMEM((SCAN_ROWS, D_INNER), F32),
                        pltpu.VMEM((SCAN_ROWS, DT_PAD), F32),
                        pltpu.VMEM((SAMPLE_LEN * SCAN_STEP_NB * SAMPLE_LEN, N_GROUPS * D_STATE),
                                   BF16)],
        compiler_params=_params(2),
        name="ssd_scan_sample",
    )(xbc, z, dt, h0, a_pad, a_e, d_e, ng, gsum, gexp)


POOL_HALO = 2 * SUBLANES


def _pool_kernel(x_ref, g_ref, win_ref, wgrp_ref, scale_ref, wout_ref, buf0_ref,
                 o_ref, bufnew_ref, ext_ref, *, nb, tl, pos0):
    rows = nb * tl
    jt = pl.program_id(1)

    @pl.when(jt == 0)
    def _():
        ext_ref[:, 0:POOL_HALO, :] = jnp.zeros((nb, POOL_HALO, D_MODEL), F32)
        ext_ref[:, POOL_HALO - (MAX_WIN - 1):POOL_HALO, :] = buf0_ref[...]

    x = x_ref[...]
    h = _rmsnorm(x, g_ref[...]).astype(BF16)
    ext_ref[:, POOL_HALO:POOL_HALO + tl, :] = _dot(h, win_ref[...]).reshape(nb, tl, D_MODEL)
    nblk = tl // SUBLANES
    halo_blocks = POOL_HALO // SUBLANES
    shape4 = (nb, nblk, SUBLANES, POOL_GROUP_DIM)
    pos = (pos0 + jt * tl + SUBLANES * lax.broadcasted_iota(jnp.int32, shape4, 1)
           + lax.broadcasted_iota(jnp.int32, shape4, 2)).astype(F32)
    out = x
    for k, w in enumerate(POOL_WINDOWS):
        sl = slice(k * POOL_GROUP_DIM, (k + 1) * POOL_GROUP_DIM)
        ext = ext_ref[:, :, sl].reshape(nb, nblk + halo_blocks, SUBLANES, POOL_GROUP_DIM)
        tot, shift = ext, 1
        while shift < w:
            tot = tot + _shift_rows(tot, shift)
            shift *= 2
        u = ext[:, halo_blocks:]
        mean = tot[:, halo_blocks:] / jnp.minimum(jnp.float32(w), pos + 1.0)
        m = (mean - u).reshape(rows, POOL_GROUP_DIM).astype(BF16)
        mixed = (_dot(m, wgrp_ref[k]) * scale_ref[:, sl]).astype(BF16)
        out = out + _dot(mixed, wout_ref[sl, :])
    o_ref[...] = out
    bufnew_ref[...] = ext_ref[:, tl + 1:tl + POOL_HALO, :]
    ext_ref[:, 0:POOL_HALO, :] = ext_ref[:, tl:tl + POOL_HALO, :]


def _pool(x2d, row0, b, l, g, w_in, w_grp, scale, w_out, buf0, *, nb, tl, pos0):
    assert row0 % (nb * tl) == 0 and (nb == 1 or l == tl)
    tile0, tps = row0 // (nb * tl), l // tl
    return pl.pallas_call(
        functools.partial(_pool_kernel, nb=nb, tl=tl, pos0=pos0),
        grid=(b // nb, tps),
        in_specs=[pl.BlockSpec((nb * tl, D_MODEL), lambda i, j: (tile0 + i * tps + j, 0)),
                  _resident((1, D_MODEL)), _resident((D_MODEL, D_MODEL)),
                  _resident(w_grp.shape), _resident((1, D_MODEL)), _resident((D_MODEL, D_MODEL)),
                  pl.BlockSpec((nb, MAX_WIN - 1, D_MODEL), lambda i, j: (i, 0, 0))],
        out_specs=[pl.BlockSpec((nb * tl, D_MODEL), lambda i, j: (i * tps + j, 0)),
                   pl.BlockSpec((nb, MAX_WIN - 1, D_MODEL), lambda i, j: (i, 0, 0))],
        out_shape=[jax.ShapeDtypeStruct((b * l, D_MODEL), F32),
                   jax.ShapeDtypeStruct((b, MAX_WIN - 1, D_MODEL), F32)],
        scratch_shapes=[pltpu.VMEM((nb, POOL_HALO + tl, D_MODEL), F32)],
        compiler_params=_params(2),
        name="pool_mixer",
    )(x2d, g.reshape(1, D_MODEL), w_in, w_grp, scale.reshape(1, D_MODEL), w_out, buf0)


def _expand_heads(v):
    return jnp.repeat(v.astype(F32), HEAD_DIM).reshape(1, D_INNER)


def _pad_heads(v):
    return jnp.pad(v.astype(F32), (0, DT_PAD - N_HEADS)).reshape(1, DT_PAD)


def _group_sum_matrix():
    m = np.zeros((N_GROUPS * D_STATE, LANES), np.float32)
    for g in range(N_GROUPS):
        m[g * D_STATE:(g + 1) * D_STATE, g] = 1.0
    return jnp.asarray(m, BF16)


def _group_expand_matrix():
    m = np.zeros((LANES, D_INNER), np.float32)
    for g in range(N_GROUPS):
        m[g, g * GROUP_DIM:(g + 1) * GROUP_DIM] = 1.0
    return jnp.asarray(m, BF16)


def _trunks(x_prompt, x_sample, ssm0_s, conv0_s, pool0_s, p):
    bp, lp, _ = x_prompt.shape
    bs, ls, _ = x_sample.shape
    rows = (bp * lp, bs * ls)
    bf = lambda a: a.astype(BF16)
    w_gate, w_up, w_down = p["ffn_w_gate"], p["ffn_w_up"], p["ffn_w_down"]
    ffn = lambda xs, i, k, **kw: _ffn(xs, rows, p["ffn_norm"][i, k], w_gate, w_up, w_down, (i, k), **kw)
    x2d = ffn([x_prompt.reshape(rows[0], D_MODEL), x_sample.reshape(rows[1], D_MODEL)], 0, 0)

    w_in = p["ssd_w_in"][0]
    w_dt = w_in[:, D_INNER + CONV_DIM:]
    dt_bias = p["ssd_dt_bias"][0]
    wdt = jnp.pad(w_dt, ((0, 0), (0, DT_PAD - N_HEADS)))
    dtb = jnp.pad(dt_bias, (0, DT_PAD - N_HEADS))
    wdt_s = jnp.concatenate([wdt, jnp.repeat(w_dt, HEAD_DIM, axis=1)], axis=1)
    dtb_s = jnp.concatenate([dtb, jnp.repeat(dt_bias, HEAD_DIM)])
    a_neg = -jnp.exp(p["ssd_a_log"][0].astype(F32))
    a_pad, d_e = _pad_heads(a_neg), _expand_heads(p["ssd_d"][0])
    ng = p["ssd_norm"][0].reshape(1, D_INNER)
    common = (p["mix_norm"][0], bf(w_in))
    conv_w, conv_b = p["ssd_conv_w"][0], p["ssd_conv_b"][0]
    z, acum, tr, xbc, conv_p = _ssd_in_prompt(x2d, bp, lp, *common, bf(wdt), conv_w, conv_b, dtb,
                                              jnp.zeros((bp, CONV_W - 1, CONV_DIM), F32), a_pad)
    y_p, ssm_p = _scan_prompt(xbc, z, acum, tr, d_e, ng)
    z, xbc, dt, conv_s = _ssd_in(x2d, rows[0], bs, ls, *common, bf(wdt_s), conv_w, conv_b, dtb_s,
                                 conv0_s, nb=32, tl=SAMPLE_LEN)
    y_s, ssm_s = _scan_sample(xbc, z, dt, ssm0_s.reshape(bs, D_INNER, D_STATE), a_pad,
                              _expand_heads(a_neg), d_e, ng, _group_sum_matrix(),
                              _group_expand_matrix())
    x2d = ffn([x2d], 0, 1, pre=([y_p.reshape(rows[0], D_INNER), bf(y_s).reshape(rows[1], D_INNER)],
                                bf(p["ssd_w_out"][0])))

    x2d = ffn([x2d], 1, 0)
    pool_w = (p["mix_norm"][1], bf(p["pool_w_in"][0]), bf(p["pool_w_group"][0]), p["pool_scale"][0],
              bf(p["pool_w_out"][0]))
    xm_p, pool_p = _pool(x2d, 0, bp, lp, *pool_w, jnp.zeros((bp, MAX_WIN - 1, D_MODEL), F32),
                         nb=1, tl=1024, pos0=0)
    xm_s, pool_s = _pool(x2d, rows[0], bs, ls, *pool_w, pool0_s, nb=64, tl=SAMPLE_LEN, pos0=PAST_LEN)
    out_p, out_s = ffn([xm_p, xm_s], 1, 1, final_g=p["final_norm"], split_out=True)
    state = lambda a, b: a.reshape(b, N_HEADS, HEAD_DIM, D_STATE)[None]
    return (out_p.reshape(bp, lp, D_MODEL), out_s.reshape(bs, ls, D_MODEL),
            state(ssm_p, bp), conv_p[None], pool_p[None], state(ssm_s, bs), conv_s[None], pool_s[None])


def kernel(x_prompt, x_sample, state_ssm, state_conv, state_pool, ffn_norm, ffn_w_gate, ffn_w_up,
           ffn_w_down, mix_norm, ssd_w_in, ssd_conv_w, ssd_conv_b, ssd_dt_bias, ssd_a_log, ssd_d,
           ssd_norm, ssd_w_out, pool_w_in, pool_w_group, pool_scale, pool_w_out, final_norm):
    p = dict(ffn_norm=ffn_norm, ffn_w_gate=ffn_w_gate, ffn_w_up=ffn_w_up, ffn_w_down=ffn_w_down,
             mix_norm=mix_norm, ssd_w_in=ssd_w_in, ssd_conv_w=ssd_conv_w, ssd_conv_b=ssd_conv_b,
             ssd_dt_bias=ssd_dt_bias, ssd_a_log=ssd_a_log, ssd_d=ssd_d, ssd_norm=ssd_norm,
             ssd_w_out=ssd_w_out, pool_w_in=pool_w_in, pool_w_group=pool_w_group,
             pool_scale=pool_scale, pool_w_out=pool_w_out, final_norm=final_norm)
    return _trunks(x_prompt, x_sample, state_ssm[0], state_conv[0], state_pool[0], p)
```

```python
import functools

import numpy as np
import jax
import jax.numpy as jnp
from jax import lax
from jax.experimental import pallas as pl
from jax.experimental.pallas import tpu as pltpu

F32 = jnp.float32
BF16 = jnp.bfloat16

EPS = 1e-6
D_MODEL = 1024
D_FF = 2816
D_INNER = 2048
HEAD_DIM = 64
N_HEADS = 32
N_GROUPS = 8
HEADS_PER_GROUP = 4
GROUP_DIM = HEADS_PER_GROUP * HEAD_DIM
D_STATE = 128
CONV_W = 4
CONV_DIM = D_INNER + 2 * N_GROUPS * D_STATE
CHUNK = 128
POOL_WINDOWS = (2, 4, 8, 16)
POOL_GROUP_DIM = 256
MAX_WIN = 16
PAST_LEN = 16384
LANES = 128
SUBLANES = 8
DT_PAD = LANES
VMEM_LIMIT = 60 * 1024 * 1024

NT_DIMS = (((1,), (1,)), ((), ()))
LOG2E = 1.4426950408889634


def _resident(shape, lead=()):
    nd = len(shape)
    return pl.BlockSpec((None,) * len(lead) + tuple(shape), lambda *_: tuple(lead) + (0,) * nd,
                        pipeline_mode=pl.Buffered(1))


def _params(n_axes):
    return pltpu.CompilerParams(dimension_semantics=("arbitrary",) * n_axes,
                                vmem_limit_bytes=VMEM_LIMIT)


def _rmsnorm(x, g):
    ms = jnp.mean(x * x, axis=-1, keepdims=True)
    return x * lax.rsqrt(ms + EPS) * g


def _silu(x):
    hx = 0.5 * x
    return hx * jnp.tanh(hx) + hx


def _softplus(x):
    return jnp.maximum(x, 0.0) + jnp.log(1.0 + jnp.exp(-jnp.abs(x)))


def _dot(a, b):
    return jnp.dot(a, b, preferred_element_type=F32)


def _dot_nt(a, b):
    return lax.dot_general(a, b, NT_DIMS, preferred_element_type=F32)


def _dot_exact01(t, x):
    hi = x.astype(BF16)
    r1 = x - hi.astype(F32)
    mid = r1.astype(BF16)
    lo = (r1 - mid.astype(F32)).astype(BF16)
    return _dot(t, hi) + _dot(t, mid) + _dot(t, lo)


FFN_TM = 512
FFN_TF = 256


def _ffn_kernel(*refs, n_x, n_y, final, n_out, tiles_a):
    refs = list(refs)
    x_refs = [refs.pop(0) for _ in range(n_x)]
    y_refs = [refs.pop(0) for _ in range(n_y)]
    wpre_ref = refs.pop(0) if n_y else None
    g_ref, wg_ref, wu_ref, wd_ref = refs[:4]
    del refs[:4]
    fg_ref = refs.pop(0) if final else None
    o_refs = [refs.pop(0) for _ in range(n_out)]
    h_ref, a_ref = refs
    second = pl.program_id(0) >= tiles_a

    def pick(rs, dtype):
        if len(rs) == 1:
            return rs[0][...].astype(dtype)
        return jnp.where(second, rs[1][...].astype(dtype), rs[0][...].astype(dtype))

    x = pick(x_refs, F32)
    if n_y:
        x = x + _dot(pick(y_refs, BF16), wpre_ref[...])
    h_ref[...] = _rmsnorm(x, g_ref[...]).astype(BF16)
    for f in range(0, D_FF, FFN_TF):
        h = h_ref[...]
        gate = _dot(h, wg_ref[:, f:f + FFN_TF].astype(BF16))
        up = _dot(h, wu_ref[:, f:f + FFN_TF].astype(BF16))
        a_ref[:, f:f + FFN_TF] = (_silu(gate) * up).astype(BF16)
    y = x + 0.5 * _dot(a_ref[...], wd_ref[...].astype(BF16))
    if final:
        y = _rmsnorm(y, fg_ref[...])
    if n_out == 1:
        o_refs[0][...] = y
    else:
        @pl.when(jnp.logical_not(second))
        def _():
            o_refs[0][...] = y

        @pl.when(second)
        def _():
            o_refs[1][...] = y


def _ffn(xs, rows, g, w_gate, w_up, w_down, idx, final_g=None, pre=None, split_out=False):
    tm = FFN_TM
    tiles_a, tiles_b = rows[0] // tm, rows[1] // tm
    assert rows[0] % tm == 0 and rows[1] % tm == 0
    final = final_g is not None

    def row_specs(arrs):
        if len(arrs) == 1:
            return [pl.BlockSpec((tm, arrs[0].shape[1]), lambda i: (i, 0))]
        return [pl.BlockSpec((tm, arrs[0].shape[1]), lambda i: (jnp.minimum(i, tiles_a - 1), 0)),
                pl.BlockSpec((tm, arrs[1].shape[1]), lambda i: (jnp.maximum(i - tiles_a, 0), 0))]

    in_specs, args = row_specs(xs), list(xs)
    ys = []
    if pre is not None:
        ys, wpre = pre
        in_specs += row_specs(ys) + [_resident(wpre.shape)]
        args += list(ys) + [wpre]
    in_specs += [_resident((1, D_MODEL)), _resident((D_MODEL, D_FF), idx),
                 _resident((D_MODEL, D_FF), idx), _resident((D_FF, D_MODEL), idx)]
    args += [g.reshape(1, D_MODEL), w_gate, w_up, w_down]
    if final:
        in_specs.append(_resident((1, D_MODEL)))
        args.append(final_g.reshape(1, D_MODEL))
    if split_out:
        outs = [jax.ShapeDtypeStruct((r, D_MODEL), F32) for r in rows]
    else:
        outs = [jax.ShapeDtypeStruct((rows[0] + rows[1], D_MODEL), F32)]
    res = pl.pallas_call(
        functools.partial(_ffn_kernel, n_x=len(xs), n_y=len(ys), final=final, n_out=len(outs),
                          tiles_a=tiles_a),
        grid=(tiles_a + tiles_b,),
        in_specs=in_specs,
        out_specs=row_specs(outs),
        out_shape=outs,
        scratch_shapes=[pltpu.VMEM((tm, D_MODEL), BF16), pltpu.VMEM((tm, D_FF), BF16)],
        compiler_params=_params(1),
        name="ffn" + ("_pre" if pre is not None else "") + ("_final" if final else ""),
    )(*args)
    return res if split_out else res[0]


CONV_HALO = SUBLANES
SSD_IN_LANE_CHUNK = 512


def _shift_rows(v, k):
    if k == SUBLANES:
        return jnp.concatenate([v[:, :1], v[:, :-1]], axis=1)
    r = pltpu.roll(v, k, 2)
    prev = jnp.concatenate([r[:, :1], r[:, :-1]], axis=1)
    sub = lax.broadcasted_iota(jnp.int32, v.shape, 2)
    return jnp.where(sub < k, prev, r)


def _chunk_decays(dt, a_pad):
    q = dt.shape[0]
    tril = (lax.broadcasted_iota(jnp.int32, (q, q), 0) >= lax.broadcasted_iota(jnp.int32, (q, q), 1))
    acum = _dot_exact01(tril.astype(BF16), dt * a_pad) * LOG2E
    acum_t = acum.T
    dt_t = dt.T
    src_t = acum_t - jnp.log2(dt_t)
    w_t = jnp.exp2(acum_t[:, q - 1:q] - acum_t) * dt_t
    return acum, acum_t, src_t, w_t


def _conv_silu(ext, cw_ref, cb_ref, sl):
    w = ext.shape[-1]
    tap = lambda k: cw_ref[k:k + 1, sl].reshape(1, 1, 1, w)
    ext1 = _shift_rows(ext, 1)
    p = ext * tap(3) + ext1 * tap(2)
    q = ext * tap(1) + ext1 * tap(0)
    return _silu((cb_ref[:, sl].reshape(1, 1, 1, w) + p + _shift_rows(q, 2))[:, 1:])


def _ssd_in_kernel(x_ref, g_ref, win_ref, wdt_ref, cw_ref, cb_ref, dtb_ref, conv0_ref,
                   z_ref, xbc_ref, dt_ref, convnew_ref, halo_ref, h_ref, *, nb, tl):
    rows = nb * tl
    wc = SSD_IN_LANE_CHUNK
    nblk = tl // SUBLANES

    @pl.when(pl.program_id(1) == 0)
    def _():
        halo_ref[...] = jnp.zeros_like(halo_ref)
        halo_ref[:, CONV_HALO - (CONV_W - 1):, :] = conv0_ref[...]

    h_ref[...] = _rmsnorm(x_ref[...], g_ref[...]).astype(BF16)
    dt = _softplus(_dot(h_ref[...], wdt_ref[...]) + dtb_ref[...])
    dt_ref[...] = dt.reshape(nb, tl, dt.shape[-1])
    for c in range(0, CONV_DIM, wc):
        sl = slice(c, c + wc)
        zs = slice(c // 2, c // 2 + wc // 2)
        z_ref[:, :, zs] = _dot(h_ref[...], win_ref[:, zs]).reshape(nb, tl, wc // 2)
        cur = _dot(h_ref[...], win_ref[:, D_INNER + c:D_INNER + c + wc]).reshape(nb, tl, wc)
        ext = jnp.concatenate([halo_ref[:, :, sl], cur], axis=1).reshape(nb, nblk + 1, SUBLANES, wc)
        xbc_ref[:, :, sl] = _conv_silu(ext, cw_ref, cb_ref, sl).reshape(nb, tl, wc)
        halo_ref[:, :, sl] = cur[:, tl - CONV_HALO:, :]
    convnew_ref[...] = halo_ref[:, CONV_HALO - (CONV_W - 1):, :]


def _ssd_in(x2d, row0, b, l, g, w_in, wdt, cw, cb, dtb, conv0, *, nb, tl):
    assert l == tl and row0 % (nb * tl) == 0
    tile0 = row0 // (nb * tl)
    dt_w = wdt.shape[1]
    blk = lambda w: pl.BlockSpec((nb, tl, w), lambda i, j: (i, j, 0))
    return pl.pallas_call(
        functools.partial(_ssd_in_kernel, nb=nb, tl=tl),
        grid=(b // nb, l // tl),
        in_specs=[pl.BlockSpec((nb * tl, D_MODEL), lambda i, j: (tile0 + i, 0)),
                  _resident((1, D_MODEL)), _resident(w_in.shape), _resident((D_MODEL, dt_w)),
                  _resident((CONV_W, CONV_DIM)), _resident((1, CONV_DIM)), _resident((1, dt_w)),
                  pl.BlockSpec((nb, CONV_W - 1, CONV_DIM), lambda i, j: (i, 0, 0))],
        out_specs=[blk(D_INNER), blk(CONV_DIM), blk(dt_w),
                   pl.BlockSpec((nb, CONV_W - 1, CONV_DIM), lambda i, j: (i, 0, 0))],
        out_shape=[jax.ShapeDtypeStruct((b, l, D_INNER), F32),
                   jax.ShapeDtypeStruct((b, l, CONV_DIM), F32),
                   jax.ShapeDtypeStruct((b, l, dt_w), F32),
                   jax.ShapeDtypeStruct((b, CONV_W - 1, CONV_DIM), F32)],
        scratch_shapes=[pltpu.VMEM((nb, CONV_HALO, CONV_DIM), F32),
                        pltpu.VMEM((nb * tl, D_MODEL), BF16)],
        compiler_params=_params(2),
        name="ssd_in",
    )(x2d, g.reshape(1, D_MODEL), w_in, wdt, cw, cb.reshape(1, CONV_DIM),
      dtb.reshape(1, dt_w), conv0)


SSD_PIPE_TL = 512


def _ssd_in_prompt_kernel(x_ref, g_ref, win_ref, wdt_ref, cw_ref, cb_ref, dtb_ref, conv0_ref,
                          apad_ref, z_ref, acum_ref, tr_ref, xbc_ref, convnew_ref, halo_ref, h_ref):
    tl = SSD_PIPE_TL
    wc = SSD_IN_LANE_CHUNK
    nblk = tl // SUBLANES

    @pl.when(pl.program_id(1) == 0)
    def _():
        halo_ref[...] = jnp.zeros_like(halo_ref)
        halo_ref[CONV_HALO - (CONV_W - 1):, :] = conv0_ref[0]

    h_ref[...] = _rmsnorm(x_ref[...], g_ref[...]).astype(BF16)
    dt = _softplus(_dot(h_ref[...], wdt_ref[...]) + dtb_ref[...])
    for j in range(tl // CHUNK):
        cs = slice(j * CHUNK, (j + 1) * CHUNK)
        acum, acum_t, src_t, w_t = _chunk_decays(dt[cs, :], apad_ref[...])
        acum_ref[0, cs, :] = acum
        tr_ref[0, j, 0] = acum_t
        tr_ref[0, j, 1] = src_t
        tr_ref[0, j, 2] = w_t
    for c in range(0, CONV_DIM, wc):
        sl = slice(c, c + wc)
        zs = slice(c // 2, c // 2 + wc // 2)
        z_ref[0, :, zs] = _dot(h_ref[...], win_ref[:, zs]).astype(z_ref.dtype)
        cur = _dot(h_ref[...], win_ref[:, D_INNER + c:D_INNER + c + wc])
        ext = jnp.concatenate([halo_ref[:, sl], cur], axis=0).reshape(1, nblk + 1, SUBLANES, wc)
        xbc_ref[0, :, sl] = _conv_silu(ext, cw_ref, cb_ref, sl).reshape(tl, wc)
        halo_ref[:, sl] = cur[tl - CONV_HALO:, :]
    convnew_ref[0] = halo_ref[CONV_HALO - (CONV_W - 1):, :]


def _ssd_in_prompt(x2d, b, l, g, w_in, wdt, cw, cb, dtb, conv0, a_pad):
    tl = SSD_PIPE_TL
    tps = l // tl
    ncs = tl // CHUNK
    blk = lambda w: pl.BlockSpec((1, tl, w), lambda i, j: (i, j, 0))
    return pl.pallas_call(
        _ssd_in_prompt_kernel,
        grid=(b, l // tl),
        in_specs=[pl.BlockSpec((tl, D_MODEL), lambda i, j: (i * tps + j, 0)),
                  _resident((1, D_MODEL)), _resident(w_in.shape), _resident((D_MODEL, DT_PAD)),
                  _resident((CONV_W, CONV_DIM)), _resident((1, CONV_DIM)), _resident((1, DT_PAD)),
                  pl.BlockSpec((1, CONV_W - 1, CONV_DIM), lambda i, j: (i, 0, 0)),
                  _resident((1, DT_PAD))],
        out_specs=[blk(D_INNER), blk(DT_PAD),
                   pl.BlockSpec((1, ncs, 3, DT_PAD, CHUNK), lambda i, j: (i, j, 0, 0, 0)),
                   blk(CONV_DIM),
                   pl.BlockSpec((1, CONV_W - 1, CONV_DIM), lambda i, j: (i, 0, 0))],
        out_shape=[jax.ShapeDtypeStruct((b, l, D_INNER), BF16),
                   jax.ShapeDtypeStruct((b, l, DT_PAD), F32),
                   jax.ShapeDtypeStruct((b, l // CHUNK, 3, DT_PAD, CHUNK), F32),
                   jax.ShapeDtypeStruct((b, l, CONV_DIM), F32),
                   jax.ShapeDtypeStruct((b, CONV_W - 1, CONV_DIM), F32)],
        scratch_shapes=[pltpu.VMEM((CONV_HALO, CONV_DIM), F32), pltpu.VMEM((tl, D_MODEL), BF16)],
        compiler_params=_params(2),
        name="ssd_in_prompt",
    )(x2d, g.reshape(1, D_MODEL), w_in, wdt, cw, cb.reshape(1, CONV_DIM),
      dtb.reshape(1, DT_PAD), conv0, a_pad)


B_OFF = D_INNER
C_OFF = D_INNER + N_GROUPS * D_STATE


def _gate_norm(y, z, ng):
    yg = y * _silu(z)
    ms = jnp.mean(yg * yg, axis=-1, keepdims=True)
    return yg * lax.rsqrt(ms + EPS) * ng


def _head_rows(mat, g):
    return jnp.concatenate(
        [jnp.broadcast_to(mat[g * HEADS_PER_GROUP + r:g * HEADS_PER_GROUP + r + 1, :],
                          (HEAD_DIM, mat.shape[1])) for r in range(HEADS_PER_GROUP)], axis=0)


def _scan_prompt_kernel(xbc_ref, z_ref, acum_ref, tr_ref, de_ref, ng_ref, y_ref, hout_ref, *h_refs,
                        chunks):
    q = CHUNK
    step = pl.program_id(1)

    @pl.when(step == 0)
    def _():
        for h_ref in h_refs:
            h_ref[...] = jnp.zeros_like(h_ref)

    tril = (lax.broadcasted_iota(jnp.int32, (q, q), 0) >= lax.broadcasted_iota(jnp.int32, (q, q), 1))
    lane_g = lax.broadcasted_iota(jnp.int32, (q, GROUP_DIM), 1)
    lane_half = lax.broadcasted_iota(jnp.int32, (q, LANES), 1) < HEAD_DIM

    def chunk_body(groups, ci, carry):
        ts = pl.ds(pl.multiple_of(ci * q, q), q)
        acum = acum_ref[0, ts, :]
        acum_t, src_t, w_t = tr_ref[0, ci, 0], tr_ref[0, ci, 1], tr_ref[0, ci, 2]
        cd_b = jnp.broadcast_to(jnp.exp2(acum_t[:, q - 1:q]), (DT_PAD, D_STATE))
        for g in groups:
            gs = slice(g * GROUP_DIM, (g + 1) * GROUP_DIM)
            bg = xbc_ref[0, ts, B_OFF + g * D_STATE:B_OFF + (g + 1) * D_STATE].astype(BF16)
            cg = xbc_ref[0, ts, C_OFF + g * D_STATE:C_OFF + (g + 1) * D_STATE].astype(BF16)
            xg = xbc_ref[0, ts, gs]
            xgb = xg.astype(BF16)
            cb = _dot_nt(cg, bg)
            ms, cols = [], []
            for r in range(HEADS_PER_GROUP):
                hd = g * HEADS_PER_GROUP + r
                colf = jnp.broadcast_to(acum[:, hd:hd + 1], (q, q))
                rowf = jnp.broadcast_to(src_t[hd:hd + 1, :], (q, q))
                ms.append((cb * jnp.exp2(jnp.where(tril, colf - rowf, -jnp.inf))).astype(BF16))
                cols.append(colf)
            zero = jnp.zeros_like(xgb)
            y_diag = sum(
                _dot(ms[r], jnp.where((lane_g >= r * HEAD_DIM) & (lane_g < (r + 1) * HEAD_DIM),
                                      xgb, zero)) for r in range(HEADS_PER_GROUP))
            hg = h_refs[g][...]
            y_off = _dot_nt(cg, hg.astype(BF16))
            acum_e = jnp.concatenate([jnp.where(lane_half, cols[0], cols[1]),
                                      jnp.where(lane_half, cols[2], cols[3])], axis=1)
            y = y_diag + y_off * jnp.exp2(acum_e) + de_ref[:, gs] * xg
            xg_t = xg.T
            wt = jnp.concatenate(
                [xg_t[r * HEAD_DIM:(r + 1) * HEAD_DIM, :] * w_t[g * HEADS_PER_GROUP + r:
                                                                g * HEADS_PER_GROUP + r + 1, :]
                 for r in range(HEADS_PER_GROUP)], axis=0).astype(BF16)
            h_refs[g][...] = hg * _head_rows(cd_b, g) + _dot(wt, bg)
            zg = z_ref[0, ts, gs].astype(F32)
            y_ref[0, ts, gs] = _gate_norm(y, zg, ng_ref[:, gs]).astype(y_ref.dtype)
        return carry

    for g0 in range(0, N_GROUPS, SCAN_GROUPS_PER_PASS):
        groups = range(g0, g0 + SCAN_GROUPS_PER_PASS)
        lax.fori_loop(0, chunks, functools.partial(chunk_body, groups), 0)

    @pl.when(step == pl.num_programs(1) - 1)
    def _():
        for g, h_ref in enumerate(h_refs):
            hout_ref[0, g * GROUP_DIM:(g + 1) * GROUP_DIM, :] = h_ref[...]


SCAN_CHUNKS_PER_STEP = 4
SCAN_GROUPS_PER_PASS = 4


def _scan_prompt(xbc, z, acum, tr, d_e, ng):
    b, l, _ = xbc.shape
    chunks = SCAN_CHUNKS_PER_STEP
    tl = chunks * CHUNK
    blk = lambda w: pl.BlockSpec((1, tl, w), lambda i, j: (i, j, 0))
    return pl.pallas_call(
        functools.partial(_scan_prompt_kernel, chunks=chunks),
        grid=(b, l // tl),
        in_specs=[blk(CONV_DIM), blk(D_INNER), blk(DT_PAD),
                  pl.BlockSpec((1, chunks, 3, DT_PAD, CHUNK), lambda i, j: (i, j, 0, 0, 0)),
                  _resident((1, D_INNER)), _resident((1, D_INNER))],
        out_specs=[blk(D_INNER), pl.BlockSpec((1, D_INNER, D_STATE), lambda i, j: (i, 0, 0))],
        out_shape=[jax.ShapeDtypeStruct((b, l, D_INNER), BF16),
                   jax.ShapeDtypeStruct((b, D_INNER, D_STATE), F32)],
        scratch_shapes=[pltpu.VMEM((GROUP_DIM, D_STATE), F32)] * N_GROUPS,
        compiler_params=_params(2),
        name="ssd_scan_prompt",
    )(xbc, z, acum, tr, d_e, ng)


SAMPLE_LEN = 8
SCAN_NB = 16
SCAN_ROWS = SCAN_NB * SAMPLE_LEN
SCAN_STEP_NB = 8


def _scan_sample_kernel(xbc_ref, z_ref, dt_ref, h0_ref, apad_ref, ae_ref, de_ref, ng_ref,
                        gsum_ref, gexp_ref, y_ref, hout_ref,
                        wt_ref, ea_ref, ac_ref, xdt_ref, acp_ref, p_ref):
    rows = SCAN_ROWS
    srows = SCAN_STEP_NB * SAMPLE_LEN
    j = pl.program_id(1)

    def bcast_token(v, s):
        n, w = v.shape[0] // SAMPLE_LEN, v.shape[-1]
        v3 = v.reshape(n, SAMPLE_LEN, w)
        return jnp.broadcast_to(v3[:, s:s + 1, :], (n, SAMPLE_LEN, w)).reshape(n * SAMPLE_LEN, w)

    @pl.when(j == 0)
    def _():
        tok = lax.broadcasted_iota(jnp.int32, (rows, D_INNER), 0) & (SAMPLE_LEN - 1)
        tok_p = lax.broadcasted_iota(jnp.int32, (rows, DT_PAD), 0) & (SAMPLE_LEN - 1)

        def cumsum_tokens(v, t):
            for sh in (1, 2, 4):
                v = v + jnp.where(t >= sh, pltpu.roll(v, sh, 0), 0.0)
            return v

        xs = xbc_ref[:, :, 0:D_INNER].reshape(rows, D_INNER)
        dt_p = dt_ref[:, :, 0:DT_PAD].reshape(rows, DT_PAD)
        dt_e = dt_ref[:, :, DT_PAD:DT_PAD + D_INNER].reshape(rows, D_INNER)
        acp_ref[...] = cumsum_tokens(dt_p * apad_ref[...], tok_p)
        acum_e = cumsum_tokens(dt_e * ae_ref[...], tok)
        xdt = xs * dt_e
        w = xdt * jnp.exp(bcast_token(acum_e, SAMPLE_LEN - 1) - acum_e)
        wt_ref[...] = w.T.astype(BF16)
        ea_ref[...] = jnp.exp(acum_e)
        ac_ref[...] = acum_e
        xdt_ref[...] = xdt

    seq0 = j * SCAN_STEP_NB
    step_rows = pl.ds(pl.multiple_of(seq0 * SAMPLE_LEN, srows), srows)
    step_seqs = pl.ds(seq0, SCAN_STEP_NB)
    tok = lax.broadcasted_iota(jnp.int32, (srows, D_INNER), 0) & (SAMPLE_LEN - 1)
    xs = xbc_ref[step_seqs, :, 0:D_INNER].reshape(srows, D_INNER)
    bm = xbc_ref[step_seqs, :, B_OFF:C_OFF].reshape(srows, N_GROUPS * D_STATE)
    cm = xbc_ref[step_seqs, :, C_OFF:CONV_DIM].reshape(srows, N_GROUPS * D_STATE)
    acum_e = ac_ref[step_rows, :]
    xdt = xdt_ref[step_rows, :]
    for s in range(SAMPLE_LEN):
        p_ref[s * srows:(s + 1) * srows, :] = (cm * bcast_token(bm, s)).astype(BF16)
    cb_sum = _dot(p_ref[...], gsum_ref[...])
    cb_e = _dot(cb_sum.astype(BF16), gexp_ref[...])
    yd = de_ref[...] * xs
    for s in range(SAMPLE_LEN):
        diff = acum_e - bcast_token(acum_e, s)
        decay = jnp.exp(jnp.where(tok >= s, diff, -jnp.inf))
        yd = yd + cb_e[s * srows:(s + 1) * srows, :] * decay * bcast_token(xdt, s)

    eye = (lax.broadcasted_iota(jnp.int32, (N_HEADS, DT_PAD), 0)
           == lax.broadcasted_iota(jnp.int32, (N_HEADS, DT_PAD), 1))
    rowid = lax.broadcasted_iota(jnp.int32, (rows, D_STATE), 0)
    for t in range(SCAN_STEP_NB):
        seq = seq0 + t
        r0 = pl.multiple_of(seq * SAMPLE_LEN, SAMPLE_LEN)
        alast = acp_ref[pl.ds(r0 + SAMPLE_LEN - 1, 1), :]
        alast_col = jnp.sum(jnp.where(eye, jnp.broadcast_to(alast, (N_HEADS, DT_PAD)), 0.0),
                            axis=1, keepdims=True)
        cd_b = jnp.broadcast_to(jnp.exp(alast_col), (N_HEADS, D_STATE))
        mine = (rowid >= r0) & (rowid < r0 + SAMPLE_LEN)
        y_offs = []
        for g in range(N_GROUPS):
            gs = slice(g * GROUP_DIM, (g + 1) * GROUP_DIM)
            hg = h0_ref[t, gs, :]
            cg = xbc_ref[seq, :, C_OFF + g * D_STATE:C_OFF + (g + 1) * D_STATE].astype(BF16)
            y_offs.append(_dot_nt(cg, hg.astype(BF16)))
            b_all = xbc_ref[:, :, B_OFF + g * D_STATE:B_OFF + (g + 1) * D_STATE].reshape(
                rows, D_STATE)
            b_mine = jnp.where(mine, b_all, 0.0).astype(BF16)
            hout_ref[t, gs, :] = hg * _head_rows(cd_b, g) + _dot(wt_ref[gs, :], b_mine)
        y = yd[t * SAMPLE_LEN:(t + 1) * SAMPLE_LEN, :] + jnp.concatenate(y_offs, axis=1) * ea_ref[
            pl.ds(r0, SAMPLE_LEN), :]
        z = z_ref[seq]
        y_ref[seq] = jnp.concatenate(
            [_gate_norm(y[:, g * GROUP_DIM:(g + 1) * GROUP_DIM],
                        z[:, g * GROUP_DIM:(g + 1) * GROUP_DIM],
                        ng_ref[:, g * GROUP_DIM:(g + 1) * GROUP_DIM]) for g in range(N_GROUPS)],
            axis=1)


def _scan_sample(xbc, z, dt, h0, a_pad, a_e, d_e, ng, gsum, gexp):
    b = xbc.shape[0]
    dt_w = dt.shape[-1]
    blk = lambda w: pl.BlockSpec((SCAN_NB, SAMPLE_LEN, w), lambda i, j: (i, 0, 0))
    steps = SCAN_NB // SCAN_STEP_NB
    st = pl.BlockSpec((SCAN_STEP_NB, D_INNER, D_STATE), lambda i, j: (i * steps + j, 0, 0))
    return pl.pallas_call(
        _scan_sample_kernel,
        grid=(b // SCAN_NB, steps),
        in_specs=[blk(CONV_DIM), blk(D_INNER), blk(dt_w), st, _resident((1, DT_PAD)),
                  _resident((1, D_INNER)), _resident((1, D_INNER)), _resident((1, D_INNER)),
                  _resident(gsum.shape), _resident(gexp.shape)],
        out_specs=[blk(D_INNER), st],
        out_shape=[jax.ShapeDtypeStruct((b, SAMPLE_LEN, D_INNER), F32),
                   jax.ShapeDtypeStruct((b, D_INNER, D_STATE), F32)],
        scratch_shapes=[pltpu.VMEM((D_INNER, SCAN_ROWS), BF16),
                        pltpu.VMEM((SCAN_ROWS, D_INNER), F32),
                        pltpu.VMEM((SCAN_ROWS, D_INNER), F32),
                        pltpu.VMEM((SCAN_ROWS, D_INNER), F32),
                        pltpu.VMEM((SCAN_ROWS, DT_PAD), F32),
                        pltpu.VMEM((SAMPLE_LEN * SCAN_STEP_NB * SAMPLE_LEN, N_GROUPS * D_STATE),
                                   BF16)],
        compiler_params=_params(2),
        name="ssd_scan_sample",
    )(xbc, z, dt, h0, a_pad, a_e, d_e, ng, gsum, gexp)


POOL_HALO = 2 * SUBLANES


def _pool_kernel(x_ref, g_ref, win_ref, wgrp_ref, scale_ref, wout_ref, buf0_ref,
                 o_ref, bufnew_ref, ext_ref, *, nb, tl, pos0):
    rows = nb * tl
    jt = pl.program_id(1)

    @pl.when(jt == 0)
    def _():
        ext_ref[:, 0:POOL_HALO, :] = jnp.zeros((nb, POOL_HALO, D_MODEL), F32)
        ext_ref[:, POOL_HALO - (MAX_WIN - 1):POOL_HALO, :] = buf0_ref[...]

    x = x_ref[...]
    h = _rmsnorm(x, g_ref[...]).astype(BF16)
    ext_ref[:, POOL_HALO:POOL_HALO + tl, :] = _dot(h, win_ref[...]).reshape(nb, tl, D_MODEL)
    nblk = tl // SUBLANES
    halo_blocks = POOL_HALO // SUBLANES
    shape4 = (nb, nblk, SUBLANES, POOL_GROUP_DIM)
    pos = (pos0 + jt * tl + SUBLANES * lax.broadcasted_iota(jnp.int32, shape4, 1)
           + lax.broadcasted_iota(jnp.int32, shape4, 2)).astype(F32)
    out = x
    for k, w in enumerate(POOL_WINDOWS):
        sl = slice(k * POOL_GROUP_DIM, (k + 1) * POOL_GROUP_DIM)
        ext = ext_ref[:, :, sl].reshape(nb, nblk + halo_blocks, SUBLANES, POOL_GROUP_DIM)
        tot, shift = ext, 1
        while shift < w:
            tot = tot + _shift_rows(tot, shift)
            shift *= 2
        u = ext[:, halo_blocks:]
        mean = tot[:, halo_blocks:] / jnp.minimum(jnp.float32(w), pos + 1.0)
        m = (mean - u).reshape(rows, POOL_GROUP_DIM).astype(BF16)
        mixed = (_dot(m, wgrp_ref[k]) * scale_ref[:, sl]).astype(BF16)
        out = out + _dot(mixed, wout_ref[sl, :])
    o_ref[...] = out
    bufnew_ref[...] = ext_ref[:, tl + 1:tl + POOL_HALO, :]
    ext_ref[:, 0:POOL_HALO, :] = ext_ref[:, tl:tl + POOL_HALO, :]


def _pool(x2d, row0, b, l, g, w_in, w_grp, scale, w_out, buf0, *, nb, tl, pos0):
    assert row0 % (nb * tl) == 0 and (nb == 1 or l == tl)
    tile0, tps = row0 // (nb * tl), l // tl
    return pl.pallas_call(
        functools.partial(_pool_kernel, nb=nb, tl=tl, pos0=pos0),
        grid=(b // nb, tps),
        in_specs=[pl.BlockSpec((nb * tl, D_MODEL), lambda i, j: (tile0 + i * tps + j, 0)),
                  _resident((1, D_MODEL)), _resident((D_MODEL, D_MODEL)),
                  _resident(w_grp.shape), _resident((1, D_MODEL)), _resident((D_MODEL, D_MODEL)),
                  pl.BlockSpec((nb, MAX_WIN - 1, D_MODEL), lambda i, j: (i, 0, 0))],
        out_specs=[pl.BlockSpec((nb * tl, D_MODEL), lambda i, j: (i * tps + j, 0)),
                   pl.BlockSpec((nb, MAX_WIN - 1, D_MODEL), lambda i, j: (i, 0, 0))],
        out_shape=[jax.ShapeDtypeStruct((b * l, D_MODEL), F32),
                   jax.ShapeDtypeStruct((b, MAX_WIN - 1, D_MODEL), F32)],
        scratch_shapes=[pltpu.VMEM((nb, POOL_HALO + tl, D_MODEL), F32)],
        compiler_params=_params(2),
        name="pool_mixer",
    )(x2d, g.reshape(1, D_MODEL), w_in, w_grp, scale.reshape(1, D_MODEL), w_out, buf0)


def _expand_heads(v):
    return jnp.repeat(v.astype(F32), HEAD_DIM).reshape(1, D_INNER)


def _pad_heads(v):
    return jnp.pad(v.astype(F32), (0, DT_PAD - N_HEADS)).reshape(1, DT_PAD)


def _group_sum_matrix():
    m = np.zeros((N_GROUPS * D_STATE, LANES), np.float32)
    for g in range(N_GROUPS):
        m[g * D_STATE:(g + 1) * D_STATE, g] = 1.0
    return jnp.asarray(m, BF16)


def _group_expand_matrix():
    m = np.zeros((LANES, D_INNER), np.float32)
    for g in range(N_GROUPS):
        m[g, g * GROUP_DIM:(g + 1) * GROUP_DIM] = 1.0
    return jnp.asarray(m, BF16)


def _trunks(x_prompt, x_sample, ssm0_s, conv0_s, pool0_s, p):
    bp, lp, _ = x_prompt.shape
    bs, ls, _ = x_sample.shape
    rows = (bp * lp, bs * ls)
    bf = lambda a: a.astype(BF16)
    w_gate, w_up, w_down = p["ffn_w_gate"], p["ffn_w_up"], p["ffn_w_down"]
    ffn = lambda xs, i, k, **kw: _ffn(xs, rows, p["ffn_norm"][i, k], w_gate, w_up, w_down, (i, k), **kw)
    x2d = ffn([x_prompt.reshape(rows[0], D_MODEL), x_sample.reshape(rows[1], D_MODEL)], 0, 0)

    w_in = p["ssd_w_in"][0]
    w_dt = w_in[:, D_INNER + CONV_DIM:]
    dt_bias = p["ssd_dt_bias"][0]
    wdt = jnp.pad(w_dt, ((0, 0), (0, DT_PAD - N_HEADS)))
    dtb = jnp.pad(dt_bias, (0, DT_PAD - N_HEADS))
    wdt_s = jnp.concatenate([wdt, jnp.repeat(w_dt, HEAD_DIM, axis=1)], axis=1)
    dtb_s = jnp.concatenate([dtb, jnp.repeat(dt_bias, HEAD_DIM)])
    a_neg = -jnp.exp(p["ssd_a_log"][0].astype(F32))
    a_pad, d_e = _pad_heads(a_neg), _expand_heads(p["ssd_d"][0])
    ng = p["ssd_norm"][0].reshape(1, D_INNER)
    common = (p["mix_norm"][0], bf(w_in))
    conv_w, conv_b = p["ssd_conv_w"][0], p["ssd_conv_b"][0]
    z, acum, tr, xbc, conv_p = _ssd_in_prompt(x2d, bp, lp, *common, bf(wdt), conv_w, conv_b, dtb,
                                              jnp.zeros((bp, CONV_W - 1, CONV_DIM), F32), a_pad)
    y_p, ssm_p = _scan_prompt(xbc, z, acum, tr, d_e, ng)
    z, xbc, dt, conv_s = _ssd_in(x2d, rows[0], bs, ls, *common, bf(wdt_s), conv_w, conv_b, dtb_s,
                                 conv0_s, nb=32, tl=SAMPLE_LEN)
    y_s, ssm_s = _scan_sample(xbc, z, dt, ssm0_s.reshape(bs, D_INNER, D_STATE), a_pad,
                              _expand_heads(a_neg), d_e, ng, _group_sum_matrix(),
                              _group_expand_matrix())
    x2d = ffn([x2d], 0, 1, pre=([y_p.reshape(rows[0], D_INNER), bf(y_s).reshape(rows[1], D_INNER)],
                                bf(p["ssd_w_out"][0])))

    x2d = ffn([x2d], 1, 0)
    pool_w = (p["mix_norm"][1], bf(p["pool_w_in"][0]), bf(p["pool_w_group"][0]), p["pool_scale"][0],
              bf(p["pool_w_out"][0]))
    xm_p, pool_p = _pool(x2d, 0, bp, lp, *pool_w, jnp.zeros((bp, MAX_WIN - 1, D_MODEL), F32),
                         nb=1, tl=1024, pos0=0)
    xm_s, pool_s = _pool(x2d, rows[0], bs, ls, *pool_w, pool0_s, nb=64, tl=SAMPLE_LEN, pos0=PAST_LEN)
    out_p, out_s = ffn([xm_p, xm_s], 1, 1, final_g=p["final_norm"], split_out=True)
    state = lambda a, b: a.reshape(b, N_HEADS, HEAD_DIM, D_STATE)[None]
    return (out_p.reshape(bp, lp, D_MODEL), out_s.reshape(bs, ls, D_MODEL),
            state(ssm_p, bp), conv_p[None], pool_p[None], state(ssm_s, bs), conv_s[None], pool_s[None])


def kernel(x_prompt, x_sample, state_ssm, state_conv, state_pool, ffn_norm, ffn_w_gate, ffn_w_up,
           ffn_w_down, mix_norm, ssd_w_in, ssd_conv_w, ssd_conv_b, ssd_dt_bias, ssd_a_log, ssd_d,
           ssd_norm, ssd_w_out, pool_w_in, pool_w_group, pool_scale, pool_w_out, final_norm):
    p = dict(ffn_norm=ffn_norm, ffn_w_gate=ffn_w_gate, ffn_w_up=ffn_w_up, ffn_w_down=ffn_w_down,
             mix_norm=mix_norm, ssd_w_in=ssd_w_in, ssd_conv_w=ssd_conv_w, ssd_conv_b=ssd_conv_b,
             ssd_dt_bias=ssd_dt_bias, ssd_a_log=ssd_a_log, ssd_d=ssd_d, ssd_norm=ssd_norm,
             ssd_w_out=ssd_w_out, pool_w_in=pool_w_in, pool_w_group=pool_w_group,
             pool_scale=pool_scale, pool_w_out=pool_w_out, final_norm=final_norm)
    return _trunks(x_prompt, x_sample, state_ssm[0], state_conv[0], state_pool[0], p)
```

```python
import functools

import numpy as np
import jax
import jax.numpy as jnp
from jax import lax
from jax.experimental import pallas as pl
from jax.experimental.pallas import tpu as pltpu

F32 = jnp.float32
BF16 = jnp.bfloat16

EPS = 1e-6
D_MODEL = 1024
D_FF = 2816
D_INNER = 2048
HEAD_DIM = 64
N_HEADS = 32
N_GROUPS = 8
HEADS_PER_GROUP = 4
GROUP_DIM = HEADS_PER_GROUP * HEAD_DIM
D_STATE = 128
CONV_W = 4
CONV_DIM = D_INNER + 2 * N_GROUPS * D_STATE
CHUNK = 128
POOL_WINDOWS = (2, 4, 8, 16)
POOL_GROUP_DIM = 256
MAX_WIN = 16
PAST_LEN = 16384
LANES = 128
SUBLANES = 8
DT_PAD = LANES
VMEM_LIMIT = 60 * 1024 * 1024

NT_DIMS = (((1,), (1,)), ((), ()))
LOG2E = 1.4426950408889634


def _resident(shape, lead=()):
    nd = len(shape)
    return pl.BlockSpec((None,) * len(lead) + tuple(shape), lambda *_: tuple(lead) + (0,) * nd,
                        pipeline_mode=pl.Buffered(1))


def _params(n_axes):
    return pltpu.CompilerParams(dimension_semantics=("arbitrary",) * n_axes,
                                vmem_limit_bytes=VMEM_LIMIT)


def _rmsnorm(x, g):
    ms = jnp.mean(x * x, axis=-1, keepdims=True)
    return x * lax.rsqrt(ms + EPS) * g


def _silu(x):
    hx = 0.5 * x
    return hx * jnp.tanh(hx) + hx


def _softplus(x):
    return jnp.maximum(x, 0.0) + jnp.log(1.0 + jnp.exp(-jnp.abs(x)))


def _dot(a, b):
    return jnp.dot(a, b, preferred_element_type=F32)


def _dot_nt(a, b):
    return lax.dot_general(a, b, NT_DIMS, preferred_element_type=F32)


def _dot_exact01(t, x):
    hi = x.astype(BF16)
    r1 = x - hi.astype(F32)
    mid = r1.astype(BF16)
    lo = (r1 - mid.astype(F32)).astype(BF16)
    return _dot(t, hi) + _dot(t, mid) + _dot(t, lo)


FFN_TM = 512
FFN_TF = 256


def _ffn_kernel(*refs, n_x, n_y, final, n_out, tiles_a):
    refs = list(refs)
    x_refs = [refs.pop(0) for _ in range(n_x)]
    y_refs = [refs.pop(0) for _ in range(n_y)]
    wpre_ref = refs.pop(0) if n_y else None
    g_ref, wg_ref, wu_ref, wd_ref = refs[:4]
    del refs[:4]
    fg_ref = refs.pop(0) if final else None
    o_refs = [refs.pop(0) for _ in range(n_out)]
    h_ref, a_ref = refs
    second = pl.program_id(0) >= tiles_a

    def pick(rs, dtype):
        if len(rs) == 1:
            return rs[0][...].astype(dtype)
        return jnp.where(second, rs[1][...].astype(dtype), rs[0][...].astype(dtype))

    x = pick(x_refs, F32)
    if n_y:
        x = x + _dot(pick(y_refs, BF16), wpre_ref[...])
    h_ref[...] = _rmsnorm(x, g_ref[...]).astype(BF16)
    for f in range(0, D_FF, FFN_TF):
        h = h_ref[...]
        gate = _dot(h, wg_ref[:, f:f + FFN_TF].astype(BF16))
        up = _dot(h, wu_ref[:, f:f + FFN_TF].astype(BF16))
        a_ref[:, f:f + FFN_TF] = (_silu(gate) * up).astype(BF16)
    y = x + 0.5 * _dot(a_ref[...], wd_ref[...].astype(BF16))
    if final:
        y = _rmsnorm(y, fg_ref[...])
    if n_out == 1:
        o_refs[0][...] = y
    else:
        @pl.when(jnp.logical_not(second))
        def _():
            o_refs[0][...] = y

        @pl.when(second)
        def _():
            o_refs[1][...] = y


def _ffn(xs, rows, g, w_gate, w_up, w_down, idx, final_g=None, pre=None, split_out=False):
    tm = FFN_TM
    tiles_a, tiles_b = rows[0] // tm, rows[1] // tm
    assert rows[0] % tm == 0 and rows[1] % tm == 0
    final = final_g is not None

    def row_specs(arrs):
        if len(arrs) == 1:
            return [pl.BlockSpec((tm, arrs[0].shape[1]), lambda i: (i, 0))]
        return [pl.BlockSpec((tm, arrs[0].shape[1]), lambda i: (jnp.minimum(i, tiles_a - 1), 0)),
                pl.BlockSpec((tm, arrs[1].shape[1]), lambda i: (jnp.maximum(i - tiles_a, 0), 0))]

    in_specs, args = row_specs(xs), list(xs)
    ys = []
    if pre is not None:
        ys, wpre = pre
        in_specs += row_specs(ys) + [_resident(wpre.shape)]
        args += list(ys) + [wpre]
    in_specs += [_resident((1, D_MODEL)), _resident((D_MODEL, D_FF), idx),
                 _resident((D_MODEL, D_FF), idx), _resident((D_FF, D_MODEL), idx)]
    args += [g.reshape(1, D_MODEL), w_gate, w_up, w_down]
    if final:
        in_specs.append(_resident((1, D_MODEL)))
        args.append(final_g.reshape(1, D_MODEL))
    if split_out:
        outs = [jax.ShapeDtypeStruct((r, D_MODEL), F32) for r in rows]
    else:
        outs = [jax.ShapeDtypeStruct((rows[0] + rows[1], D_MODEL), F32)]
    res = pl.pallas_call(
        functools.partial(_ffn_kernel, n_x=len(xs), n_y=len(ys), final=final, n_out=len(outs),
                          tiles_a=tiles_a),
        grid=(tiles_a + tiles_b,),
        in_specs=in_specs,
        out_specs=row_specs(outs),
        out_shape=outs,
        scratch_shapes=[pltpu.VMEM((tm, D_MODEL), BF16), pltpu.VMEM((tm, D_FF), BF16)],
        compiler_params=_params(1),
        name="ffn" + ("_pre" if pre is not None else "") + ("_final" if final else ""),
    )(*args)
    return res if split_out else res[0]


CONV_HALO = SUBLANES
SSD_IN_LANE_CHUNK = 512


def _shift_rows(v, k):
    if k == SUBLANES:
        return jnp.concatenate([v[:, :1], v[:, :-1]], axis=1)
    r = pltpu.roll(v, k, 2)
    prev = jnp.concatenate([r[:, :1], r[:, :-1]], axis=1)
    sub = lax.broadcasted_iota(jnp.int32, v.shape, 2)
    return jnp.where(sub < k, prev, r)


def _chunk_decays(dt, a_pad):
    q = dt.shape[0]
    tril = (lax.broadcasted_iota(jnp.int32, (q, q), 0) >= lax.broadcasted_iota(jnp.int32, (q, q), 1))
    acum = _dot_exact01(tril.astype(BF16), dt * a_pad) * LOG2E
    acum_t = acum.T
    dt_t = dt.T
    src_t = acum_t - jnp.log2(dt_t)
    w_t = jnp.exp2(acum_t[:, q - 1:q] - acum_t) * dt_t
    return acum, acum_t, src_t, w_t


def _conv_silu(ext, cw_ref, cb_ref, sl):
    w = ext.shape[-1]
    tap = lambda k: cw_ref[k:k + 1, sl].reshape(1, 1, 1, w)
    ext1 = _shift_rows(ext, 1)
    p = ext * tap(3) + ext1 * tap(2)
    q = ext * tap(1) + ext1 * tap(0)
    return _silu((cb_ref[:, sl].reshape(1, 1, 1, w) + p + _shift_rows(q, 2))[:, 1:])


def _ssd_in_kernel(x_ref, g_ref, win_ref, wdt_ref, cw_ref, cb_ref, dtb_ref, conv0_ref,
                   z_ref, xbc_ref, dt_ref, convnew_ref, halo_ref, h_ref, *, nb, tl):
    rows = nb * tl
    wc = SSD_IN_LANE_CHUNK
    nblk = tl // SUBLANES

    @pl.when(pl.program_id(1) == 0)
    def _():
        halo_ref[...] = jnp.zeros_like(halo_ref)
        halo_ref[:, CONV_HALO - (CONV_W - 1):, :] = conv0_ref[...]

    h_ref[...] = _rmsnorm(x_ref[...], g_ref[...]).astype(BF16)
    dt = _softplus(_dot(h_ref[...], wdt_ref[...]) + dtb_ref[...])
    dt_ref[...] = dt.reshape(nb, tl, dt.shape[-1])
    for c in range(0, CONV_DIM, wc):
        sl = slice(c, c + wc)
        zs = slice(c // 2, c // 2 + wc // 2)
        z_ref[:, :, zs] = _dot(h_ref[...], win_ref[:, zs]).reshape(nb, tl, wc // 2)
        cur = _dot(h_ref[...], win_ref[:, D_INNER + c:D_INNER + c + wc]).reshape(nb, tl, wc)
        ext = jnp.concatenate([halo_ref[:, :, sl], cur], axis=1).reshape(nb, nblk + 1, SUBLANES, wc)
        xbc_ref[:, :, sl] = _conv_silu(ext, cw_ref, cb_ref, sl).reshape(nb, tl, wc)
        halo_ref[:, :, sl] = cur[:, tl - CONV_HALO:, :]
    convnew_ref[...] = halo_ref[:, CONV_HALO - (CONV_W - 1):, :]


def _ssd_in(x2d, row0, b, l, g, w_in, wdt, cw, cb, dtb, conv0, *, nb, tl):
    assert l == tl and row0 % (nb * tl) == 0
    tile0 = row0 // (nb * tl)
    dt_w = wdt.shape[1]
    blk = lambda w: pl.BlockSpec((nb, tl, w), lambda i, j: (i, j, 0))
    return pl.pallas_call(
        functools.partial(_ssd_in_kernel, nb=nb, tl=tl),
        grid=(b // nb, l // tl),
        in_specs=[pl.BlockSpec((nb * tl, D_MODEL), lambda i, j: (tile0 + i, 0)),
                  _resident((1, D_MODEL)), _resident(w_in.shape), _resident((D_MODEL, dt_w)),
                  _resident((CONV_W, CONV_DIM)), _resident((1, CONV_DIM)), _resident((1, dt_w)),
                  pl.BlockSpec((nb, CONV_W - 1, CONV_DIM), lambda i, j: (i, 0, 0))],
        out_specs=[blk(D_INNER), blk(CONV_DIM), blk(dt_w),
                   pl.BlockSpec((nb, CONV_W - 1, CONV_DIM), lambda i, j: (i, 0, 0))],
        out_shape=[jax.ShapeDtypeStruct((b, l, D_INNER), F32),
                   jax.ShapeDtypeStruct((b, l, CONV_DIM), F32),
                   jax.ShapeDtypeStruct((b, l, dt_w), F32),
                   jax.ShapeDtypeStruct((b, CONV_W - 1, CONV_DIM), F32)],
        scratch_shapes=[pltpu.VMEM((nb, CONV_HALO, CONV_DIM), F32),
                        pltpu.VMEM((nb * tl, D_MODEL), BF16)],
        compiler_params=_params(2),
        name="ssd_in",
    )(x2d, g.reshape(1, D_MODEL), w_in, wdt, cw, cb.reshape(1, CONV_DIM),
      dtb.reshape(1, dt_w), conv0)


SSD_PIPE_TL = 512


def _ssd_in_prompt_kernel(x_ref, g_ref, win_ref, wdt_ref, cw_ref, cb_ref, dtb_ref, conv0_ref,
                          apad_ref, z_ref, acum_ref, tr_ref, xbc_ref, convnew_ref, halo_ref, h_ref):
    tl = SSD_PIPE_TL
    wc = SSD_IN_LANE_CHUNK
    nblk = tl // SUBLANES

    @pl.when(pl.program_id(1) == 0)
    def _():
        halo_ref[...] = jnp.zeros_like(halo_ref)
        halo_ref[CONV_HALO - (CONV_W - 1):, :] = conv0_ref[0]

    h_ref[...] = _rmsnorm(x_ref[...], g_ref[...]).astype(BF16)
    dt = _softplus(_dot(h_ref[...], wdt_ref[...]) + dtb_ref[...])
    for j in range(tl // CHUNK):
        cs = slice(j * CHUNK, (j + 1) * CHUNK)
        acum, acum_t, src_t, w_t = _chunk_decays(dt[cs, :], apad_ref[...])
        acum_ref[0, cs, :] = acum
        tr_ref[0, j, 0] = acum_t
        tr_ref[0, j, 1] = src_t
        tr_ref[0, j, 2] = w_t
    for c in range(0, CONV_DIM, wc):
        sl = slice(c, c + wc)
        zs = slice(c // 2, c // 2 + wc // 2)
        z_ref[0, :, zs] = _dot(h_ref[...], win_ref[:, zs]).astype(z_ref.dtype)
        cur = _dot(h_ref[...], win_ref[:, D_INNER + c:D_INNER + c + wc])
        ext = jnp.concatenate([halo_ref[:, sl], cur], axis=0).reshape(1, nblk + 1, SUBLANES, wc)
        xbc_ref[0, :, sl] = _conv_silu(ext, cw_ref, cb_ref, sl).reshape(tl, wc)
        halo_ref[:, sl] = cur[tl - CONV_HALO:, :]
    convnew_ref[0] = halo_ref[CONV_HALO - (CONV_W - 1):, :]


def _ssd_in_prompt(x2d, b, l, g, w_in, wdt, cw, cb, dtb, conv0, a_pad):
    tl = SSD_PIPE_TL
    tps = l // tl
    ncs = tl // CHUNK
    blk = lambda w: pl.BlockSpec((1, tl, w), lambda i, j: (i, j, 0))
    return pl.pallas_call(
        _ssd_in_prompt_kernel,
        grid=(b, l // tl),
        in_specs=[pl.BlockSpec((tl, D_MODEL), lambda i, j: (i * tps + j, 0)),
                  _resident((1, D_MODEL)), _resident(w_in.shape), _resident((D_MODEL, DT_PAD)),
                  _resident((CONV_W, CONV_DIM)), _resident((1, CONV_DIM)), _resident((1, DT_PAD)),
                  pl.BlockSpec((1, CONV_W - 1, CONV_DIM), lambda i, j: (i, 0, 0)),
                  _resident((1, DT_PAD))],
        out_specs=[blk(D_INNER), blk(DT_PAD),
                   pl.BlockSpec((1, ncs, 3, DT_PAD, CHUNK), lambda i, j: (i, j, 0, 0, 0)),
                   blk(CONV_DIM),
                   pl.BlockSpec((1, CONV_W - 1, CONV_DIM), lambda i, j: (i, 0, 0))],
        out_shape=[jax.ShapeDtypeStruct((b, l, D_INNER), BF16),
                   jax.ShapeDtypeStruct((b, l, DT_PAD), F32),
                   jax.ShapeDtypeStruct((b, l // CHUNK, 3, DT_PAD, CHUNK), F32),
                   jax.ShapeDtypeStruct((b, l, CONV_DIM), F32),
                   jax.ShapeDtypeStruct((b, CONV_W - 1, CONV_DIM), F32)],
        scratch_shapes=[pltpu.VMEM((CONV_HALO, CONV_DIM), F32), pltpu.VMEM((tl, D_MODEL), BF16)],
        compiler_params=_params(2),
        name="ssd_in_prompt",
    )(x2d, g.reshape(1, D_MODEL), w_in, wdt, cw, cb.reshape(1, CONV_DIM),
      dtb.reshape(1, DT_PAD), conv0, a_pad)


B_OFF = D_INNER
C_OFF = D_INNER + N_GROUPS * D_STATE


def _gate_norm(y, z, ng):
    yg = y * _silu(z)
    ms = jnp.mean(yg * yg, axis=-1, keepdims=True)
    return yg * lax.rsqrt(ms + EPS) * ng


def _head_rows(mat, g):
    return jnp.concatenate(
        [jnp.broadcast_to(mat[g * HEADS_PER_GROUP + r:g * HEADS_PER_GROUP + r + 1, :],
                          (HEAD_DIM, mat.shape[1])) for r in range(HEADS_PER_GROUP)], axis=0)


def _scan_prompt_kernel(xbc_ref, z_ref, acum_ref, tr_ref, de_ref, ng_ref, y_ref, hout_ref, *h_refs,
                        chunks):
    q = CHUNK
    step = pl.program_id(1)

    @pl.when(step == 0)
    def _():
        for h_ref in h_refs:
            h_ref[...] = jnp.zeros_like(h_ref)

    tril = (lax.broadcasted_iota(jnp.int32, (q, q), 0) >= lax.broadcasted_iota(jnp.int32, (q, q), 1))
    lane_g = lax.broadcasted_iota(jnp.int32, (q, GROUP_DIM), 1)
    lane_half = lax.broadcasted_iota(jnp.int32, (q, LANES), 1) < HEAD_DIM

    def chunk_body(groups, ci, carry):
        ts = pl.ds(pl.multiple_of(ci * q, q), q)
        acum = acum_ref[0, ts, :]
        acum_t, src_t, w_t = tr_ref[0, ci, 0], tr_ref[0, ci, 1], tr_ref[0, ci, 2]
        cd_b = jnp.broadcast_to(jnp.exp2(acum_t[:, q - 1:q]), (DT_PAD, D_STATE))
        for g in groups:
            gs = slice(g * GROUP_DIM, (g + 1) * GROUP_DIM)
            bg = xbc_ref[0, ts, B_OFF + g * D_STATE:B_OFF + (g + 1) * D_STATE].astype(BF16)
            cg = xbc_ref[0, ts, C_OFF + g * D_STATE:C_OFF + (g + 1) * D_STATE].astype(BF16)
            xg = xbc_ref[0, ts, gs]
            xgb = xg.astype(BF16)
            cb = _dot_nt(cg, bg)
            ms, cols = [], []
            for r in range(HEADS_PER_GROUP):
                hd = g * HEADS_PER_GROUP + r
                colf = jnp.broadcast_to(acum[:, hd:hd + 1], (q, q))
                rowf = jnp.broadcast_to(src_t[hd:hd + 1, :], (q, q))
                ms.append((cb * jnp.exp2(jnp.where(tril, colf - rowf, -jnp.inf))).astype(BF16))
                cols.append(colf)
            zero = jnp.zeros_like(xgb)
            y_diag = sum(
                _dot(ms[r], jnp.where((lane_g >= r * HEAD_DIM) & (lane_g < (r + 1) * HEAD_DIM),
                                      xgb, zero)) for r in range(HEADS_PER_GROUP))
            hg = h_refs[g][...]
            y_off = _dot_nt(cg, hg.astype(BF16))
            acum_e = jnp.concatenate([jnp.where(lane_half, cols[0], cols[1]),
                                      jnp.where(lane_half, cols[2], cols[3])], axis=1)
            y = y_diag + y_off * jnp.exp2(acum_e) + de_ref[:, gs] * xg
            xg_t = xg.T
            wt = jnp.concatenate(
                [xg_t[r * HEAD_DIM:(r + 1) * HEAD_DIM, :] * w_t[g * HEADS_PER_GROUP + r:
                                                                g * HEADS_PER_GROUP + r + 1, :]
                 for r in range(HEADS_PER_GROUP)], axis=0).astype(BF16)
            h_refs[g][...] = hg * _head_rows(cd_b, g) + _dot(wt, bg)
            zg = z_ref[0, ts, gs].astype(F32)
            y_ref[0, ts, gs] = _gate_norm(y, zg, ng_ref[:, gs]).astype(y_ref.dtype)
        return carry

    for g0 in range(0, N_GROUPS, SCAN_GROUPS_PER_PASS):
        groups = range(g0, g0 + SCAN_GROUPS_PER_PASS)
        lax.fori_loop(0, chunks, functools.partial(chunk_body, groups), 0)

    @pl.when(step == pl.num_programs(1) - 1)
    def _():
        for g, h_ref in enumerate(h_refs):
            hout_ref[0, g * GROUP_DIM:(g + 1) * GROUP_DIM, :] = h_ref[...]


SCAN_CHUNKS_PER_STEP = 4
SCAN_GROUPS_PER_PASS = 4


def _scan_prompt(xbc, z, acum, tr, d_e, ng):
    b, l, _ = xbc.shape
    chunks = SCAN_CHUNKS_PER_STEP
    tl = chunks * CHUNK
    blk = lambda w: pl.BlockSpec((1, tl, w), lambda i, j: (i, j, 0))
    return pl.pallas_call(
        functools.partial(_scan_prompt_kernel, chunks=chunks),
        grid=(b, l // tl),
        in_specs=[blk(CONV_DIM), blk(D_INNER), blk(DT_PAD),
                  pl.BlockSpec((1, chunks, 3, DT_PAD, CHUNK), lambda i, j: (i, j, 0, 0, 0)),
                  _resident((1, D_INNER)), _resident((1, D_INNER))],
        out_specs=[blk(D_INNER), pl.BlockSpec((1, D_INNER, D_STATE), lambda i, j: (i, 0, 0))],
        out_shape=[jax.ShapeDtypeStruct((b, l, D_INNER), BF16),
                   jax.ShapeDtypeStruct((b, D_INNER, D_STATE), F32)],
        scratch_shapes=[pltpu.VMEM((GROUP_DIM, D_STATE), F32)] * N_GROUPS,
        compiler_params=_params(2),
        name="ssd_scan_prompt",
    )(xbc, z, acum, tr, d_e, ng)


SAMPLE_LEN = 8
SCAN_NB = 16
SCAN_ROWS = SCAN_NB * SAMPLE_LEN
SCAN_STEP_NB = 8


def _scan_sample_kernel(xbc_ref, z_ref, dt_ref, h0_ref, apad_ref, ae_ref, de_ref, ng_ref,
                        gsum_ref, gexp_ref, y_ref, hout_ref,
                        wt_ref, ea_ref, ac_ref, xdt_ref, acp_ref, p_ref):
    rows = SCAN_ROWS
    srows = SCAN_STEP_NB * SAMPLE_LEN
    j = pl.program_id(1)

    def bcast_token(v, s):
        n, w = v.shape[0] // SAMPLE_LEN, v.shape[-1]
        v3 = v.reshape(n, SAMPLE_LEN, w)
        return jnp.broadcast_to(v3[:, s:s + 1, :], (n, SAMPLE_LEN, w)).reshape(n * SAMPLE_LEN, w)

    @pl.when(j == 0)
    def _():
        tok = lax.broadcasted_iota(jnp.int32, (rows, D_INNER), 0) & (SAMPLE_LEN - 1)
        tok_p = lax.broadcasted_iota(jnp.int32, (rows, DT_PAD), 0) & (SAMPLE_LEN - 1)

        def cumsum_tokens(v, t):
            for sh in (1, 2, 4):
                v = v + jnp.where(t >= sh, pltpu.roll(v, sh, 0), 0.0)
            return v

        xs = xbc_ref[:, :, 0:D_INNER].reshape(rows, D_INNER)
        dt_p = dt_ref[:, :, 0:DT_PAD].reshape(rows, DT_PAD)
        dt_e = dt_ref[:, :, DT_PAD:DT_PAD + D_INNER].reshape(rows, D_INNER)
        acp_ref[...] = cumsum_tokens(dt_p * apad_ref[...], tok_p)
        acum_e = cumsum_tokens(dt_e * ae_ref[...], tok)
        xdt = xs * dt_e
        w = xdt * jnp.exp(bcast_token(acum_e, SAMPLE_LEN - 1) - acum_e)
        wt_ref[...] = w.T.astype(BF16)
        ea_ref[...] = jnp.exp(acum_e)
        ac_ref[...] = acum_e
        xdt_ref[...] = xdt

    seq0 = j * SCAN_STEP_NB
    step_rows = pl.ds(pl.multiple_of(seq0 * SAMPLE_LEN, srows), srows)
    step_seqs = pl.ds(seq0, SCAN_STEP_NB)
    tok = lax.broadcasted_iota(jnp.int32, (srows, D_INNER), 0) & (SAMPLE_LEN - 1)
    xs = xbc_ref[step_seqs, :, 0:D_INNER].reshape(srows, D_INNER)
    bm = xbc_ref[step_seqs, :, B_OFF:C_OFF].reshape(srows, N_GROUPS * D_STATE)
    cm = xbc_ref[step_seqs, :, C_OFF:CONV_DIM].reshape(srows, N_GROUPS * D_STATE)
    acum_e = ac_ref[step_rows, :]
    xdt = xdt_ref[step_rows, :]
    for s in range(SAMPLE_LEN):
        p_ref[s * srows:(s + 1) * srows, :] = (cm * bcast_token(bm, s)).astype(BF16)
    cb_sum = _dot(p_ref[...], gsum_ref[...])
    cb_e = _dot(cb_sum.astype(BF16), gexp_ref[...])
    yd = de_ref[...] * xs
    for s in range(SAMPLE_LEN):
        diff = acum_e - bcast_token(acum_e, s)
        decay = jnp.exp(jnp.where(tok >= s, diff, -jnp.inf))
        yd = yd + cb_e[s * srows:(s + 1) * srows, :] * decay * bcast_token(xdt, s)

    eye = (lax.broadcasted_iota(jnp.int32, (N_HEADS, DT_PAD), 0)
           == lax.broadcasted_iota(jnp.int32, (N_HEADS, DT_PAD), 1))
    rowid = lax.broadcasted_iota(jnp.int32, (rows, D_STATE), 0)
    for t in range(SCAN_STEP_NB):
        seq = seq0 + t
        r0 = pl.multiple_of(seq * SAMPLE_LEN, SAMPLE_LEN)
        alast = acp_ref[pl.ds(r0 + SAMPLE_LEN - 1, 1), :]
        alast_col = jnp.sum(jnp.where(eye, jnp.broadcast_to(alast, (N_HEADS, DT_PAD)), 0.0),
                            axis=1, keepdims=True)
        cd_b = jnp.broadcast_to(jnp.exp(alast_col), (N_HEADS, D_STATE))
        mine = (rowid >= r0) & (rowid < r0 + SAMPLE_LEN)
        y_offs = []
        for g in range(N_GROUPS):
            gs = slice(g * GROUP_DIM, (g + 1) * GROUP_DIM)
            hg = h0_ref[t, gs, :]
            cg = xbc_ref[seq, :, C_OFF + g * D_STATE:C_OFF + (g + 1) * D_STATE].astype(BF16)
            y_offs.append(_dot_nt(cg, hg.astype(BF16)))
            b_all = xbc_ref[:, :, B_OFF + g * D_STATE:B_OFF + (g + 1) * D_STATE].reshape(
                rows, D_STATE)
            b_mine = jnp.where(mine, b_all, 0.0).astype(BF16)
            hout_ref[t, gs, :] = hg * _head_rows(cd_b, g) + _dot(wt_ref[gs, :], b_mine)
        y = yd[t * SAMPLE_LEN:(t + 1) * SAMPLE_LEN, :] + jnp.concatenate(y_offs, axis=1) * ea_ref[
            pl.ds(r0, SAMPLE_LEN), :]
        z = z_ref[seq]
        y_ref[seq] = jnp.concatenate(
            [_gate_norm(y[:, g * GROUP_DIM:(g + 1) * GROUP_DIM],
                        z[:, g * GROUP_DIM:(g + 1) * GROUP_DIM],
                        ng_ref[:, g * GROUP_DIM:(g + 1) * GROUP_DIM]) for g in range(N_GROUPS)],
            axis=1)


def _scan_sample(xbc, z, dt, h0, a_pad, a_e, d_e, ng, gsum, gexp):
    b = xbc.shape[0]
    dt_w = dt.shape[-1]
    blk = lambda w: pl.BlockSpec((SCAN_NB, SAMPLE_LEN, w), lambda i, j: (i, 0, 0))
    steps = SCAN_NB // SCAN_STEP_NB
    st = pl.BlockSpec((SCAN_STEP_NB, D_INNER, D_STATE), lambda i, j: (i * steps + j, 0, 0))
    return pl.pallas_call(
        _scan_sample_kernel,
        grid=(b // SCAN_NB, steps),
        in_specs=[blk(CONV_DIM), blk(D_INNER), blk(dt_w), st, _resident((1, DT_PAD)),
                  _resident((1, D_INNER)), _resident((1, D_INNER)), _resident((1, D_INNER)),
                  _resident(gsum.shape), _resident(gexp.shape)],
        out_specs=[blk(D_INNER), st],
        out_shape=[jax.ShapeDtypeStruct((b, SAMPLE_LEN, D_INNER), F32),
                   jax.ShapeDtypeStruct((b, D_INNER, D_STATE), F32)],
        scratch_shapes=[pltpu.VMEM((D_INNER, SCAN_ROWS), BF16),
                        pltpu.VMEM((SCAN_ROWS, D_INNER), F32),
                        pltpu.VMEM((SCAN_ROWS, D_INNER), F32),
                        pltpu.VMEM((SCAN_ROWS, D_INNER), F32),
                        pltpu.VMEM((SCAN_ROWS, DT_PAD), F32),
                        pltpu.VMEM((SAMPLE_LEN * SCAN_STEP_NB * SAMPLE_LEN, N_GROUPS * D_STATE),
                                   BF16)],
        compiler_params=_params(2),
        name="ssd_scan_sample",
    )(xbc, z, dt, h0, a_pad, a_e, d_e, ng, gsum, gexp)


POOL_HALO = 2 * SUBLANES


def _pool_kernel(x_ref, g_ref, win_ref, wgrp_ref, scale_ref, buf0_ref,
                 o_ref, bufnew_ref, ext_ref, *, nb, tl, pos0):
    rows = nb * tl
    jt = pl.program_id(1)

    @pl.when(jt == 0)
    def _():
        ext_ref[:, 0:POOL_HALO, :] = jnp.zeros((nb, POOL_HALO, D_MODEL), F32)
        ext_ref[:, POOL_HALO - (MAX_WIN - 1):POOL_HALO, :] = buf0_ref[...]

    x = x_ref[...]
    h = _rmsnorm(x, g_ref[...]).astype(BF16)
    ext_ref[:, POOL_HALO:POOL_HALO + tl, :] = _dot(h, win_ref[...]).reshape(nb, tl, D_MODEL)
    nblk = tl // SUBLANES
    halo_blocks = POOL_HALO // SUBLANES
    shape4 = (nb, nblk, SUBLANES, POOL_GROUP_DIM)
    pos = (pos0 + jt * tl + SUBLANES * lax.broadcasted_iota(jnp.int32, shape4, 1)
           + lax.broadcasted_iota(jnp.int32, shape4, 2)).astype(F32)
    for k, w in enumerate(POOL_WINDOWS):
        sl = slice(k * POOL_GROUP_DIM, (k + 1) * POOL_GROUP_DIM)
        ext = ext_ref[:, :, sl].reshape(nb, nblk + halo_blocks, SUBLANES, POOL_GROUP_DIM)
        tot, shift = ext, 1
        while shift < w:
            tot = tot + _shift_rows(tot, shift)
            shift *= 2
        u = ext[:, halo_blocks:]
        mean = tot[:, halo_blocks:] / jnp.minimum(jnp.float32(w), pos + 1.0)
        m = (mean - u).reshape(rows, POOL_GROUP_DIM).astype(BF16)
        o_ref[:, sl] = (_dot(m, wgrp_ref[k]) * scale_ref[:, sl]).astype(o_ref.dtype)
    bufnew_ref[...] = ext_ref[:, tl + 1:tl + POOL_HALO, :]
    ext_ref[:, 0:POOL_HALO, :] = ext_ref[:, tl:tl + POOL_HALO, :]


def _pool(x2d, row0, b, l, g, w_in, w_grp, scale, buf0, *, nb, tl, pos0):
    assert row0 % (nb * tl) == 0 and (nb == 1 or l == tl)
    tile0, tps = row0 // (nb * tl), l // tl
    return pl.pallas_call(
        functools.partial(_pool_kernel, nb=nb, tl=tl, pos0=pos0),
        grid=(b // nb, tps),
        in_specs=[pl.BlockSpec((nb * tl, D_MODEL), lambda i, j: (tile0 + i * tps + j, 0)),
                  _resident((1, D_MODEL)), _resident((D_MODEL, D_MODEL)),
                  _resident(w_grp.shape), _resident((1, D_MODEL)),
                  pl.BlockSpec((nb, MAX_WIN - 1, D_MODEL), lambda i, j: (i, 0, 0))],
        out_specs=[pl.BlockSpec((nb * tl, D_MODEL), lambda i, j: (i * tps + j, 0)),
                   pl.BlockSpec((nb, MAX_WIN - 1, D_MODEL), lambda i, j: (i, 0, 0))],
        out_shape=[jax.ShapeDtypeStruct((b * l, D_MODEL), BF16),
                   jax.ShapeDtypeStruct((b, MAX_WIN - 1, D_MODEL), F32)],
        scratch_shapes=[pltpu.VMEM((nb, POOL_HALO + tl, D_MODEL), F32)],
        compiler_params=_params(2),
        name="pool_mixer",
    )(x2d, g.reshape(1, D_MODEL), w_in, w_grp, scale.reshape(1, D_MODEL), buf0)


def _expand_heads(v):
    return jnp.repeat(v.astype(F32), HEAD_DIM).reshape(1, D_INNER)


def _pad_heads(v):
    return jnp.pad(v.astype(F32), (0, DT_PAD - N_HEADS)).reshape(1, DT_PAD)


def _group_sum_matrix():
    m = np.zeros((N_GROUPS * D_STATE, LANES), np.float32)
    for g in range(N_GROUPS):
        m[g * D_STATE:(g + 1) * D_STATE, g] = 1.0
    return jnp.asarray(m, BF16)


def _group_expand_matrix():
    m = np.zeros((LANES, D_INNER), np.float32)
    for g in range(N_GROUPS):
        m[g, g * GROUP_DIM:(g + 1) * GROUP_DIM] = 1.0
    return jnp.asarray(m, BF16)


def _trunks(x_prompt, x_sample, ssm0_s, conv0_s, pool0_s, p):
    bp, lp, _ = x_prompt.shape
    bs, ls, _ = x_sample.shape
    rows = (bp * lp, bs * ls)
    bf = lambda a: a.astype(BF16)
    w_gate, w_up, w_down = p["ffn_w_gate"], p["ffn_w_up"], p["ffn_w_down"]
    ffn = lambda xs, i, k, **kw: _ffn(xs, rows, p["ffn_norm"][i, k], w_gate, w_up, w_down, (i, k), **kw)
    x2d = ffn([x_prompt.reshape(rows[0], D_MODEL), x_sample.reshape(rows[1], D_MODEL)], 0, 0)

    w_in = p["ssd_w_in"][0]
    w_dt = w_in[:, D_INNER + CONV_DIM:]
    dt_bias = p["ssd_dt_bias"][0]
    wdt = jnp.pad(w_dt, ((0, 0), (0, DT_PAD - N_HEADS)))
    dtb = jnp.pad(dt_bias, (0, DT_PAD - N_HEADS))
    wdt_s = jnp.concatenate([wdt, jnp.repeat(w_dt, HEAD_DIM, axis=1)], axis=1)
    dtb_s = jnp.concatenate([dtb, jnp.repeat(dt_bias, HEAD_DIM)])
    a_neg = -jnp.exp(p["ssd_a_log"][0].astype(F32))
    a_pad, d_e = _pad_heads(a_neg), _expand_heads(p["ssd_d"][0])
    ng = p["ssd_norm"][0].reshape(1, D_INNER)
    common = (p["mix_norm"][0], bf(w_in))
    conv_w, conv_b = p["ssd_conv_w"][0], p["ssd_conv_b"][0]
    z, acum, tr, xbc, conv_p = _ssd_in_prompt(x2d, bp, lp, *common, bf(wdt), conv_w, conv_b, dtb,
                                              jnp.zeros((bp, CONV_W - 1, CONV_DIM), F32), a_pad)
    y_p, ssm_p = _scan_prompt(xbc, z, acum, tr, d_e, ng)
    z, xbc, dt, conv_s = _ssd_in(x2d, rows[0], bs, ls, *common, bf(wdt_s), conv_w, conv_b, dtb_s,
                                 conv0_s, nb=32, tl=SAMPLE_LEN)
    y_s, ssm_s = _scan_sample(xbc, z, dt, ssm0_s.reshape(bs, D_INNER, D_STATE), a_pad,
                              _expand_heads(a_neg), d_e, ng, _group_sum_matrix(),
                              _group_expand_matrix())
    x2d = ffn([x2d], 0, 1, pre=([y_p.reshape(rows[0], D_INNER), bf(y_s).reshape(rows[1], D_INNER)],
                                bf(p["ssd_w_out"][0])))

    x2d = ffn([x2d], 1, 0)
    pool_w = (p["mix_norm"][1], bf(p["pool_w_in"][0]), bf(p["pool_w_group"][0]), p["pool_scale"][0])
    xm_p, pool_p = _pool(x2d, 0, bp, lp, *pool_w, jnp.zeros((bp, MAX_WIN - 1, D_MODEL), F32),
                         nb=1, tl=1024, pos0=0)
    xm_s, pool_s = _pool(x2d, rows[0], bs, ls, *pool_w, pool0_s, nb=64, tl=SAMPLE_LEN, pos0=PAST_LEN)
    out_p, out_s = ffn([x2d], 1, 1, final_g=p["final_norm"], split_out=True,
                       pre=([xm_p, xm_s], bf(p["pool_w_out"][0])))
    state = lambda a, b: a.reshape(b, N_HEADS, HEAD_DIM, D_STATE)[None]
    return (out_p.reshape(bp, lp, D_MODEL), out_s.reshape(bs, ls, D_MODEL),
            state(ssm_p, bp), conv_p[None], pool_p[None], state(ssm_s, bs), conv_s[None], pool_s[None])


def kernel(x_prompt, x_sample, state_ssm, state_conv, state_pool, ffn_norm, ffn_w_gate, ffn_w_up,
           ffn_w_down, mix_norm, ssd_w_in, ssd_conv_w, ssd_conv_b, ssd_dt_bias, ssd_a_log, ssd_d,
           ssd_norm, ssd_w_out, pool_w_in, pool_w_group, pool_scale, pool_w_out, final_norm):
    p = dict(ffn_norm=ffn_norm, ffn_w_gate=ffn_w_gate, ffn_w_up=ffn_w_up, ffn_w_down=ffn_w_down,
             mix_norm=mix_norm, ssd_w_in=ssd_w_in, ssd_conv_w=ssd_conv_w, ssd_conv_b=ssd_conv_b,
             ssd_dt_bias=ssd_dt_bias, ssd_a_log=ssd_a_log, ssd_d=ssd_d, ssd_norm=ssd_norm,
             ssd_w_out=ssd_w_out, pool_w_in=pool_w_in, pool_w_group=pool_w_group,
             pool_scale=pool_scale, pool_w_out=pool_w_out, final_norm=final_norm)
    return _trunks(x_prompt, x_sample, state_ssm[0], state_conv[0], state_pool[0], p)
```
